```python
import jax, jax.numpy as jnp
from jax import lax
import numpy as np

D_MODEL = 1024
BATCH = 8
SEQ = 2048
DEPTH = 2
DEC_BATCH = 32
DEC_SEQ = 1
PAST_LEN = 16384
PAGE_SIZE = 128

N_A_LAYERS = DEPTH // 2
N_B_LAYERS = DEPTH - N_A_LAYERS
N_DENSE = (DEPTH + 1) // 2
N_MOE = DEPTH // 2
POOL_WINDOWS = (2, 4, 8, 16)
N_POOL_GROUPS = len(POOL_WINDOWS)
POOL_GROUP = D_MODEL // N_POOL_GROUPS
POOL_STATE = max(POOL_WINDOWS) - 1
ATTN_GROUPS = ((128, 1), (512, 4), (2048, 16))
N_ATTN_GROUPS = len(ATTN_GROUPS)
N_KV_HEADS = 16
HEAD_DIM = D_MODEL // N_KV_HEADS
N_Q_HEADS = N_ATTN_GROUPS * N_KV_HEADS
MAX_WINDOW = max(w for w, _ in ATTN_GROUPS)
BLK = 128
ROPE_THETA = 10000.0
D_FF = 2816
N_EXPERTS = 8
TOP_K = 2
D_FF_EXPERT = 3584
EPS = 1e-6

kernel_name = "yoco_pool_dilated_attn_step"


def rms_norm(x, g):
    xf = x.astype(jnp.float32)
    y = xf * lax.rsqrt(jnp.mean(xf * xf, axis=-1, keepdims=True) + EPS)
    return (y * g.astype(jnp.float32)).astype(x.dtype)


def ada_mod(c, w, b):
    m = (jax.nn.silu(c) @ w + b).astype(jnp.float32)[:, None, :]
    return jnp.split(m, 3, axis=-1)


def rope(x, pos):
    half = HEAD_DIM // 2
    inv = ROPE_THETA ** (-jnp.arange(half, dtype=jnp.float32) / half)
    ang = pos.astype(jnp.float32)[:, None] * inv[None, :]
    cos = jnp.cos(ang)[None, :, None, :]
    sin = jnp.sin(ang)[None, :, None, :]
    xf = x.astype(jnp.float32)
    x1, x2 = xf[..., :half], xf[..., half:]
    return jnp.concatenate([x1 * cos - x2 * sin, x2 * cos + x1 * sin], axis=-1).astype(x.dtype)


def multiscale_pool(u, prev, pos, w_grp, ls):
    B, T, D = u.shape
    ext = jnp.concatenate([prev.astype(u.dtype), u], axis=1)
    cs = jnp.cumsum(ext.astype(jnp.float32), axis=1)
    cs = jnp.concatenate([jnp.zeros((B, 1, D), jnp.float32), cs], axis=1)
    cs = cs.reshape(B, POOL_STATE + T + 1, N_POOL_GROUPS, POOL_GROUP)
    win = jnp.array(POOL_WINDOWS, dtype=jnp.int32)
    end = POOL_STATE + 1 + jnp.arange(T, dtype=jnp.int32)
    lo_idx = end[:, None] - win[None, :]
    hi = cs[:, end]
    lo = cs[:, lo_idx, jnp.arange(N_POOL_GROUPS)[None, :]]
    cnt = jnp.minimum(win[None, :], pos[:, None] + 1).astype(jnp.float32)
    mean = (hi - lo) / cnt[None, :, :, None]
    tok = ext[:, POOL_STATE:].astype(jnp.float32).reshape(B, T, N_POOL_GROUPS, POOL_GROUP)
    y = jnp.einsum('btgc,gcd->btgd', mean - tok, w_grp.astype(jnp.float32))
    y = y.reshape(B, T, D) * ls.astype(jnp.float32)
    return y.astype(u.dtype), ext[:, -POOL_STATE:]


def dilated_group_prompt(q, k, v, dil, span):
    B, T, H, hd = q.shape
    step = dil * BLK
    Tp = -(-T // step) * step
    nb = Tp // step
    pad = ((0, 0), (0, Tp - T), (0, 0), (0, 0))

    def split(a):
        a = jnp.pad(a, pad).reshape(B, Tp // dil, dil, H, hd).transpose(0, 2, 1, 3, 4)
        return a.reshape(B, dil, nb, BLK, H, hd)

    def with_prev(a):
        prev = jnp.pad(a, ((0, 0), (0, 0), (1, 0), (0, 0), (0, 0), (0, 0)))[:, :, :-1]
        return jnp.concatenate([prev, a], axis=3)

    qb = split(q)
    kc = with_prev(split(k))
    vc = with_prev(split(v))
    s = jnp.einsum('brnqhd,brnkhd->brnhqk', qb, kc,
                   preferred_element_type=jnp.float32) * (HEAD_DIM ** -0.5)
    i = jnp.arange(BLK)[:, None]
    j = jnp.arange(2 * BLK)[None, :]
    dist = i + BLK - j
    band = (dist >= 0) & (dist <= span)
    has_prev = (jnp.arange(nb)[:, None, None] > 0) | (j >= BLK)[None]
    mask = band[None] & has_prev
    s = jnp.where(mask[None, None, :, None], s, -jnp.inf)
    m = jnp.max(s, axis=-1, keepdims=True)
    p = jnp.exp(s - m)
    den = jnp.sum(p, axis=-1, keepdims=True)
    o = jnp.einsum('brnhqk,brnkhd->brnqhd', p, vc.astype(jnp.float32))
    o = o / jnp.swapaxes(den, 3, 4)
    lse = jnp.swapaxes((m + jnp.log(den))[..., 0], 3, 4)
    o = o.reshape(B, dil, Tp // dil, H, hd).transpose(0, 2, 1, 3, 4).reshape(B, Tp, H, hd)[:, :T]
    lse = lse.reshape(B, dil, Tp // dil, H).transpose(0, 2, 1, 3).reshape(B, Tp, H)[:, :T]
    return o, lse


def dilated_group_sample(q, k_all, v_all, n_past, dil, span):
    S = q.shape[1]
    idx = n_past + jnp.arange(S)[:, None] - dil * jnp.arange(span + 1)[None, :]
    valid = idx >= 0
    idx_c = jnp.maximum(idx, 0)
    kg = k_all[:, idx_c]
    vg = v_all[:, idx_c]
    s = jnp.einsum('bshd,bskhd->bshk', q, kg,
                   preferred_element_type=jnp.float32) * (HEAD_DIM ** -0.5)
    s = jnp.where(valid[None, :, None, :], s, -jnp.inf)
    m = jnp.max(s, axis=-1, keepdims=True)
    p = jnp.exp(s - m)
    den = jnp.sum(p, axis=-1, keepdims=True)
    o = jnp.einsum('bshk,bskhd->bshd', p, vg.astype(jnp.float32)) / den
    return o, (m + jnp.log(den))[..., 0]


def dilated_attention(h, pos, k_all, v_all, n_past, w_q, w_o, prompt):
    B, T, _ = h.shape
    q = (h @ w_q).reshape(B, T, N_Q_HEADS, HEAD_DIM)
    q = rope(q, pos).reshape(B, T, N_ATTN_GROUPS, N_KV_HEADS, HEAD_DIM)
    outs, lses = [], []
    for g, (win, dil) in enumerate(ATTN_GROUPS):
        span = win // dil
        if prompt:
            o, l = dilated_group_prompt(q[:, :, g], k_all, v_all, dil, span)
        else:
            o, l = dilated_group_sample(q[:, :, g], k_all, v_all, n_past, dil, span)
        outs.append(o)
        lses.append(l)
    alpha = jax.nn.softmax(jnp.stack(lses, axis=0), axis=0)
    o = jnp.sum(alpha[..., None] * jnp.stack(outs, axis=0), axis=0)
    return o.reshape(B, T, N_KV_HEADS * HEAD_DIM).astype(h.dtype) @ w_o


def swiglu(h, w1, w3, w2):
    return (jax.nn.silu(h @ w1) * (h @ w3)) @ w2


def moe_swiglu(h, w_router, b_router, w1, w3, w2):
    logits = (h @ w_router).astype(jnp.float32) + b_router.astype(jnp.float32)
    top_v, top_i = lax.top_k(logits, TOP_K)
    gates = jax.nn.softmax(top_v, axis=-1)
    dense_gate = jnp.sum(jax.nn.one_hot(top_i, N_EXPERTS, dtype=jnp.float32) * gates[..., None], axis=-2)
    out = jnp.zeros(h.shape, jnp.float32)
    for e in range(N_EXPERTS):
        y = swiglu(h, w1[e], w3[e], w2[e]).astype(jnp.float32)
        out = out + dense_gate[..., e:e + 1] * y
    return out.astype(h.dtype)


def run_trunk(x, c, pos0, pool_prev, k_past, v_past, prompt,
              ada_mix_w, ada_mix_b, g_pre_mix, g_post_mix, pool_w, pool_scale,
              g_kv, w_kv, w_q, w_o, ada_ffn_w, ada_ffn_b, g_pre_ffn, g_post_ffn,
              ffn_w1, ffn_w3, ffn_w2, router_w, router_b, moe_w1, moe_w3, moe_w2):
    T = x.shape[1]
    pos = pos0 + jnp.arange(T, dtype=jnp.int32)
    kv_width = N_KV_HEADS * HEAD_DIM
    new_pool = []
    k_all = v_all = k_new = v_new = None
    n_past = 0
    for layer in range(DEPTH):
        sh, sc, gt = ada_mod(c, ada_mix_w[layer], ada_mix_b[layer])
        hm = (rms_norm(x, g_pre_mix[layer]) * (1.0 + sc) + sh).astype(x.dtype)
        if layer < N_A_LAYERS:
            a = layer
            mix, st = multiscale_pool(hm, pool_prev[:, a], pos, pool_w[a], pool_scale[a])
            new_pool.append(st)
        else:
            b = layer - N_A_LAYERS
            mix = dilated_attention(hm, pos, k_all, v_all, n_past, w_q[b], w_o[b], prompt)
        x = x + (gt * rms_norm(mix, g_post_mix[layer])).astype(x.dtype)
        sh, sc, gt = ada_mod(c, ada_ffn_w[layer], ada_ffn_b[layer])
        hf = (rms_norm(x, g_pre_ffn[layer]) * (1.0 + sc) + sh).astype(x.dtype)
        if layer % 2 == 0:
            i = layer // 2
            f = swiglu(hf, ffn_w1[i], ffn_w3[i], ffn_w2[i])
        else:
            i = layer // 2
            f = moe_swiglu(hf, router_w[i], router_b[i], moe_w1[i], moe_w3[i], moe_w2[i])
        x = x + (gt * rms_norm(f, g_post_ffn[layer])).astype(x.dtype)
        if layer == N_A_LAYERS - 1:
            kv = rms_norm(x, g_kv) @ w_kv
            B = x.shape[0]
            k_new = rope(kv[..., :kv_width].reshape(B, T, N_KV_HEADS, HEAD_DIM), pos)
            v_new = kv[..., kv_width:].reshape(B, T, N_KV_HEADS, HEAD_DIM)
            if prompt:
                k_all, v_all, n_past = k_new, v_new, 0
            else:
                n_past = k_past.shape[1]
                k_all = jnp.concatenate([k_past.astype(k_new.dtype), k_new], axis=1)
                v_all = jnp.concatenate([v_past.astype(v_new.dtype), v_new], axis=1)
    pool_state = jnp.stack(new_pool, axis=1)
    if prompt:
        n_keep = min(MAX_WINDOW, T)
        k_new, v_new = k_new[:, T - n_keep:], v_new[:, T - n_keep:]
    return x, pool_state, k_new, v_new


def setup_inputs(seed: int = 0) -> dict:
    key = jax.random.key(seed)
    ks = iter(jax.random.split(key, 48))
    D = D_MODEL

    def nrm(shape, scale=1.0):
        return scale * jax.random.normal(next(ks), shape, jnp.float32)

    def gain(shape):
        return 1.0 + 0.05 * nrm(shape)

    kv_buf = min(MAX_WINDOW, PAST_LEN)
    kv_width = N_KV_HEADS * HEAD_DIM
    return {
        "x_prompt": nrm((BATCH, SEQ, D)),
        "x_sample": nrm((DEC_BATCH, DEC_SEQ, D)),
        "state_pool": nrm((DEC_BATCH, N_A_LAYERS, POOL_STATE, D)),
        "cache_k": nrm((DEC_BATCH, kv_buf, N_KV_HEADS, HEAD_DIM)),
        "cache_v": nrm((DEC_BATCH, kv_buf, N_KV_HEADS, HEAD_DIM)),
        "c_prompt": nrm((BATCH, D)),
        "c_sample": nrm((DEC_BATCH, D)),
        "ada_mix_w": nrm((DEPTH, D, 3 * D), 0.5 * D ** -0.5),
        "ada_mix_b": nrm((DEPTH, 3 * D), 0.02),
        "g_pre_mix": gain((DEPTH, D)),
        "g_post_mix": gain((DEPTH, D)),
        "pool_w": nrm((N_A_LAYERS, N_POOL_GROUPS, POOL_GROUP, POOL_GROUP), POOL_GROUP ** -0.5),
        "pool_scale": 1.0 + 0.1 * nrm((N_A_LAYERS, D)),
        "g_kv": gain((D,)),
        "w_kv": nrm((D, 2 * kv_width), D ** -0.5),
        "w_q": nrm((N_B_LAYERS, D, N_Q_HEADS * HEAD_DIM), D ** -0.5),
        "w_o": nrm((N_B_LAYERS, kv_width, D), kv_width ** -0.5),
        "ada_ffn_w": nrm((DEPTH, D, 3 * D), 0.5 * D ** -0.5),
        "ada_ffn_b": nrm((DEPTH, 3 * D), 0.02),
        "g_pre_ffn": gain((DEPTH, D)),
        "g_post_ffn": gain((DEPTH, D)),
        "ffn_w1": nrm((N_DENSE, D, D_FF), D ** -0.5),
        "ffn_w3": nrm((N_DENSE, D, D_FF), D ** -0.5),
        "ffn_w2": nrm((N_DENSE, D_FF, D), D_FF ** -0.5),
        "router_w": nrm((N_MOE, D, N_EXPERTS), D ** -0.5),
        "router_b": nrm((N_MOE, N_EXPERTS), 0.01),
        "moe_w1": nrm((N_MOE, N_EXPERTS, D, D_FF_EXPERT), D ** -0.5),
        "moe_w3": nrm((N_MOE, N_EXPERTS, D, D_FF_EXPERT), D ** -0.5),
        "moe_w2": nrm((N_MOE, N_EXPERTS, D_FF_EXPERT, D), D_FF_EXPERT ** -0.5),
    }


def reference(x_prompt, x_sample, state_pool, cache_k, cache_v, c_prompt, c_sample,
              ada_mix_w, ada_mix_b, g_pre_mix, g_post_mix, pool_w, pool_scale,
              g_kv, w_kv, w_q, w_o, ada_ffn_w, ada_ffn_b, g_pre_ffn, g_post_ffn,
              ffn_w1, ffn_w3, ffn_w2, router_w, router_b, moe_w1, moe_w3, moe_w2):
    params = (ada_mix_w, ada_mix_b, g_pre_mix, g_post_mix, pool_w, pool_scale,
              g_kv, w_kv, w_q, w_o, ada_ffn_w, ada_ffn_b, g_pre_ffn, g_post_ffn,
              ffn_w1, ffn_w3, ffn_w2, router_w, router_b, moe_w1, moe_w3, moe_w2)
    pool_zero = jnp.zeros((x_prompt.shape[0], N_A_LAYERS, POOL_STATE, D_MODEL), x_prompt.dtype)
    y_prompt, pool_prompt, k_prompt, v_prompt = run_trunk(
        x_prompt, c_prompt, 0, pool_zero, None, None, True, *params)
    y_sample, pool_sample, k_sample, v_sample = run_trunk(
        x_sample, c_sample, PAST_LEN, state_pool, cache_k, cache_v, False, *params)
    return (y_prompt, y_sample, pool_prompt, pool_sample, k_prompt, v_prompt, k_sample, v_sample)
```

```python
import functools

import jax
import jax.numpy as jnp
import numpy as np
from jax import lax
from jax.experimental import pallas as pl
from jax.experimental.pallas import tpu as pltpu

F32, BF16, I32 = jnp.float32, jnp.bfloat16, jnp.int32
HIGHEST = lax.Precision.HIGHEST

D = 1024
POOL_WINDOWS = (2, 4, 8, 16)
PG = D // len(POOL_WINDOWS)
POOL_STATE = max(POOL_WINDOWS) - 1
HALO = POOL_STATE + 1
ATTN_GROUPS = ((128, 1), (512, 4), (2048, 16))
NH = 16
HD = 64
BLK = 128
ROPE_THETA = 10000.0
N_EXPERTS = 8
EPS = 1e-6
SCALE = HD ** -0.5
LANES = 128
MIB = 1024 * 1024

TT_MIX = 512
TM_FFN = 256
FC_FFN = 256
TM_POST = 512
TB = 512
SUB = 128
NS = TB // SUB
TMX = 1024
FCX = 512
ROW_ALIGN = 16


def _cparams(sem, vmem_mib):
    return pltpu.CompilerParams(dimension_semantics=sem, vmem_limit_bytes=vmem_mib * MIB)


def _rms_scale(x):
    return x * lax.rsqrt(jnp.mean(x * x, axis=-1, keepdims=True) + EPS)


def _split3(m):
    return m[:, :D], m[:, D:2 * D], m[:, 2 * D:]


def _silu(x):
    return x * jax.nn.sigmoid(x)


def _rope_cols(x, cos, sin):
    lane = lax.broadcasted_iota(I32, (x.shape[0], LANES), 1)
    first = (lane % HD) < (HD // 2)
    shared = cos.shape[1] == LANES
    outs = []
    for c in range(x.shape[1] // LANES):
        cols = slice(c * LANES, (c + 1) * LANES)
        xc = x[:, cols]
        swapped = jnp.where(first, pltpu.roll(xc, LANES - HD // 2, 1), pltpu.roll(xc, HD // 2, 1))
        outs.append(xc * (cos if shared else cos[:, cols]) + swapped * (sin if shared else sin[:, cols]))
    return jnp.concatenate(outs, axis=1)


def _head_expand_matrix():
    r = lax.broadcasted_iota(I32, (LANES, D), 0)
    c = lax.broadcasted_iota(I32, (LANES, D), 1)
    return (c // HD == r).astype(F32)


def _head_reduce_matrix():
    r = lax.broadcasted_iota(I32, (D, LANES), 0)
    c = lax.broadcasted_iota(I32, (D, LANES), 1)
    return (r // HD == c).astype(F32)


def _dot_hi(a, b):
    return jnp.dot(a, b, precision=HIGHEST, preferred_element_type=F32)


def _dot(a, b):
    return jnp.dot(a, b, preferred_element_type=F32)


def _ada_kernel(c_ref, w_ref, b_ref, o_ref):
    o_ref[0] = _dot_hi(_silu(c_ref[...]), w_ref[0]) + b_ref[0]


def _ada(c_all, w, b):
    nl, _, n3 = w.shape
    rows = c_all.shape[0]
    cw = 768
    return pl.pallas_call(
        _ada_kernel,
        grid=(nl, n3 // cw),
        in_specs=[pl.BlockSpec((rows, D), lambda l, j: (0, 0)),
                  pl.BlockSpec((1, D, cw), lambda l, j: (l, 0, j)),
                  pl.BlockSpec((1, 1, cw), lambda l, j: (l, 0, j))],
        out_specs=pl.BlockSpec((1, rows, cw), lambda l, j: (l, 0, j)),
        out_shape=jax.ShapeDtypeStruct((nl, rows, n3), F32),
        compiler_params=_cparams(("arbitrary", "arbitrary"), 32),
        name="ada_mod",
    )(c_all, w, b.reshape(nl, 1, n3))


def _mix0_kernel(x_ref, xh_ref, mod_ref, gpre_ref, gpost_ref, pw_ref, ps_ref,
                 x1_ref, pool_ref, ext_ref):
    i = pl.program_id(1)
    tt = x_ref.shape[1]
    sh, sc, gt = _split3(mod_ref[0, 0])
    gpre = gpre_ref[...]
    x = x_ref[0]
    hm = _rms_scale(x) * gpre * (1.0 + sc) + sh
    hh = _rms_scale(xh_ref[0]) * gpre * (1.0 + sc) + sh
    ext_ref[0:HALO, :] = jnp.where(i > 0, hh, 0.0)
    ext_ref[HALO:, :] = hm
    t = i * tt + lax.broadcasted_iota(I32, (tt, 1), 0)
    parts = []
    for g, w in enumerate(POOL_WINDOWS):
        lo = g * PG
        tok = hm[:, lo:lo + PG]
        s = tok
        for j in range(1, w):
            s = s + ext_ref[HALO - j:HALO - j + tt, lo:lo + PG]
        cnt = jnp.minimum(w, t + 1).astype(F32)
        parts.append(_dot((s / cnt - tok).astype(BF16), pw_ref[g]))
    mix = jnp.concatenate(parts, axis=1) * ps_ref[...]
    x1_ref[0] = x + gt * (_rms_scale(mix) * gpost_ref[...])

    @pl.when(i == pl.num_programs(1) - 1)
    def _():
        pool_ref[0] = ext_ref[tt:tt + HALO, :]


def _mix0(x, mods, gpre, gpost, pw_bf, ps):
    b, t, _ = x.shape
    tt = TT_MIX
    hb = tt // HALO
    return pl.pallas_call(
        _mix0_kernel,
        grid=(b, t // tt),
        in_specs=[pl.BlockSpec((1, tt, D), lambda bb, i: (bb, i, 0)),
                  pl.BlockSpec((1, HALO, D), lambda bb, i: (bb, jnp.maximum(i * hb - 1, 0), 0)),
                  pl.BlockSpec((1, 1, 1, 3 * D), lambda bb, i: (0, bb, 0, 0)),
                  pl.BlockSpec((1, D), lambda bb, i: (0, 0)),
                  pl.BlockSpec((1, D), lambda bb, i: (0, 0)),
                  pl.BlockSpec((len(POOL_WINDOWS), PG, PG), lambda bb, i: (0, 0, 0)),
                  pl.BlockSpec((1, D), lambda bb, i: (0, 0))],
        out_specs=[pl.BlockSpec((1, tt, D), lambda bb, i: (bb, i, 0)),
                   pl.BlockSpec((1, HALO, D), lambda bb, i: (bb, 0, 0))],
        out_shape=[jax.ShapeDtypeStruct((b, t, D), F32),
                   jax.ShapeDtypeStruct((b, HALO, D), F32)],
        scratch_shapes=[pltpu.VMEM((tt + HALO, D), F32)],
        compiler_params=_cparams(("arbitrary", "arbitrary"), 40),
        name="mix0",
    )(x, x, mods, gpre, gpost, pw_bf, ps)


def _ffn0_kernel(x1_ref, modf_ref, modm_ref, gpre_ref, gpost_ref, gkv_ref, gpre1_ref,
                 cos_ref, sin_ref, w1_ref, w3_ref, w2_ref, wkv_ref, wq_ref,
                 x2_ref, k_ref, v_ref, kb_ref, vb_ref, qb_ref, acc_ref):
    sh, sc, gt = _split3(modf_ref[0, 0])
    x1 = x1_ref[...]
    hf = (_rms_scale(x1) * gpre_ref[...] * (1.0 + sc) + sh).astype(BF16)
    for c in range(w1_ref.shape[1] // FC_FFN):
        cs = slice(c * FC_FFN, (c + 1) * FC_FFN)
        a = (_silu(_dot(hf, w1_ref[:, cs])) * _dot(hf, w3_ref[:, cs])).astype(BF16)
        contrib = _dot(a, w2_ref[cs, :])
        if c == 0:
            acc_ref[...] = contrib
        else:
            acc_ref[...] += contrib
    x2 = x1 + gt * (_rms_scale(acc_ref[...]) * gpost_ref[...])
    x2_ref[...] = x2
    xn = _rms_scale(x2)
    cos, sin = cos_ref[...], sin_ref[...]
    kv = _dot((xn * gkv_ref[...]).astype(BF16), wkv_ref[...])
    k = _rope_cols(kv[:, :D], cos, sin)
    v = kv[:, D:]
    k_ref[...] = k
    v_ref[...] = v
    kb_ref[...] = k.astype(BF16)
    vb_ref[...] = v.astype(BF16)
    sh1, sc1, _ = _split3(modm_ref[0, 0])
    hm1 = (xn * gpre1_ref[...] * (1.0 + sc1) + sh1).astype(BF16)
    q = _rope_cols(_dot(hm1, wq_ref[...]), cos, sin) * SCALE
    qb_ref[...] = q.astype(BF16)


def _ffn0(x1, mods_ffn, mods_mix, gpre, gpost, gkv, gpre1, cos, sin, w1, w3, w2, wkv, wq, t):
    n = x1.shape[0]
    tm = TM_FFN
    per_b = t // tm
    dff = w1.shape[1]
    row = lambda i: (i, 0)
    fixed = lambda i: (0, 0)
    resident = lambda shape: pl.BlockSpec(shape, fixed, pipeline_mode=pl.Buffered(1))
    vec = pl.BlockSpec((1, D), fixed)
    return pl.pallas_call(
        _ffn0_kernel,
        grid=(n // tm,),
        in_specs=[pl.BlockSpec((tm, D), row),
                  pl.BlockSpec((1, 1, 1, 3 * D), lambda i: (0, i // per_b, 0, 0)),
                  pl.BlockSpec((1, 1, 1, 3 * D), lambda i: (1, i // per_b, 0, 0)),
                  vec, vec, vec, vec,
                  pl.BlockSpec((tm, LANES), lambda i: (i % per_b, 0)),
                  pl.BlockSpec((tm, LANES), lambda i: (i % per_b, 0)),
                  resident((D, dff)), resident((D, dff)), resident((dff, D)),
                  resident((D, 2 * D)), resident((D, 3 * D))],
        out_specs=[pl.BlockSpec((tm, D), row)] * 5 + [pl.BlockSpec((tm, 3 * D), row)],
        out_shape=[jax.ShapeDtypeStruct((n, D), F32)] * 3
        + [jax.ShapeDtypeStruct((n, D), BF16)] * 2 + [jax.ShapeDtypeStruct((n, 3 * D), BF16)],
        scratch_shapes=[pltpu.VMEM((tm, D), F32)],
        compiler_params=_cparams(("arbitrary",), 56),
        name="ffn0_kv_q",
    )(x1, mods_ffn, mods_mix, gpre, gpost, gkv, gpre1, cos, sin, w1, w3, w2, wkv, wq)


def _attn_kernel(*refs, has_prev):
    if has_prev:
        q_ref, kc_ref, vc_ref, kp_ref, vp_ref, o_ref, st_ref = refs
    else:
        q_ref, kc_ref, vc_ref, o_ref, st_ref = refs
    n = pl.program_id(2)
    qi = lax.broadcasted_iota(I32, (BLK, BLK), 0)
    kj = lax.broadcasted_iota(I32, (BLK, BLK), 1)
    cur_ok = kj <= qi
    if has_prev:
        prev_ok = (kj >= qi) & (n > 0)
    st_ref[0] = jnp.zeros((BLK, LANES), F32)
    nt = (((1,), (1,)), ((), ()))
    for h in range(NH):
        hs = slice(h * HD, (h + 1) * HD)
        qh = q_ref[0, :, hs]
        s_c = jnp.where(cur_ok, lax.dot_general(qh, kc_ref[0, :, hs], nt, preferred_element_type=F32), -jnp.inf)
        m = jnp.max(s_c, axis=-1, keepdims=True)
        if has_prev:
            s_p = jnp.where(prev_ok, lax.dot_general(qh, kp_ref[0, :, hs], nt, preferred_element_type=F32), -jnp.inf)
            m = jnp.maximum(m, jnp.max(s_p, axis=-1, keepdims=True))
        p_c = jnp.exp(s_c - m)
        den = jnp.sum(p_c, axis=-1, keepdims=True)
        acc = _dot(p_c.astype(BF16), vc_ref[0, :, hs])
        if has_prev:
            p_p = jnp.exp(s_p - m)
            den = den + jnp.sum(p_p, axis=-1, keepdims=True)
            acc = acc + _dot(p_p.astype(BF16), vp_ref[0, :, hs])
        o_ref[0, :, hs] = acc / den
        st_ref[0, :, h:h + 1] = m + jnp.log(den)


def _attn_group(qb, kb, vb, g, dil, b, t):
    sub = t // dil
    nb = sub // BLK
    has_prev = nb > 1
    qv = qb.reshape(b, sub, dil * 3 * D)
    kv = kb.reshape(b, sub, dil * D)
    vv = vb.reshape(b, sub, dil * D)
    blk = (1, BLK, D)
    cur = lambda bb, r, n: (bb, n, r)
    prev = lambda bb, r, n: (bb, jnp.maximum(n - 1, 0), r)
    in_specs = [pl.BlockSpec(blk, lambda bb, r, n: (bb, n, r * 3 + g)),
                pl.BlockSpec(blk, cur), pl.BlockSpec(blk, cur)]
    args = [qv, kv, vv]
    if has_prev:
        in_specs += [pl.BlockSpec(blk, prev), pl.BlockSpec(blk, prev)]
        args += [kv, vv]
    o, st = pl.pallas_call(
        functools.partial(_attn_kernel, has_prev=has_prev),
        grid=(b, dil, nb),
        in_specs=in_specs,
        out_specs=[pl.BlockSpec(blk, cur), pl.BlockSpec((1, BLK, LANES), cur)],
        out_shape=[jax.ShapeDtypeStruct((b, sub, dil * D), F32),
                   jax.ShapeDtypeStruct((b, sub, dil * LANES), F32)],
        compiler_params=_cparams(("arbitrary", "arbitrary", "arbitrary"), 32),
        name=f"attn_d{dil}",
    )(*args)
    return o.reshape(b * t, D), st.reshape(b * t, LANES)


def _router_top2(logits_t):
    idx = lax.broadcasted_iota(I32, logits_t.shape, 0)
    v1 = jnp.max(logits_t, axis=0, keepdims=True)
    i1 = jnp.min(jnp.where(logits_t == v1, idx, N_EXPERTS), axis=0, keepdims=True)
    m1 = idx == i1
    rest = jnp.where(m1, -jnp.inf, logits_t)
    v2 = jnp.max(rest, axis=0, keepdims=True)
    i2 = jnp.min(jnp.where(rest == v2, idx, N_EXPERTS), axis=0, keepdims=True)
    m2 = idx == i2
    e2 = jnp.exp(v2 - v1)
    den = 1.0 + e2
    return jnp.where(m1, 1.0 / den, 0.0) + jnp.where(m2, e2 / den, 0.0)


def _merge_groups(o_list, lse_list):
    mx = jnp.maximum(jnp.maximum(lse_list[0], lse_list[1]), lse_list[2])
    es = [jnp.exp(l - mx) for l in lse_list]
    tot = es[0] + es[1] + es[2]
    expand = _head_expand_matrix()
    out = None
    for o, e in zip(o_list, es):
        term = _dot_hi(e / tot, expand) * o
        out = term if out is None else out + term
    return out


def _post1_kernel(o0_ref, o1_ref, o2_ref, s0_ref, s1_ref, s2_ref, x2_ref, modm_ref, modf_ref,
                  gpost_ref, gpre_ref, wo_ref, rwt_ref, rb_ref,
                  x3_ref, hf_ref, gt_ref, gc_ref):
    o = _merge_groups([o0_ref[...], o1_ref[...], o2_ref[...]],
                      [s0_ref[...], s1_ref[...], s2_ref[...]])
    mix = _dot(o.astype(BF16), wo_ref[...])
    _, _, gt = _split3(modm_ref[0, 0])
    x3 = x2_ref[...] + gt * (_rms_scale(mix) * gpost_ref[...])
    x3_ref[...] = x3
    sh, sc, _ = _split3(modf_ref[0, 0])
    hf = _rms_scale(x3) * gpre_ref[...] * (1.0 + sc) + sh
    hf_ref[...] = hf.astype(BF16)
    logits_t = lax.dot_general(rwt_ref[...], hf, (((1,), (1,)), ((), ())),
                               precision=HIGHEST, preferred_element_type=F32) + rb_ref[...]
    gates_t = _router_top2(logits_t)
    gt_ref[...] = gates_t
    padded = jnp.concatenate([gates_t, jnp.zeros((LANES - N_EXPERTS, gates_t.shape[1]), F32)], axis=0)
    gc_ref[...] = padded.T


def _post1(o_list, st_list, x2, mods_mix, mods_ffn, gpost, gpre, wo_bf, rwt, rb, t):
    n = x2.shape[0]
    tm = TM_POST
    per_b = t // tm
    row = lambda i: (i, 0)
    fixed = lambda i: (0, 0)
    vec = pl.BlockSpec((1, D), fixed)
    tile = pl.BlockSpec((tm, D), row)
    stat = pl.BlockSpec((tm, LANES), row)
    return pl.pallas_call(
        _post1_kernel,
        grid=(n // tm,),
        in_specs=[tile, tile, tile, stat, stat, stat, tile,
                  pl.BlockSpec((1, 1, 1, 3 * D), lambda i: (1, i // per_b, 0, 0)),
                  pl.BlockSpec((1, 1, 1, 3 * D), lambda i: (1, i // per_b, 0, 0)),
                  vec, vec,
                  pl.BlockSpec((D, D), fixed),
                  pl.BlockSpec((N_EXPERTS, D), fixed),
                  pl.BlockSpec((N_EXPERTS, 1), fixed)],
        out_specs=[tile, tile, pl.BlockSpec((N_EXPERTS, tm), lambda i: (0, i)), stat],
        out_shape=[jax.ShapeDtypeStruct((n, D), F32), jax.ShapeDtypeStruct((n, D), BF16),
                   jax.ShapeDtypeStruct((N_EXPERTS, n), F32), jax.ShapeDtypeStruct((n, LANES), F32)],
        compiler_params=_cparams(("arbitrary",), 56),
        name="merge_wo_router",
    )(*o_list, *st_list, x2, mods_mix, mods_ffn, gpost, gpre, wo_bf, rwt, rb)


def _route_plan(gates_t, n):
    nb = n // TB
    cnt = jnp.sum((gates_t > 0.0).reshape(N_EXPERTS, nb, TB), axis=-1).astype(I32)
    c_al = (cnt + ROW_ALIGN - 1) // ROW_ALIGN * ROW_ALIGN
    within = jnp.cumsum(c_al, axis=1) - c_al
    tot = jnp.sum(c_al, axis=1)
    region = (tot + SUB + TMX - 1) // TMX * TMX
    start = jnp.cumsum(region) - region
    off = (start[:, None] + within).T.reshape(-1)
    n_used = (jnp.sum(region) // TMX).astype(I32)
    tile_row = jnp.arange(_max_sorted_rows(n) // TMX, dtype=I32) * TMX
    tile_e = jnp.sum(tile_row[:, None] >= (start + region)[None, :], axis=1).astype(I32)
    tile_e = jnp.minimum(tile_e, N_EXPERTS - 1)
    last_e = tile_e[jnp.maximum(n_used - 1, 0)]
    tile_e = jnp.where(jnp.arange(tile_e.shape[0]) < n_used, tile_e, last_e)
    return off.astype(I32), cnt.T.reshape(-1), tile_e, n_used.reshape(1)


def _max_sorted_rows(n):
    nb = n // TB
    worst = 2 * n + nb * N_EXPERTS * (ROW_ALIGN - 1) + N_EXPERTS * (SUB + TMX - 1)
    return (worst + TMX - 1) // TMX * TMX


def _dispatch_kernel(off_ref, cnt_ref, hf_ref, gt_ref, xs_in_ref, xs_ref, stage_ref, sem):
    del xs_in_ref
    b = pl.program_id(0)
    hf = hf_ref[...]
    sel = gt_ref[...] > 0.0
    before = lax.broadcasted_iota(I32, (TB, TB), 0) < lax.broadcasted_iota(I32, (TB, TB), 1)
    rank = _dot(sel.astype(F32), before.astype(F32)).astype(I32)
    jj = lax.broadcasted_iota(I32, (SUB, TB), 0)

    def copy(e, s):
        dst = pl.multiple_of(off_ref[b * N_EXPERTS + e] + s * SUB, ROW_ALIGN)
        return pltpu.make_async_copy(stage_ref.at[e * NS + s], xs_ref.at[pl.ds(dst, SUB)], sem.at[e * NS + s])

    for e in range(N_EXPERTS):
        c = cnt_ref[b * N_EXPERTS + e]
        for s in range(NS):
            @pl.when(s * SUB < c)
            def _(e=e, s=s):
                pick = ((rank[e:e + 1, :] == jj + s * SUB) & sel[e:e + 1, :]).astype(BF16)
                stage_ref[e * NS + s] = _dot(pick, hf).astype(BF16)
                copy(e, s).start()
    for e in range(N_EXPERTS):
        c = cnt_ref[b * N_EXPERTS + e]
        for s in range(NS):
            @pl.when(s * SUB < c)
            def _(e=e, s=s):
                copy(e, s).wait()


def _dispatch(off, cnt, hf_bf, gates_t):
    n = hf_bf.shape[0]
    rows = _max_sorted_rows(n)
    zeros = jnp.zeros((rows, D), BF16)
    return pl.pallas_call(
        _dispatch_kernel,
        grid_spec=pltpu.PrefetchScalarGridSpec(
            num_scalar_prefetch=2,
            grid=(n // TB,),
            in_specs=[pl.BlockSpec((TB, D), lambda b, o, c: (b, 0)),
                      pl.BlockSpec((N_EXPERTS, TB), lambda b, o, c: (0, b)),
                      pl.BlockSpec(memory_space=pl.ANY)],
            out_specs=pl.BlockSpec(memory_space=pl.ANY),
            scratch_shapes=[pltpu.VMEM((N_EXPERTS * NS, SUB, D), BF16),
                            pltpu.SemaphoreType.DMA((N_EXPERTS * NS,))]),
        out_shape=jax.ShapeDtypeStruct((rows, D), BF16),
        input_output_aliases={4: 0},
        compiler_params=_cparams(("arbitrary",), 40),
        name="moe_dispatch",
    )(off, cnt, hf_bf, gates_t, zeros)


def _expert_kernel(te_ref, nu_ref, x_ref, w1_ref, w3_ref, w2_ref, y_ref, acc_ref):
    del te_ref
    i, j = pl.program_id(0), pl.program_id(1)

    @pl.when(i < nu_ref[0])
    def _():
        x = x_ref[...]
        a = (_silu(_dot(x, w1_ref[0].astype(BF16))) * _dot(x, w3_ref[0].astype(BF16))).astype(BF16)
        contrib = _dot(a, w2_ref[0].astype(BF16))

        @pl.when(j == 0)
        def _():
            acc_ref[...] = contrib

        @pl.when(j > 0)
        def _():
            acc_ref[...] += contrib

        @pl.when(j == pl.num_programs(1) - 1)
        def _():
            y_ref[...] = acc_ref[...].astype(BF16)

    @pl.when((i >= nu_ref[0]) & (j == 0))
    def _():
        y_ref[...] = jnp.zeros(y_ref.shape, BF16)


def _experts(tile_e, n_used, xs, w1, w3, w2):
    rows = xs.shape[0]
    dff = w1.shape[2]
    nj = dff // FCX

    def tile_idx(i, j, te, nu):
        return (jnp.minimum(i, nu[0] - 1), 0)

    def chunk(i, j, nu):
        return jnp.where(i < nu[0], j, nj - 1)

    return pl.pallas_call(
        _expert_kernel,
        grid_spec=pltpu.PrefetchScalarGridSpec(
            num_scalar_prefetch=2,
            grid=(rows // TMX, nj),
            in_specs=[pl.BlockSpec((TMX, D), tile_idx),
                      pl.BlockSpec((1, D, FCX), lambda i, j, te, nu: (te[i], 0, chunk(i, j, nu))),
                      pl.BlockSpec((1, D, FCX), lambda i, j, te, nu: (te[i], 0, chunk(i, j, nu))),
                      pl.BlockSpec((1, FCX, D), lambda i, j, te, nu: (te[i], chunk(i, j, nu), 0))],
            out_specs=pl.BlockSpec((TMX, D), lambda i, j, te, nu: (i, 0)),
            scratch_shapes=[pltpu.VMEM((TMX, D), F32)]),
        out_shape=jax.ShapeDtypeStruct((rows, D), BF16),
        compiler_params=_cparams(("arbitrary", "arbitrary"), 56),
        name="moe_experts",
    )(tile_e, n_used, xs, w1, w3, w2)


def _combine_kernel(off_ref, cnt_ref, gc_ref, x3_ref, mod_ref, gpost_ref, ys_ref,
                    out_ref, ybuf_ref, acc_ref, sem):
    b = pl.program_id(0)

    def copy(e, s):
        src = pl.multiple_of(off_ref[b * N_EXPERTS + e] + s * SUB, ROW_ALIGN)
        return pltpu.make_async_copy(ys_ref.at[pl.ds(src, SUB)], ybuf_ref.at[e * NS + s], sem.at[e * NS + s])

    for e in range(N_EXPERTS):
        c = cnt_ref[b * N_EXPERTS + e]
        for s in range(NS):
            @pl.when(s * SUB < c)
            def _(e=e, s=s):
                copy(e, s).start()

    gate = gc_ref[...]
    sel = gate > 0.0
    before = lax.broadcasted_iota(I32, (TB, TB), 1) < lax.broadcasted_iota(I32, (TB, TB), 0)
    rank = _dot(before.astype(F32), sel.astype(F32)).astype(I32)
    lane = lax.broadcasted_iota(I32, (TB, SUB), 1)
    acc_ref[...] = jnp.zeros((TB, D), F32)
    for e in range(N_EXPERTS):
        c = cnt_ref[b * N_EXPERTS + e]
        for s in range(NS):
            @pl.when(s * SUB < c)
            def _(e=e, s=s):
                copy(e, s).wait()
                place = ((rank[:, e:e + 1] == lane + s * SUB) & sel[:, e:e + 1]).astype(BF16)
                acc_ref[...] += gate[:, e:e + 1] * _dot(place, ybuf_ref[e * NS + s])
    _, _, gt = _split3(mod_ref[0, 0])
    out_ref[...] = x3_ref[...] + gt * (_rms_scale(acc_ref[...]) * gpost_ref[...])


def _combine(off, cnt, gates_c, x3, mods_ffn, gpost, ys, t):
    n = x3.shape[0]
    per_b = t // TB
    return pl.pallas_call(
        _combine_kernel,
        grid_spec=pltpu.PrefetchScalarGridSpec(
            num_scalar_prefetch=2,
            grid=(n // TB,),
            in_specs=[pl.BlockSpec((TB, LANES), lambda b, o, c: (b, 0)),
                      pl.BlockSpec((TB, D), lambda b, o, c: (b, 0)),
                      pl.BlockSpec((1, 1, 1, 3 * D), lambda b, o, c: (1, b // per_b, 0, 0)),
                      pl.BlockSpec((1, D), lambda b, o, c: (0, 0)),
                      pl.BlockSpec(memory_space=pl.ANY)],
            out_specs=pl.BlockSpec((TB, D), lambda b, o, c: (b, 0)),
            scratch_shapes=[pltpu.VMEM((N_EXPERTS * NS, SUB, D), BF16),
                            pltpu.VMEM((TB, D), F32),
                            pltpu.SemaphoreType.DMA((N_EXPERTS * NS,))]),
        out_shape=jax.ShapeDtypeStruct((n, D), F32),
        compiler_params=_cparams(("arbitrary",), 40),
        name="moe_combine",
    )(off, cnt, gates_c, x3, mods_ffn, gpost, ys)


def _s_mix0_kernel(x_ref, st_ref, modm_ref, modf_ref, gpre_ref, gpost_ref, gpref_ref, pw_ref, ps_ref,
                   x1_ref, pool_ref, hf_ref, *, pos):
    sh, sc, gt = _split3(modm_ref[...])
    x = x_ref[...]
    hm = _rms_scale(x) * gpre_ref[...] * (1.0 + sc) + sh
    parts = []
    for g, w in enumerate(POOL_WINDOWS):
        lo = g * PG
        tok = hm[:, lo:lo + PG]
        s = tok
        for j in range(1, w):
            s = s + st_ref[POOL_STATE - j, :, lo:lo + PG]
        cnt = float(min(w, pos + 1))
        parts.append(_dot_hi(s / cnt - tok, pw_ref[g]))
    mix = jnp.concatenate(parts, axis=1) * ps_ref[...]
    x1 = x + gt * (_rms_scale(mix) * gpost_ref[...])
    x1_ref[...] = x1
    for j in range(POOL_STATE - 1):
        pool_ref[j] = st_ref[j + 1]
    pool_ref[POOL_STATE - 1] = hm
    shf, scf, _ = _split3(modf_ref[...])
    hf_ref[...] = _rms_scale(x1) * gpref_ref[...] * (1.0 + scf) + shf


def _s_mix0(x_s, state_t, modm, modf, gpre, gpost, gpref, pw, ps, pos):
    rows = x_s.shape[0]
    return pl.pallas_call(
        functools.partial(_s_mix0_kernel, pos=pos),
        out_shape=[jax.ShapeDtypeStruct((rows, D), F32),
                   jax.ShapeDtypeStruct((POOL_STATE, rows, D), F32),
                   jax.ShapeDtypeStruct((rows, D), F32)],
        compiler_params=pltpu.CompilerParams(vmem_limit_bytes=40 * MIB),
        name="s_mix0",
    )(x_s, state_t, modm, modf, gpre, gpost, gpref, pw, ps)


def _s_ffn_kernel(h_ref, w1_ref, w3_ref, w2_ref, f_ref):
    j = pl.program_id(0)
    h = h_ref[...]
    contrib = _dot_hi(_silu(_dot_hi(h, w1_ref[...])) * _dot_hi(h, w3_ref[...]), w2_ref[...])

    @pl.when(j == 0)
    def _():
        f_ref[...] = contrib

    @pl.when(j > 0)
    def _():
        f_ref[...] += contrib


def _s_ffn(h, w1, w3, w2):
    rows = h.shape[0]
    dff = w1.shape[1]
    fc = 256
    return pl.pallas_call(
        _s_ffn_kernel,
        grid=(dff // fc,),
        in_specs=[pl.BlockSpec((rows, D), lambda j: (0, 0)),
                  pl.BlockSpec((D, fc), lambda j: (0, j)),
                  pl.BlockSpec((D, fc), lambda j: (0, j)),
                  pl.BlockSpec((fc, D), lambda j: (j, 0))],
        out_specs=pl.BlockSpec((rows, D), lambda j: (0, 0)),
        out_shape=jax.ShapeDtypeStruct((rows, D), F32),
        compiler_params=_cparams(("arbitrary",), 32),
        name="s_ffn0",
    )(h, w1, w3, w2)


def _s_post0_kernel(x1_ref, f_ref, modf_ref, modm_ref, gpost_ref, gkv_ref, gpre1_ref,
                    x2_ref, kvin_ref, hm1_ref):
    _, _, gt = _split3(modf_ref[...])
    x2 = x1_ref[...] + gt * (_rms_scale(f_ref[...]) * gpost_ref[...])
    x2_ref[...] = x2
    xn = _rms_scale(x2)
    kvin_ref[...] = xn * gkv_ref[...]
    sh1, sc1, _ = _split3(modm_ref[...])
    hm1_ref[...] = xn * gpre1_ref[...] * (1.0 + sc1) + sh1


def _s_post0(x1, f, modf, modm, gpost, gkv, gpre1):
    rows = x1.shape[0]
    return pl.pallas_call(
        _s_post0_kernel,
        out_shape=[jax.ShapeDtypeStruct((rows, D), F32)] * 3,
        name="s_post0",
    )(x1, f, modf, modm, gpost, gkv, gpre1)


def _s_linear_kernel(h_ref, w_ref, cos_ref, sin_ref, o_ref):
    o_ref[...] = _rope_cols(_dot_hi(h_ref[...], w_ref[...]), cos_ref[...], sin_ref[...])


def _s_linear_plain_kernel(h_ref, w_ref, o_ref):
    o_ref[...] = _dot_hi(h_ref[...], w_ref[...])


def _s_linear(h, w, rope=None):
    rows, k = h.shape
    nw = w.shape[1]
    cw = 512
    in_specs = [pl.BlockSpec((rows, k), lambda j: (0, 0)), pl.BlockSpec((k, cw), lambda j: (0, j))]
    args = [h, w]
    body = _s_linear_plain_kernel
    if rope is not None:
        body = _s_linear_kernel
        in_specs += [pl.BlockSpec((1, cw), lambda j: (0, j)), pl.BlockSpec((1, cw), lambda j: (0, j))]
        args += [rope[0], rope[1]]
    return pl.pallas_call(
        body,
        grid=(nw // cw,),
        in_specs=in_specs,
        out_specs=pl.BlockSpec((rows, cw), lambda j: (0, j)),
        out_shape=jax.ShapeDtypeStruct((rows, nw), F32),
        compiler_params=_cparams(("arbitrary",), 32),
        name="s_linear",
    )(*args)


def _s_attn_kernel(q_ref, kv_ref, k0_ref, k1_ref, k2_ref, v0_ref, v1_ref, v2_ref, o_ref):
    reduce_m = _head_reduce_matrix()
    expand_m = _head_expand_matrix()
    k_new = kv_ref[0][:, :D]
    v_new = kv_ref[0][:, D:]
    outs, lses = [], []
    for g, (k_ref, v_ref) in enumerate(((k0_ref, v0_ref), (k1_ref, v1_ref), (k2_ref, v2_ref))):
        qg = q_ref[0][:, g * D:(g + 1) * D]
        kc = k_ref[0]
        s_c = _dot_hi(kc * qg, reduce_m) * SCALE
        s_n = _dot_hi(jnp.broadcast_to(k_new * qg, (8, D)), reduce_m)[0:1] * SCALE
        m = jnp.maximum(jnp.max(s_c, axis=0, keepdims=True), s_n)
        p_c = jnp.exp(s_c - m)
        p_n = jnp.exp(s_n - m)
        den = jnp.sum(p_c, axis=0, keepdims=True) + p_n
        pv = jnp.sum(_dot_hi(p_c, expand_m) * v_ref[0], axis=0, keepdims=True)
        pn_full = _dot_hi(jnp.broadcast_to(p_n, (8, LANES)), expand_m)[0:1]
        den_full = _dot_hi(jnp.broadcast_to(den, (8, LANES)), expand_m)[0:1]
        outs.append((pv + pn_full * v_new) / den_full)
        lses.append(m + jnp.log(den))
    mx = jnp.maximum(jnp.maximum(lses[0], lses[1]), lses[2])
    es = [jnp.exp(l - mx) for l in lses]
    tot = es[0] + es[1] + es[2]
    out = None
    for o, e in zip(outs, es):
        a_full = _dot_hi(jnp.broadcast_to(e / tot, (8, LANES)), expand_m)[0:1]
        out = a_full * o if out is None else out + a_full * o
    o_ref[0] = out


def _s_attn(q_s, kv_s, cache_k, cache_v):
    rows = q_s.shape[0]
    n_past = cache_k.shape[1]
    specs, args = [], []
    for cache in (cache_k, cache_v):
        flat = cache.reshape(rows, n_past, D)
        for win, dil in ATTN_GROUPS:
            span = win // dil
            assert span == BLK and n_past % (dil * BLK) == 0 and n_past >= win
            view = flat.reshape(rows, n_past // dil, dil * D)
            last = n_past // dil // BLK - 1
            specs.append(pl.BlockSpec((1, BLK, D), lambda b, last=last: (b, last, 0)))
            args.append(view)
    return pl.pallas_call(
        _s_attn_kernel,
        grid=(rows,),
        in_specs=[pl.BlockSpec((1, 1, 3 * D), lambda b: (b, 0, 0)),
                  pl.BlockSpec((1, 1, 2 * D), lambda b: (b, 0, 0))] + specs,
        out_specs=pl.BlockSpec((1, 1, D), lambda b: (b, 0, 0)),
        out_shape=jax.ShapeDtypeStruct((rows, 1, D), F32),
        compiler_params=_cparams(("arbitrary",), 32),
        name="s_attn",
    )(q_s.reshape(rows, 1, 3 * D), kv_s.reshape(rows, 1, 2 * D), *args).reshape(rows, D)


def _s_post1_kernel(x2_ref, mix_ref, modm_ref, modf_ref, gpost_ref, gpre_ref, rwt_ref, rb_ref,
                    x3_ref, hf_ref, gc_ref):
    _, _, gt = _split3(modm_ref[...])
    x3 = x2_ref[...] + gt * (_rms_scale(mix_ref[...]) * gpost_ref[...])
    x3_ref[...] = x3
    sh, sc, _ = _split3(modf_ref[...])
    hf = _rms_scale(x3) * gpre_ref[...] * (1.0 + sc) + sh
    hf_ref[...] = hf
    logits_t = lax.dot_general(rwt_ref[...], hf, (((1,), (1,)), ((), ())),
                               precision=HIGHEST, preferred_element_type=F32) + rb_ref[...]
    gc_ref[...] = _router_top2(logits_t)


def _s_post1(x2, mix, modm, modf, gpost, gpre, rwt, rb):
    rows = x2.shape[0]
    return pl.pallas_call(
        _s_post1_kernel,
        out_shape=[jax.ShapeDtypeStruct((rows, D), F32), jax.ShapeDtypeStruct((rows, D), F32),
                   jax.ShapeDtypeStruct((N_EXPERTS, rows), F32)],
        name="s_post1",
    )(x2, mix, modm, modf, gpost, gpre, rwt, rb)


def _s_moe_kernel(h_ref, g_ref, w1_ref, w3_ref, w2_ref, o_ref):
    e, j = pl.program_id(0), pl.program_id(1)
    h = h_ref[...]
    y = _dot_hi(_silu(_dot_hi(h, w1_ref[0])) * _dot_hi(h, w3_ref[0]), w2_ref[0])
    contrib = g_ref[0] * y

    @pl.when((e == 0) & (j == 0))
    def _():
        o_ref[...] = contrib

    @pl.when((e > 0) | (j > 0))
    def _():
        o_ref[...] += contrib


def _s_moe(h, gates_e, w1, w3, w2):
    rows = h.shape[0]
    dff = w1.shape[2]
    return pl.pallas_call(
        _s_moe_kernel,
        grid=(N_EXPERTS, dff // FCX),
        in_specs=[pl.BlockSpec((rows, D), lambda e, j: (0, 0)),
                  pl.BlockSpec((1, rows, 1), lambda e, j: (e, 0, 0)),
                  pl.BlockSpec((1, D, FCX), lambda e, j: (e, 0, j)),
                  pl.BlockSpec((1, D, FCX), lambda e, j: (e, 0, j)),
                  pl.BlockSpec((1, FCX, D), lambda e, j: (e, j, 0))],
        out_specs=pl.BlockSpec((rows, D), lambda e, j: (0, 0)),
        out_shape=jax.ShapeDtypeStruct((rows, D), F32),
        compiler_params=_cparams(("arbitrary", "arbitrary"), 40),
        name="s_moe",
    )(h, gates_e, w1, w3, w2)


def _s_post2_kernel(x3_ref, f_ref, modf_ref, gpost_ref, y_ref):
    _, _, gt = _split3(modf_ref[...])
    y_ref[...] = x3_ref[...] + gt * (_rms_scale(f_ref[...]) * gpost_ref[...])


def _s_post2(x3, f, modf, gpost):
    return pl.pallas_call(
        _s_post2_kernel,
        out_shape=jax.ShapeDtypeStruct(x3.shape, F32),
        name="s_post2",
    )(x3, f, modf, gpost)


def _rope_tables(pos):
    half = HD // 2
    inv = ROPE_THETA ** (-jnp.arange(half, dtype=F32) / half)
    ang = pos.astype(F32)[:, None] * inv[None, :]
    cos, sin = jnp.cos(ang), jnp.sin(ang)
    cos = jnp.tile(cos, (1, LANES // half))
    sin = jnp.tile(jnp.concatenate([-sin, sin], axis=1), (1, LANES // HD))
    return cos, sin


def kernel(x_prompt, x_sample, state_pool, cache_k, cache_v, c_prompt, c_sample, ada_mix_w, ada_mix_b, g_pre_mix, g_post_mix, pool_w, pool_scale, g_kv, w_kv, w_q, w_o, ada_ffn_w, ada_ffn_b, g_pre_ffn, g_post_ffn, ffn_w1, ffn_w3, ffn_w2, router_w, router_b, moe_w1, moe_w3, moe_w2):
    b, t, _ = x_prompt.shape
    nsmp = x_sample.shape[0]
    n = b * t
    past_len = 16384
    assert x_sample.shape[1] == 1 and g_pre_mix.shape[0] == 2 and t % (ATTN_GROUPS[-1][1] * BLK) == 0
    vec = lambda a: a.reshape(1, D)

    c_all = jnp.concatenate([c_prompt, c_sample], axis=0)
    mods_mix = _ada(c_all, ada_mix_w, ada_mix_b)
    mods_ffn = _ada(c_all, ada_ffn_w, ada_ffn_b)
    pm_mix = mods_mix[:, :b].reshape(2, b, 1, 3 * D)
    pm_ffn = mods_ffn[:, :b].reshape(2, b, 1, 3 * D)
    sm_mix = mods_mix[:, b:]
    sm_ffn = mods_ffn[:, b:]

    rwt = router_w[0].T
    rb = router_b[0].reshape(N_EXPERTS, 1)

    x1, pool16 = _mix0(x_prompt, pm_mix, vec(g_pre_mix[0]), vec(g_post_mix[0]),
                       pool_w[0].astype(BF16), vec(pool_scale[0]))
    cos_p, sin_p = _rope_tables(jnp.arange(t, dtype=jnp.int32))
    x2, k_p, v_p, kb, vb, qb = _ffn0(
        x1.reshape(n, D), pm_ffn, pm_mix, vec(g_pre_ffn[0]), vec(g_post_ffn[0]), vec(g_kv), vec(g_pre_mix[1]),
        cos_p, sin_p, ffn_w1[0].astype(BF16), ffn_w3[0].astype(BF16), ffn_w2[0].astype(BF16),
        w_kv.astype(BF16), w_q[0].astype(BF16), t)
    o_list, st_list = [], []
    for g, (_, dil) in enumerate(ATTN_GROUPS):
        o, st = _attn_group(qb.reshape(b, t, 3 * D), kb.reshape(b, t, D), vb.reshape(b, t, D), g, dil, b, t)
        o_list.append(o)
        st_list.append(st)
    x3, hf1, gates_t, gates_c = _post1(o_list, st_list, x2, pm_mix, pm_ffn, vec(g_post_mix[1]),
                                       vec(g_pre_ffn[1]), w_o[0].astype(BF16), rwt, rb, t)
    off, cnt, tile_e, n_used = _route_plan(gates_t, n)
    xs = _dispatch(off, cnt, hf1, gates_t)
    ys = _experts(tile_e, n_used, xs, moe_w1[0], moe_w3[0], moe_w2[0])
    y_prompt = _combine(off, cnt, gates_c, x3, pm_ffn, vec(g_post_ffn[1]), ys, t).reshape(b, t, D)

    state_t = jnp.transpose(state_pool[:, 0], (1, 0, 2))
    x1s, pool_t, hf0s = _s_mix0(x_sample.reshape(nsmp, D), state_t, sm_mix[0], sm_ffn[0],
                                vec(g_pre_mix[0]), vec(g_post_mix[0]), vec(g_pre_ffn[0]),
                                pool_w[0], vec(pool_scale[0]), past_len)
    f0s = _s_ffn(hf0s, ffn_w1[0], ffn_w3[0], ffn_w2[0])
    x2s, kvin_s, hm1s = _s_post0(x1s, f0s, sm_ffn[0], sm_mix[1], vec(g_post_ffn[0]), vec(g_kv), vec(g_pre_mix[1]))
    cos_s, sin_s = _rope_tables(jnp.full((1,), past_len, jnp.int32))
    ones, zeros = jnp.ones((1, D), F32), jnp.zeros((1, D), F32)
    kv_s = _s_linear(kvin_s, w_kv, (jnp.concatenate([jnp.tile(cos_s, (1, D // LANES)), ones], axis=1),
                                    jnp.concatenate([jnp.tile(sin_s, (1, D // LANES)), zeros], axis=1)))
    q_s = _s_linear(hm1s, w_q[0], (jnp.tile(cos_s, (1, 3 * D // LANES)), jnp.tile(sin_s, (1, 3 * D // LANES))))
    attn_s = _s_attn(q_s, kv_s, cache_k, cache_v)
    mix1s = _s_linear(attn_s, w_o[0])
    x3s, hf1s, gts = _s_post1(x2s, mix1s, sm_mix[1], sm_ffn[1], vec(g_post_mix[1]), vec(g_pre_ffn[1]), rwt, rb)
    moe_s = _s_moe(hf1s, gts.reshape(N_EXPERTS, nsmp, 1), moe_w1[0], moe_w3[0], moe_w2[0])
    y_sample = _s_post2(x3s, moe_s, sm_ffn[1], vec(g_post_ffn[1])).reshape(nsmp, 1, D)

    pool_prompt = pool16[:, None, 1:, :]
    pool_sample = jnp.transpose(pool_t, (1, 0, 2))[:, None]
    return (y_prompt, y_sample, pool_prompt, pool_sample,
            k_p.reshape(b, t, NH, HD), v_p.reshape(b, t, NH, HD),
            kv_s[:, :D].reshape(nsmp, 1, NH, HD), kv_s[:, D:].reshape(nsmp, 1, NH, HD))
```

```python
import functools

import jax
import jax.numpy as jnp
import numpy as np
from jax import lax
from jax.experimental import pallas as pl
from jax.experimental.pallas import tpu as pltpu

F32, BF16, I32 = jnp.float32, jnp.bfloat16, jnp.int32
HIGHEST = lax.Precision.HIGHEST

D = 1024
POOL_WINDOWS = (2, 4, 8, 16)
PG = D // len(POOL_WINDOWS)
POOL_STATE = max(POOL_WINDOWS) - 1
HALO = POOL_STATE + 1
ATTN_GROUPS = ((128, 1), (512, 4), (2048, 16))
NH = 16
HD = 64
BLK = 128
ROPE_THETA = 10000.0
N_EXPERTS = 8
EPS = 1e-6
SCALE = HD ** -0.5
LANES = 128
MIB = 1024 * 1024

TT_MIX = 512
TM_FFN = 256
FC_FFN = 256
TM_POST = 512
TB = 512
SUB = 128
NS = TB // SUB
TMX = 1024
FCX = 512
ROW_ALIGN = 16


def _cparams(sem, vmem_mib):
    return pltpu.CompilerParams(dimension_semantics=sem, vmem_limit_bytes=vmem_mib * MIB)


def _rms_scale(x):
    return x * lax.rsqrt(jnp.mean(x * x, axis=-1, keepdims=True) + EPS)


def _split3(m):
    return m[:, :D], m[:, D:2 * D], m[:, 2 * D:]


def _silu(x):
    return x * jax.nn.sigmoid(x)


def _rope_cols(x, cos, sin):
    lane = lax.broadcasted_iota(I32, (x.shape[0], LANES), 1)
    first = (lane % HD) < (HD // 2)
    shared = cos.shape[1] == LANES
    outs = []
    for c in range(x.shape[1] // LANES):
        cols = slice(c * LANES, (c + 1) * LANES)
        xc = x[:, cols]
        swapped = jnp.where(first, pltpu.roll(xc, LANES - HD // 2, 1), pltpu.roll(xc, HD // 2, 1))
        outs.append(xc * (cos if shared else cos[:, cols]) + swapped * (sin if shared else sin[:, cols]))
    return jnp.concatenate(outs, axis=1)


def _dot_hi(a, b):
    return jnp.dot(a, b, precision=HIGHEST, preferred_element_type=F32)


def _dot(a, b):
    return jnp.dot(a, b, preferred_element_type=F32)


def _ada_kernel(c_ref, w_ref, b_ref, o_ref):
    o_ref[0] = _dot_hi(_silu(c_ref[...]), w_ref[0]) + b_ref[0]


def _ada(c_all, w, b):
    nl, _, n3 = w.shape
    rows = c_all.shape[0]
    cw = 768
    return pl.pallas_call(
        _ada_kernel,
        grid=(nl, n3 // cw),
        in_specs=[pl.BlockSpec((rows, D), lambda l, j: (0, 0)),
                  pl.BlockSpec((1, D, cw), lambda l, j: (l, 0, j)),
                  pl.BlockSpec((1, 1, cw), lambda l, j: (l, 0, j))],
        out_specs=pl.BlockSpec((1, rows, cw), lambda l, j: (l, 0, j)),
        out_shape=jax.ShapeDtypeStruct((nl, rows, n3), F32),
        compiler_params=_cparams(("arbitrary", "arbitrary"), 32),
        name="ada_mod",
    )(c_all, w, b.reshape(nl, 1, n3))


def _mix0_kernel(x_ref, xh_ref, mod_ref, gpre_ref, gpost_ref, pw_ref, ps_ref,
                 x1_ref, pool_ref, ext_ref):
    i = pl.program_id(1)
    tt = x_ref.shape[1]
    sh, sc, gt = _split3(mod_ref[0, 0])
    gpre = gpre_ref[...]
    x = x_ref[0]
    hm = _rms_scale(x) * gpre * (1.0 + sc) + sh
    hh = _rms_scale(xh_ref[0]) * gpre * (1.0 + sc) + sh
    ext_ref[0:HALO, :] = jnp.where(i > 0, hh, 0.0)
    ext_ref[HALO:, :] = hm
    t = i * tt + lax.broadcasted_iota(I32, (tt, 1), 0)
    parts = []
    for g, w in enumerate(POOL_WINDOWS):
        lo = g * PG
        tok = hm[:, lo:lo + PG]
        s = tok
        for j in range(1, w):
            s = s + ext_ref[HALO - j:HALO - j + tt, lo:lo + PG]
        cnt = jnp.minimum(w, t + 1).astype(F32)
        parts.append(_dot((s / cnt - tok).astype(BF16), pw_ref[g]))
    mix = jnp.concatenate(parts, axis=1) * ps_ref[...]
    x1_ref[0] = x + gt * (_rms_scale(mix) * gpost_ref[...])

    @pl.when(i == pl.num_programs(1) - 1)
    def _():
        pool_ref[0] = ext_ref[tt:tt + HALO, :]


def _mix0(x, mods, gpre, gpost, pw_bf, ps):
    b, t, _ = x.shape
    tt = TT_MIX
    hb = tt // HALO
    return pl.pallas_call(
        _mix0_kernel,
        grid=(b, t // tt),
        in_specs=[pl.BlockSpec((1, tt, D), lambda bb, i: (bb, i, 0)),
                  pl.BlockSpec((1, HALO, D), lambda bb, i: (bb, jnp.maximum(i * hb - 1, 0), 0)),
                  pl.BlockSpec((1, 1, 1, 3 * D), lambda bb, i: (0, bb, 0, 0)),
                  pl.BlockSpec((1, D), lambda bb, i: (0, 0)),
                  pl.BlockSpec((1, D), lambda bb, i: (0, 0)),
                  pl.BlockSpec((len(POOL_WINDOWS), PG, PG), lambda bb, i: (0, 0, 0)),
                  pl.BlockSpec((1, D), lambda bb, i: (0, 0))],
        out_specs=[pl.BlockSpec((1, tt, D), lambda bb, i: (bb, i, 0)),
                   pl.BlockSpec((1, HALO, D), lambda bb, i: (bb, 0, 0))],
        out_shape=[jax.ShapeDtypeStruct((b, t, D), F32),
                   jax.ShapeDtypeStruct((b, HALO, D), F32)],
        scratch_shapes=[pltpu.VMEM((tt + HALO, D), F32)],
        compiler_params=_cparams(("arbitrary", "arbitrary"), 40),
        name="mix0",
    )(x, x, mods, gpre, gpost, pw_bf, ps)


def _store_dilated(val, plane_ref, dils, out_refs):
    tm = val.shape[0]
    for hp in range(NH // 2):
        plane_ref[hp] = val[:, hp * LANES:(hp + 1) * LANES]
    for dil, out_ref in zip(dils, out_refs):
        for hp in range(NH // 2):
            for r in range(dil):
                out_ref[0, hp, r] = plane_ref[hp, pl.ds(r, tm // dil, stride=dil), :].astype(BF16)


def _dilated_specs(b, t, tm):
    per_b = t // tm
    specs = [pl.BlockSpec((1, NH // 2, dil, tm // dil, LANES), lambda i: (i // per_b, 0, 0, i % per_b, 0))
             for _, dil in ATTN_GROUPS]
    shapes = [jax.ShapeDtypeStruct((b, NH // 2, dil, t // dil, LANES), BF16) for _, dil in ATTN_GROUPS]
    return specs, shapes


def _ffn0_kernel(x1_ref, modf_ref, gpre_ref, gpost_ref, gkv_ref, cos_ref, sin_ref,
                 w1_ref, w3_ref, w2_ref, wkv_ref,
                 x2_ref, k_ref, v_ref, k0_ref, k1_ref, k2_ref, v0_ref, v1_ref, v2_ref, acc_ref, plane_ref):
    sh, sc, gt = _split3(modf_ref[0, 0])
    x1 = x1_ref[...]
    hf = (_rms_scale(x1) * gpre_ref[...] * (1.0 + sc) + sh).astype(BF16)
    for c in range(w1_ref.shape[1] // FC_FFN):
        cs = slice(c * FC_FFN, (c + 1) * FC_FFN)
        a = (_silu(_dot(hf, w1_ref[:, cs])) * _dot(hf, w3_ref[:, cs])).astype(BF16)
        contrib = _dot(a, w2_ref[cs, :])
        if c == 0:
            acc_ref[...] = contrib
        else:
            acc_ref[...] += contrib
    x2 = x1 + gt * (_rms_scale(acc_ref[...]) * gpost_ref[...])
    x2_ref[...] = x2
    kv = _dot((_rms_scale(x2) * gkv_ref[...]).astype(BF16), wkv_ref[...])
    k = _rope_cols(kv[:, :D], cos_ref[...], sin_ref[...])
    v = kv[:, D:]
    k_ref[...] = k
    v_ref[...] = v
    dils = [dil for _, dil in ATTN_GROUPS]
    _store_dilated(k, plane_ref, dils, (k0_ref, k1_ref, k2_ref))
    _store_dilated(v, plane_ref, dils, (v0_ref, v1_ref, v2_ref))


def _ffn0(x1, mods_ffn, gpre, gpost, gkv, cos, sin, w1, w3, w2, wkv, b, t):
    n = x1.shape[0]
    tm = TM_FFN
    per_b = t // tm
    dff = w1.shape[1]
    row = lambda i: (i, 0)
    fixed = lambda i: (0, 0)
    resident = lambda shape: pl.BlockSpec(shape, fixed, pipeline_mode=pl.Buffered(1))
    vec = pl.BlockSpec((1, D), fixed)
    dspecs, dshapes = _dilated_specs(b, t, tm)
    return pl.pallas_call(
        _ffn0_kernel,
        grid=(n // tm,),
        in_specs=[pl.BlockSpec((tm, D), row),
                  pl.BlockSpec((1, 1, 1, 3 * D), lambda i: (0, i // per_b, 0, 0)),
                  vec, vec, vec,
                  pl.BlockSpec((tm, LANES), lambda i: (i % per_b, 0)),
                  pl.BlockSpec((tm, LANES), lambda i: (i % per_b, 0)),
                  resident((D, dff)), resident((D, dff)), resident((dff, D)), resident((D, 2 * D))],
        out_specs=[pl.BlockSpec((tm, D), row)] * 3 + dspecs + dspecs,
        out_shape=[jax.ShapeDtypeStruct((n, D), F32)] * 3 + dshapes + dshapes,
        scratch_shapes=[pltpu.VMEM((tm, D), F32), pltpu.VMEM((NH // 2, tm, LANES), F32)],
        compiler_params=_cparams(("arbitrary",), 52),
        name="ffn0_kv",
    )(x1, mods_ffn, gpre, gpost, gkv, cos, sin, w1, w3, w2, wkv)


def _qproj_kernel(x2_ref, modm_ref, gpre1_ref, cos_ref, sin_ref, wq_ref, q0_ref, q1_ref, q2_ref, scr_ref):
    sh1, sc1, _ = _split3(modm_ref[0, 0])
    hm1 = (_rms_scale(x2_ref[...]) * gpre1_ref[...] * (1.0 + sc1) + sh1).astype(BF16)
    q = _rope_cols(_dot(hm1, wq_ref[...]), cos_ref[...], sin_ref[...]) * SCALE
    for g, ((_, dil), q_ref) in enumerate(zip(ATTN_GROUPS, (q0_ref, q1_ref, q2_ref))):
        _store_dilated(q[:, g * D:(g + 1) * D], scr_ref, (dil,), (q_ref,))


def _qproj(x2, mods_mix, gpre1, cos, sin, wq, b, t):
    n = x2.shape[0]
    tm = TM_POST
    per_b = t // tm
    fixed = lambda i: (0, 0)
    dspecs, dshapes = _dilated_specs(b, t, tm)
    return pl.pallas_call(
        _qproj_kernel,
        grid=(n // tm,),
        in_specs=[pl.BlockSpec((tm, D), lambda i: (i, 0)),
                  pl.BlockSpec((1, 1, 1, 3 * D), lambda i: (1, i // per_b, 0, 0)),
                  pl.BlockSpec((1, D), fixed),
                  pl.BlockSpec((tm, LANES), lambda i: (i % per_b, 0)),
                  pl.BlockSpec((tm, LANES), lambda i: (i % per_b, 0)),
                  pl.BlockSpec((D, 3 * D), fixed, pipeline_mode=pl.Buffered(1))],
        out_specs=dspecs,
        out_shape=dshapes,
        scratch_shapes=[pltpu.VMEM((NH // 2, tm, LANES), F32)],
        compiler_params=_cparams(("arbitrary",), 48),
        name="q_proj",
    )(x2, mods_mix, gpre1, cos, sin, wq)


def _attn_kernel(q0_ref, q1_ref, q2_ref, k0_ref, k1_ref, k2_ref, v0_ref, v1_ref, v2_ref,
                 o_ref, og_ref, lg_ref):
    t = o_ref.shape[2]
    first_head = lax.broadcasted_iota(I32, (BLK, LANES), 1) < HD
    nt = (((1,), (1,)), ((), ()))
    groups = zip(ATTN_GROUPS, (q0_ref, q1_ref, q2_ref), (k0_ref, k1_ref, k2_ref), (v0_ref, v1_ref, v2_ref))
    for g, ((win, dil), q_ref, k_ref, v_ref) in enumerate(groups):
        sub = t // dil
        nb = sub // BLK
        kw = min(2 * BLK, sub)
        span = win // dil
        qi = lax.broadcasted_iota(I32, (2 * BLK, kw), 0) % BLK
        kj = lax.broadcasted_iota(I32, (2 * BLK, kw), 1)

        def unit(u, carry, dil=dil, nb=nb, kw=kw, span=span, qi=qi, kj=kj,
                 q_ref=q_ref, k_ref=k_ref, v_ref=v_ref, g=g):
            r = u // nb
            n = u % nb
            q_lo = pl.multiple_of(n * BLK, BLK)
            k_lo = pl.multiple_of(jnp.maximum(n - 1, 0) * BLK, BLK)
            qp = q_ref[0, 0, r, pl.ds(q_lo, BLK), :]
            zero = jnp.zeros_like(qp)
            qs = jnp.concatenate([jnp.where(first_head, qp, zero), jnp.where(first_head, zero, qp)], axis=0)
            s = lax.dot_general(qs, k_ref[0, 0, r, pl.ds(k_lo, kw), :], nt, preferred_element_type=F32)
            dist = (q_lo + qi) - (k_lo + kj)
            s = jnp.where((dist >= 0) & (dist <= span), s, -jnp.inf)
            m = jnp.max(s, axis=-1, keepdims=True)
            p = jnp.exp(s - m)
            den = jnp.sum(p, axis=-1, keepdims=True)
            o2 = _dot(p.astype(BF16), v_ref[0, 0, r, pl.ds(k_lo, kw), :]) / den
            l2 = jnp.broadcast_to(m + jnp.log(den), (2 * BLK, LANES))
            rows = pl.ds(q_lo * dil + r, BLK, stride=dil) if dil > 1 else pl.ds(q_lo, BLK)
            og_ref[g, rows, :] = jnp.where(first_head, o2[:BLK], o2[BLK:])
            lg_ref[g, rows, :] = jnp.where(first_head, l2[:BLK], l2[BLK:])
            return carry

        lax.fori_loop(0, dil * nb, unit, 0)

    mt = 2 * BLK

    def merge(c, carry):
        rows = pl.ds(pl.multiple_of(c * mt, mt), mt)
        ls = [lg_ref[g, rows, :] for g in range(len(ATTN_GROUPS))]
        mx = jnp.maximum(jnp.maximum(ls[0], ls[1]), ls[2])
        es = [jnp.exp(l - mx) for l in ls]
        tot = es[0] + es[1] + es[2]
        out = (es[0] / tot) * og_ref[0, rows, :]
        for g in range(1, len(ATTN_GROUPS)):
            out = out + (es[g] / tot) * og_ref[g, rows, :]
        o_ref[0, 0, rows, :] = out.astype(BF16)
        return carry

    lax.fori_loop(0, t // mt, merge, 0)


def _attention(q_l, k_l, v_l, b, t):
    specs = [pl.BlockSpec((1, 1, dil, t // dil, LANES), lambda bb, hp: (bb, hp, 0, 0, 0))
             for _, dil in ATTN_GROUPS]
    return pl.pallas_call(
        _attn_kernel,
        grid=(b, NH // 2),
        in_specs=specs * 3,
        out_specs=pl.BlockSpec((1, 1, t, LANES), lambda bb, hp: (bb, hp, 0, 0)),
        out_shape=jax.ShapeDtypeStruct((b, NH // 2, t, LANES), BF16),
        scratch_shapes=[pltpu.VMEM((len(ATTN_GROUPS), t, LANES), F32),
                        pltpu.VMEM((len(ATTN_GROUPS), t, LANES), F32)],
        compiler_params=_cparams(("arbitrary", "arbitrary"), 40),
        name="dilated_attn",
    )(*q_l, *k_l, *v_l)


def _router_top2(logits_t):
    idx = lax.broadcasted_iota(I32, logits_t.shape, 0)
    v1 = jnp.max(logits_t, axis=0, keepdims=True)
    i1 = jnp.min(jnp.where(logits_t == v1, idx, N_EXPERTS), axis=0, keepdims=True)
    m1 = idx == i1
    rest = jnp.where(m1, -jnp.inf, logits_t)
    v2 = jnp.max(rest, axis=0, keepdims=True)
    i2 = jnp.min(jnp.where(rest == v2, idx, N_EXPERTS), axis=0, keepdims=True)
    m2 = idx == i2
    e2 = jnp.exp(v2 - v1)
    den = 1.0 + e2
    return jnp.where(m1, 1.0 / den, 0.0) + jnp.where(m2, e2 / den, 0.0)


def _post1_kernel(a_ref, x2_ref, modm_ref, modf_ref, gpost_ref, gpre_ref, wo_ref, rwt_ref, rb_ref,
                  x3_ref, hf_ref, gt_ref, gc_ref):
    o = jnp.concatenate([a_ref[0, hp] for hp in range(NH // 2)], axis=1)
    mix = _dot(o, wo_ref[...])
    _, _, gt = _split3(modm_ref[0, 0])
    x3 = x2_ref[...] + gt * (_rms_scale(mix) * gpost_ref[...])
    x3_ref[...] = x3
    sh, sc, _ = _split3(modf_ref[0, 0])
    hf = _rms_scale(x3) * gpre_ref[...] * (1.0 + sc) + sh
    hf_ref[...] = hf.astype(BF16)
    logits_t = lax.dot_general(rwt_ref[...], hf, (((1,), (1,)), ((), ())),
                               precision=HIGHEST, preferred_element_type=F32) + rb_ref[...]
    gates_t = _router_top2(logits_t)
    gt_ref[...] = gates_t
    padded = jnp.concatenate([gates_t, jnp.zeros((LANES - N_EXPERTS, gates_t.shape[1]), F32)], axis=0)
    gc_ref[...] = padded.T


def _post1(attn, x2, mods_mix, mods_ffn, gpost, gpre, wo_bf, rwt, rb, t):
    n = x2.shape[0]
    tm = TM_POST
    per_b = t // tm
    row = lambda i: (i, 0)
    fixed = lambda i: (0, 0)
    vec = pl.BlockSpec((1, D), fixed)
    tile = pl.BlockSpec((tm, D), row)
    return pl.pallas_call(
        _post1_kernel,
        grid=(n // tm,),
        in_specs=[pl.BlockSpec((1, NH // 2, tm, LANES), lambda i: (i // per_b, 0, i % per_b, 0)),
                  tile,
                  pl.BlockSpec((1, 1, 1, 3 * D), lambda i: (1, i // per_b, 0, 0)),
                  pl.BlockSpec((1, 1, 1, 3 * D), lambda i: (1, i // per_b, 0, 0)),
                  vec, vec,
                  pl.BlockSpec((D, D), fixed),
                  pl.BlockSpec((N_EXPERTS, D), fixed),
                  pl.BlockSpec((N_EXPERTS, 1), fixed)],
        out_specs=[tile, tile, pl.BlockSpec((N_EXPERTS, tm), lambda i: (0, i)),
                   pl.BlockSpec((tm, LANES), row)],
        out_shape=[jax.ShapeDtypeStruct((n, D), F32), jax.ShapeDtypeStruct((n, D), BF16),
                   jax.ShapeDtypeStruct((N_EXPERTS, n), F32), jax.ShapeDtypeStruct((n, LANES), F32)],
        compiler_params=_cparams(("arbitrary",), 40),
        name="wo_router",
    )(attn, x2, mods_mix, mods_ffn, gpost, gpre, wo_bf, rwt, rb)


def _route_plan(gates_t, n):
    nb = n // TB
    cnt = jnp.sum((gates_t > 0.0).reshape(N_EXPERTS, nb, TB), axis=-1).astype(I32)
    c_al = (cnt + ROW_ALIGN - 1) // ROW_ALIGN * ROW_ALIGN
    within = jnp.cumsum(c_al, axis=1) - c_al
    tot = jnp.sum(c_al, axis=1)
    region = (tot + SUB + TMX - 1) // TMX * TMX
    start = jnp.cumsum(region) - region
    off = (start[:, None] + within).T.reshape(-1)
    n_used = (jnp.sum(region) // TMX).astype(I32)
    tile_row = jnp.arange(_max_sorted_rows(n) // TMX, dtype=I32) * TMX
    tile_e = jnp.sum(tile_row[:, None] >= (start + region)[None, :], axis=1).astype(I32)
    tile_e = jnp.minimum(tile_e, N_EXPERTS - 1)
    last_e = tile_e[jnp.maximum(n_used - 1, 0)]
    tile_e = jnp.where(jnp.arange(tile_e.shape[0]) < n_used, tile_e, last_e)
    return off.astype(I32), cnt.T.reshape(-1), tile_e, n_used.reshape(1)


def _max_sorted_rows(n):
    nb = n // TB
    worst = 2 * n + nb * N_EXPERTS * (ROW_ALIGN - 1) + N_EXPERTS * (SUB + TMX - 1)
    return (worst + TMX - 1) // TMX * TMX


def _dispatch_kernel(off_ref, cnt_ref, hf_ref, gt_ref, xs_in_ref, xs_ref, stage_ref, sem):
    del xs_in_ref
    b = pl.program_id(0)
    hf = hf_ref[...]
    sel = gt_ref[...] > 0.0
    before = lax.broadcasted_iota(I32, (TB, TB), 0) < lax.broadcasted_iota(I32, (TB, TB), 1)
    rank = _dot(sel.astype(F32), before.astype(F32)).astype(I32)
    jj = lax.broadcasted_iota(I32, (SUB, TB), 0)

    def copy(e, s):
        dst = pl.multiple_of(off_ref[b * N_EXPERTS + e] + s * SUB, ROW_ALIGN)
        return pltpu.make_async_copy(stage_ref.at[e * NS + s], xs_ref.at[pl.ds(dst, SUB)], sem.at[e * NS + s])

    for e in range(N_EXPERTS):
        c = cnt_ref[b * N_EXPERTS + e]
        for s in range(NS):
            @pl.when(s * SUB < c)
            def _(e=e, s=s):
                pick = ((rank[e:e + 1, :] == jj + s * SUB) & sel[e:e + 1, :]).astype(BF16)
                stage_ref[e * NS + s] = _dot(pick, hf).astype(BF16)
                copy(e, s).start()
    for e in range(N_EXPERTS):
        c = cnt_ref[b * N_EXPERTS + e]
        for s in range(NS):
            @pl.when(s * SUB < c)
            def _(e=e, s=s):
                copy(e, s).wait()


def _dispatch(off, cnt, hf_bf, gates_t):
    n = hf_bf.shape[0]
    rows = _max_sorted_rows(n)
    zeros = jnp.zeros((rows, D), BF16)
    return pl.pallas_call(
        _dispatch_kernel,
        grid_spec=pltpu.PrefetchScalarGridSpec(
            num_scalar_prefetch=2,
            grid=(n // TB,),
            in_specs=[pl.BlockSpec((TB, D), lambda b, o, c: (b, 0)),
                      pl.BlockSpec((N_EXPERTS, TB), lambda b, o, c: (0, b)),
                      pl.BlockSpec(memory_space=pl.ANY)],
            out_specs=pl.BlockSpec(memory_space=pl.ANY),
            scratch_shapes=[pltpu.VMEM((N_EXPERTS * NS, SUB, D), BF16),
                            pltpu.SemaphoreType.DMA((N_EXPERTS * NS,))]),
        out_shape=jax.ShapeDtypeStruct((rows, D), BF16),
        input_output_aliases={4: 0},
        compiler_params=_cparams(("arbitrary",), 40),
        name="moe_dispatch",
    )(off, cnt, hf_bf, gates_t, zeros)


def _expert_kernel(te_ref, nu_ref, x_ref, w1_ref, w3_ref, w2_ref, y_ref, acc_ref):
    del te_ref
    i, j = pl.program_id(0), pl.program_id(1)

    @pl.when(i < nu_ref[0])
    def _():
        x = x_ref[...]
        a = (_silu(_dot(x, w1_ref[0].astype(BF16))) * _dot(x, w3_ref[0].astype(BF16))).astype(BF16)
        contrib = _dot(a, w2_ref[0].astype(BF16))

        @pl.when(j == 0)
        def _():
            acc_ref[...] = contrib

        @pl.when(j > 0)
        def _():
            acc_ref[...] += contrib

        @pl.when(j == pl.num_programs(1) - 1)
        def _():
            y_ref[...] = acc_ref[...].astype(BF16)

    @pl.when((i >= nu_ref[0]) & (j == 0))
    def _():
        y_ref[...] = jnp.zeros(y_ref.shape, BF16)


def _experts(tile_e, n_used, xs, w1, w3, w2):
    rows = xs.shape[0]
    dff = w1.shape[2]
    nj = dff // FCX

    def tile_idx(i, j, te, nu):
        return (jnp.minimum(i, nu[0] - 1), 0)

    def chunk(i, j, nu):
        return jnp.where(i < nu[0], j, nj - 1)

    return pl.pallas_call(
        _expert_kernel,
        grid_spec=pltpu.PrefetchScalarGridSpec(
            num_scalar_prefetch=2,
            grid=(rows // TMX, nj),
            in_specs=[pl.BlockSpec((TMX, D), tile_idx),
                      pl.BlockSpec((1, D, FCX), lambda i, j, te, nu: (te[i], 0, chunk(i, j, nu))),
                      pl.BlockSpec((1, D, FCX), lambda i, j, te, nu: (te[i], 0, chunk(i, j, nu))),
                      pl.BlockSpec((1, FCX, D), lambda i, j, te, nu: (te[i], chunk(i, j, nu), 0))],
            out_specs=pl.BlockSpec((TMX, D), lambda i, j, te, nu: (i, 0)),
            scratch_shapes=[pltpu.VMEM((TMX, D), F32)]),
        out_shape=jax.ShapeDtypeStruct((rows, D), BF16),
        compiler_params=_cparams(("arbitrary", "arbitrary"), 56),
        name="moe_experts",
    )(tile_e, n_used, xs, w1, w3, w2)


def _combine_kernel(off_ref, cnt_ref, gc_ref, x3_ref, mod_ref, gpost_ref, ys_ref,
                    out_ref, ybuf_ref, acc_ref, sem):
    b = pl.program_id(0)

    def copy(e, s):
        src = pl.multiple_of(off_ref[b * N_EXPERTS + e] + s * SUB, ROW_ALIGN)
        return pltpu.make_async_copy(ys_ref.at[pl.ds(src, SUB)], ybuf_ref.at[e * NS + s], sem.at[e * NS + s])

    for e in range(N_EXPERTS):
        c = cnt_ref[b * N_EXPERTS + e]
        for s in range(NS):
            @pl.when(s * SUB < c)
            def _(e=e, s=s):
                copy(e, s).start()

    gate = gc_ref[...]
    sel = gate > 0.0
    before = lax.broadcasted_iota(I32, (TB, TB), 1) < lax.broadcasted_iota(I32, (TB, TB), 0)
    rank = _dot(before.astype(F32), sel.astype(F32)).astype(I32)
    lane = lax.broadcasted_iota(I32, (TB, SUB), 1)
    acc_ref[...] = jnp.zeros((TB, D), F32)
    for e in range(N_EXPERTS):
        c = cnt_ref[b * N_EXPERTS + e]
        for s in range(NS):
            @pl.when(s * SUB < c)
            def _(e=e, s=s):
                copy(e, s).wait()
                place = ((rank[:, e:e + 1] == lane + s * SUB) & sel[:, e:e + 1]).astype(BF16)
                acc_ref[...] += gate[:, e:e + 1] * _dot(place, ybuf_ref[e * NS + s])
    _, _, gt = _split3(mod_ref[0, 0])
    out_ref[...] = x3_ref[...] + gt * (_rms_scale(acc_ref[...]) * gpost_ref[...])


def _combine(off, cnt, gates_c, x3, mods_ffn, gpost, ys, t):
    n = x3.shape[0]
    per_b = t // TB
    return pl.pallas_call(
        _combine_kernel,
        grid_spec=pltpu.PrefetchScalarGridSpec(
            num_scalar_prefetch=2,
            grid=(n // TB,),
            in_specs=[pl.BlockSpec((TB, LANES), lambda b, o, c: (b, 0)),
                      pl.BlockSpec((TB, D), lambda b, o, c: (b, 0)),
                      pl.BlockSpec((1, 1, 1, 3 * D), lambda b, o, c: (1, b // per_b, 0, 0)),
                      pl.BlockSpec((1, D), lambda b, o, c: (0, 0)),
                      pl.BlockSpec(memory_space=pl.ANY)],
            out_specs=pl.BlockSpec((TB, D), lambda b, o, c: (b, 0)),
            scratch_shapes=[pltpu.VMEM((N_EXPERTS * NS, SUB, D), BF16),
                            pltpu.VMEM((TB, D), F32),
                            pltpu.SemaphoreType.DMA((N_EXPERTS * NS,))]),
        out_shape=jax.ShapeDtypeStruct((n, D), F32),
        compiler_params=_cparams(("arbitrary",), 40),
        name="moe_combine",
    )(off, cnt, gates_c, x3, mods_ffn, gpost, ys)


def _s_mix0_kernel(x_ref, st_ref, modm_ref, modf_ref, gpre_ref, gpost_ref, gpref_ref, pw_ref, ps_ref,
                   x1_ref, pool_ref, hf_ref, *, pos):
    sh, sc, gt = _split3(modm_ref[...])
    x = x_ref[...]
    hm = _rms_scale(x) * gpre_ref[...] * (1.0 + sc) + sh
    parts = []
    for g, w in enumerate(POOL_WINDOWS):
        lo = g * PG
        tok = hm[:, lo:lo + PG]
        s = tok
        for j in range(1, w):
            s = s + st_ref[POOL_STATE - j, :, lo:lo + PG]
        cnt = float(min(w, pos + 1))
        parts.append(_dot_hi(s / cnt - tok, pw_ref[g]))
    mix = jnp.concatenate(parts, axis=1) * ps_ref[...]
    x1 = x + gt * (_rms_scale(mix) * gpost_ref[...])
    x1_ref[...] = x1
    for j in range(POOL_STATE - 1):
        pool_ref[j] = st_ref[j + 1]
    pool_ref[POOL_STATE - 1] = hm
    shf, scf, _ = _split3(modf_ref[...])
    hf_ref[...] = _rms_scale(x1) * gpref_ref[...] * (1.0 + scf) + shf


def _s_mix0(x_s, state_t, modm, modf, gpre, gpost, gpref, pw, ps, pos):
    rows = x_s.shape[0]
    return pl.pallas_call(
        functools.partial(_s_mix0_kernel, pos=pos),
        out_shape=[jax.ShapeDtypeStruct((rows, D), F32),
                   jax.ShapeDtypeStruct((POOL_STATE, rows, D), F32),
                   jax.ShapeDtypeStruct((rows, D), F32)],
        compiler_params=pltpu.CompilerParams(vmem_limit_bytes=40 * MIB),
        name="s_mix0",
    )(x_s, state_t, modm, modf, gpre, gpost, gpref, pw, ps)


def _s_ffn_kernel(h_ref, w1_ref, w3_ref, w2_ref, f_ref):
    j = pl.program_id(0)
    h = h_ref[...]
    contrib = _dot_hi(_silu(_dot_hi(h, w1_ref[...])) * _dot_hi(h, w3_ref[...]), w2_ref[...])

    @pl.when(j == 0)
    def _():
        f_ref[...] = contrib

    @pl.when(j > 0)
    def _():
        f_ref[...] += contrib


def _s_ffn(h, w1, w3, w2):
    rows = h.shape[0]
    dff = w1.shape[1]
    fc = 256
    return pl.pallas_call(
        _s_ffn_kernel,
        grid=(dff // fc,),
        in_specs=[pl.BlockSpec((rows, D), lambda j: (0, 0)),
                  pl.BlockSpec((D, fc), lambda j: (0, j)),
                  pl.BlockSpec((D, fc), lambda j: (0, j)),
                  pl.BlockSpec((fc, D), lambda j: (j, 0))],
        out_specs=pl.BlockSpec((rows, D), lambda j: (0, 0)),
        out_shape=jax.ShapeDtypeStruct((rows, D), F32),
        compiler_params=_cparams(("arbitrary",), 32),
        name="s_ffn0",
    )(h, w1, w3, w2)


def _s_post0_kernel(x1_ref, f_ref, modf_ref, modm_ref, gpost_ref, gkv_ref, gpre1_ref,
                    x2_ref, kvin_ref, hm1_ref):
    _, _, gt = _split3(modf_ref[...])
    x2 = x1_ref[...] + gt * (_rms_scale(f_ref[...]) * gpost_ref[...])
    x2_ref[...] = x2
    xn = _rms_scale(x2)
    kvin_ref[...] = xn * gkv_ref[...]
    sh1, sc1, _ = _split3(modm_ref[...])
    hm1_ref[...] = xn * gpre1_ref[...] * (1.0 + sc1) + sh1


def _s_post0(x1, f, modf, modm, gpost, gkv, gpre1):
    rows = x1.shape[0]
    return pl.pallas_call(
        _s_post0_kernel,
        out_shape=[jax.ShapeDtypeStruct((rows, D), F32)] * 3,
        name="s_post0",
    )(x1, f, modf, modm, gpost, gkv, gpre1)


def _s_linear_kernel(h_ref, w_ref, cos_ref, sin_ref, o_ref):
    o_ref[...] = _rope_cols(_dot_hi(h_ref[...], w_ref[...]), cos_ref[...], sin_ref[...])


def _s_linear_plain_kernel(h_ref, w_ref, o_ref):
    o_ref[...] = _dot_hi(h_ref[...], w_ref[...])


def _s_linear(h, w, rope=None):
    rows, k = h.shape
    nw = w.shape[1]
    cw = 512
    in_specs = [pl.BlockSpec((rows, k), lambda j: (0, 0)), pl.BlockSpec((k, cw), lambda j: (0, j))]
    args = [h, w]
    body = _s_linear_plain_kernel
    if rope is not None:
        body = _s_linear_kernel
        in_specs += [pl.BlockSpec((1, cw), lambda j: (0, j)), pl.BlockSpec((1, cw), lambda j: (0, j))]
        args += [rope[0], rope[1]]
    return pl.pallas_call(
        body,
        grid=(nw // cw,),
        in_specs=in_specs,
        out_specs=pl.BlockSpec((rows, cw), lambda j: (0, j)),
        out_shape=jax.ShapeDtypeStruct((rows, nw), F32),
        compiler_params=_cparams(("arbitrary",), 32),
        name="s_linear",
    )(*args)


def _s_attn_kernel(q_ref, kn_ref, vn_ref, k0_ref, k1_ref, k2_ref, v0_ref, v1_ref, v2_ref, o_ref):
    k_new = kn_ref[0]
    v_new = vn_ref[0]
    outs, lses = [], []
    for g, (k_ref, v_ref) in enumerate(((k0_ref, v0_ref), (k1_ref, v1_ref), (k2_ref, v2_ref))):
        qg = q_ref[0, g]
        kc = k_ref[0]
        s_c = jnp.sum(kc * qg[None], axis=-1, keepdims=True) * SCALE
        s_n = jnp.sum(k_new * qg, axis=-1, keepdims=True) * SCALE
        m = jnp.maximum(jnp.max(s_c, axis=0), s_n)
        p_c = jnp.exp(s_c - m[None])
        p_n = jnp.exp(s_n - m)
        den = jnp.sum(p_c, axis=0) + p_n
        outs.append((jnp.sum(p_c * v_ref[0], axis=0) + p_n * v_new) / den)
        lses.append(m + jnp.log(den))
    mx = jnp.maximum(jnp.maximum(lses[0], lses[1]), lses[2])
    es = [jnp.exp(l - mx) for l in lses]
    tot = es[0] + es[1] + es[2]
    out = (es[0] / tot) * outs[0]
    for g in range(1, len(ATTN_GROUPS)):
        out = out + (es[g] / tot) * outs[g]
    o_ref[0] = out


def _s_attn(q_s, k_new, v_new, cache_k, cache_v):
    rows, n_past = cache_k.shape[:2]
    specs, args = [], []
    for cache in (cache_k, cache_v):
        for win, dil in ATTN_GROUPS:
            span = win // dil
            assert span == BLK and n_past % (dil * BLK) == 0 and n_past >= win
            last = n_past // dil // BLK - 1
            specs.append(pl.BlockSpec((1, BLK, None, NH, HD), lambda b, last=last: (b, last, 0, 0, 0)))
            args.append(cache.reshape(rows, n_past // dil, dil, NH, HD))
    head = pl.BlockSpec((1, NH, HD), lambda b: (b, 0, 0))
    return pl.pallas_call(
        _s_attn_kernel,
        grid=(rows,),
        in_specs=[pl.BlockSpec((1, len(ATTN_GROUPS), NH, HD), lambda b: (b, 0, 0, 0)), head, head] + specs,
        out_specs=head,
        out_shape=jax.ShapeDtypeStruct((rows, NH, HD), F32),
        compiler_params=_cparams(("arbitrary",), 32),
        name="s_attn",
    )(q_s, k_new, v_new, *args)


def _s_post1_kernel(x2_ref, mix_ref, modm_ref, modf_ref, gpost_ref, gpre_ref, rwt_ref, rb_ref,
                    x3_ref, hf_ref, gc_ref):
    _, _, gt = _split3(modm_ref[...])
    x3 = x2_ref[...] + gt * (_rms_scale(mix_ref[...]) * gpost_ref[...])
    x3_ref[...] = x3
    sh, sc, _ = _split3(modf_ref[...])
    hf = _rms_scale(x3) * gpre_ref[...] * (1.0 + sc) + sh
    hf_ref[...] = hf
    logits_t = lax.dot_general(rwt_ref[...], hf, (((1,), (1,)), ((), ())),
                               precision=HIGHEST, preferred_element_type=F32) + rb_ref[...]
    gc_ref[...] = _router_top2(logits_t)


def _s_post1(x2, mix, modm, modf, gpost, gpre, rwt, rb):
    rows = x2.shape[0]
    return pl.pallas_call(
        _s_post1_kernel,
        out_shape=[jax.ShapeDtypeStruct((rows, D), F32), jax.ShapeDtypeStruct((rows, D), F32),
                   jax.ShapeDtypeStruct((N_EXPERTS, rows), F32)],
        name="s_post1",
    )(x2, mix, modm, modf, gpost, gpre, rwt, rb)


def _s_moe_kernel(h_ref, g_ref, w1_ref, w3_ref, w2_ref, o_ref):
    e, j = pl.program_id(0), pl.program_id(1)
    h = h_ref[...]
    y = _dot_hi(_silu(_dot_hi(h, w1_ref[0])) * _dot_hi(h, w3_ref[0]), w2_ref[0])
    contrib = g_ref[0] * y

    @pl.when((e == 0) & (j == 0))
    def _():
        o_ref[...] = contrib

    @pl.when((e > 0) | (j > 0))
    def _():
        o_ref[...] += contrib


def _s_moe(h, gates_e, w1, w3, w2):
    rows = h.shape[0]
    dff = w1.shape[2]
    return pl.pallas_call(
        _s_moe_kernel,
        grid=(N_EXPERTS, dff // FCX),
        in_specs=[pl.BlockSpec((rows, D), lambda e, j: (0, 0)),
                  pl.BlockSpec((1, rows, 1), lambda e, j: (e, 0, 0)),
                  pl.BlockSpec((1, D, FCX), lambda e, j: (e, 0, j)),
                  pl.BlockSpec((1, D, FCX), lambda e, j: (e, 0, j)),
                  pl.BlockSpec((1, FCX, D), lambda e, j: (e, j, 0))],
        out_specs=pl.BlockSpec((rows, D), lambda e, j: (0, 0)),
        out_shape=jax.ShapeDtypeStruct((rows, D), F32),
        compiler_params=_cparams(("arbitrary", "arbitrary"), 40),
        name="s_moe",
    )(h, gates_e, w1, w3, w2)


def _s_post2_kernel(x3_ref, f_ref, modf_ref, gpost_ref, y_ref):
    _, _, gt = _split3(modf_ref[...])
    y_ref[...] = x3_ref[...] + gt * (_rms_scale(f_ref[...]) * gpost_ref[...])


def _s_post2(x3, f, modf, gpost):
    return pl.pallas_call(
        _s_post2_kernel,
        out_shape=jax.ShapeDtypeStruct(x3.shape, F32),
        name="s_post2",
    )(x3, f, modf, gpost)


def _rope_tables(pos):
    half = HD // 2
    inv = ROPE_THETA ** (-jnp.arange(half, dtype=F32) / half)
    ang = pos.astype(F32)[:, None] * inv[None, :]
    cos, sin = jnp.cos(ang), jnp.sin(ang)
    cos = jnp.tile(cos, (1, LANES // half))
    sin = jnp.tile(jnp.concatenate([-sin, sin], axis=1), (1, LANES // HD))
    return cos, sin


def kernel(x_prompt, x_sample, state_pool, cache_k, cache_v, c_prompt, c_sample, ada_mix_w, ada_mix_b, g_pre_mix, g_post_mix, pool_w, pool_scale, g_kv, w_kv, w_q, w_o, ada_ffn_w, ada_ffn_b, g_pre_ffn, g_post_ffn, ffn_w1, ffn_w3, ffn_w2, router_w, router_b, moe_w1, moe_w3, moe_w2):
    b, t, _ = x_prompt.shape
    nsmp = x_sample.shape[0]
    n = b * t
    past_len = 16384
    assert x_sample.shape[1] == 1 and g_pre_mix.shape[0] == 2 and t % (ATTN_GROUPS[-1][1] * BLK) == 0
    vec = lambda a: a.reshape(1, D)

    c_all = jnp.concatenate([c_prompt, c_sample], axis=0)
    mods_mix = _ada(c_all, ada_mix_w, ada_mix_b)
    mods_ffn = _ada(c_all, ada_ffn_w, ada_ffn_b)
    pm_mix = mods_mix[:, :b].reshape(2, b, 1, 3 * D)
    pm_ffn = mods_ffn[:, :b].reshape(2, b, 1, 3 * D)
    sm_mix = mods_mix[:, b:]
    sm_ffn = mods_ffn[:, b:]

    rwt = router_w[0].T
    rb = router_b[0].reshape(N_EXPERTS, 1)

    x1, pool16 = _mix0(x_prompt, pm_mix, vec(g_pre_mix[0]), vec(g_post_mix[0]),
                       pool_w[0].astype(BF16), vec(pool_scale[0]))
    cos_p, sin_p = _rope_tables(jnp.arange(t, dtype=jnp.int32))
    x2, k_p, v_p, *kv_l = _ffn0(
        x1.reshape(n, D), pm_ffn, vec(g_pre_ffn[0]), vec(g_post_ffn[0]), vec(g_kv), cos_p, sin_p,
        ffn_w1[0].astype(BF16), ffn_w3[0].astype(BF16), ffn_w2[0].astype(BF16), w_kv.astype(BF16), b, t)
    q_l = _qproj(x2, pm_mix, vec(g_pre_mix[1]), cos_p, sin_p, w_q[0].astype(BF16), b, t)
    attn = _attention(q_l, kv_l[:3], kv_l[3:], b, t)
    x3, hf1, gates_t, gates_c = _post1(attn, x2, pm_mix, pm_ffn, vec(g_post_mix[1]),
                                       vec(g_pre_ffn[1]), w_o[0].astype(BF16), rwt, rb, t)
    off, cnt, tile_e, n_used = _route_plan(gates_t, n)
    xs = _dispatch(off, cnt, hf1, gates_t)
    ys = _experts(tile_e, n_used, xs, moe_w1[0], moe_w3[0], moe_w2[0])
    y_prompt = _combine(off, cnt, gates_c, x3, pm_ffn, vec(g_post_ffn[1]), ys, t).reshape(b, t, D)

    state_t = jnp.transpose(state_pool[:, 0], (1, 0, 2))
    x1s, pool_t, hf0s = _s_mix0(x_sample.reshape(nsmp, D), state_t, sm_mix[0], sm_ffn[0],
                                vec(g_pre_mix[0]), vec(g_post_mix[0]), vec(g_pre_ffn[0]),
                                pool_w[0], vec(pool_scale[0]), past_len)
    f0s = _s_ffn(hf0s, ffn_w1[0], ffn_w3[0], ffn_w2[0])
    x2s, kvin_s, hm1s = _s_post0(x1s, f0s, sm_ffn[0], sm_mix[1], vec(g_post_ffn[0]), vec(g_kv), vec(g_pre_mix[1]))
    cos_s, sin_s = _rope_tables(jnp.full((1,), past_len, jnp.int32))
    ones, zeros = jnp.ones((1, D), F32), jnp.zeros((1, D), F32)
    kv_s = _s_linear(kvin_s, w_kv, (jnp.concatenate([jnp.tile(cos_s, (1, D // LANES)), ones], axis=1),
                                    jnp.concatenate([jnp.tile(sin_s, (1, D // LANES)), zeros], axis=1)))
    q_s = _s_linear(hm1s, w_q[0], (jnp.tile(cos_s, (1, 3 * D // LANES)), jnp.tile(sin_s, (1, 3 * D // LANES))))
    k_s = kv_s[:, :D].reshape(nsmp, NH, HD)
    v_s = kv_s[:, D:].reshape(nsmp, NH, HD)
    attn_s = _s_attn(q_s.reshape(nsmp, len(ATTN_GROUPS), NH, HD), k_s, v_s, cache_k, cache_v).reshape(nsmp, D)
    mix1s = _s_linear(attn_s, w_o[0])
    x3s, hf1s, gts = _s_post1(x2s, mix1s, sm_mix[1], sm_ffn[1], vec(g_post_mix[1]), vec(g_pre_ffn[1]), rwt, rb)
    moe_s = _s_moe(hf1s, gts.reshape(N_EXPERTS, nsmp, 1), moe_w1[0], moe_w3[0], moe_w2[0])
    y_sample = _s_post2(x3s, moe_s, sm_ffn[1], vec(g_post_ffn[1])).reshape(nsmp, 1, D)

    pool_prompt = pool16[:, None, 1:, :]
    pool_sample = jnp.transpose(pool_t, (1, 0, 2))[:, None]
    return (y_prompt, y_sample, pool_prompt, pool_sample,
            k_p.reshape(b, t, NH, HD), v_p.reshape(b, t, NH, HD),
            k_s[:, None], v_s[:, None])
```

```python
import functools

import jax
import jax.numpy as jnp
import numpy as np
from jax import lax
from jax.experimental import pallas as pl
from jax.experimental.pallas import tpu as pltpu

F32, BF16, I32 = jnp.float32, jnp.bfloat16, jnp.int32
HIGHEST = lax.Precision.HIGHEST

D = 1024
POOL_WINDOWS = (2, 4, 8, 16)
PG = D // len(POOL_WINDOWS)
POOL_STATE = max(POOL_WINDOWS) - 1
HALO = POOL_STATE + 1
ATTN_GROUPS = ((128, 1), (512, 4), (2048, 16))
NH = 16
HD = 64
BLK = 128
ROPE_THETA = 10000.0
N_EXPERTS = 8
EPS = 1e-6
SCALE = HD ** -0.5
LANES = 128
MIB = 1024 * 1024

TT_MIX = 512
TM_FFN = 1024
HB_S = 4
ATTN_UNITS = 4
FC_FFN = 256
TM_POST = 512
TB = 512
SUB = 128
NS = TB // SUB
TMX = 1024
FCX = 512
ROW_ALIGN = 16


def _cparams(sem, vmem_mib):
    return pltpu.CompilerParams(dimension_semantics=sem, vmem_limit_bytes=vmem_mib * MIB)


def _rms_scale(x):
    return x * lax.rsqrt(jnp.mean(x * x, axis=-1, keepdims=True) + EPS)


def _split3(m):
    return m[:, :D], m[:, D:2 * D], m[:, 2 * D:]


def _silu(x):
    return x * jax.nn.sigmoid(x)


def _rope_cols(x, cos, sin):
    lane = lax.broadcasted_iota(I32, (x.shape[0], LANES), 1)
    first = (lane % HD) < (HD // 2)
    shared = cos.shape[1] == LANES
    outs = []
    for c in range(x.shape[1] // LANES):
        cols = slice(c * LANES, (c + 1) * LANES)
        xc = x[:, cols]
        swapped = jnp.where(first, pltpu.roll(xc, LANES - HD // 2, 1), pltpu.roll(xc, HD // 2, 1))
        outs.append(xc * (cos if shared else cos[:, cols]) + swapped * (sin if shared else sin[:, cols]))
    return jnp.concatenate(outs, axis=1)


def _dot_hi(a, b):
    return jnp.dot(a, b, precision=HIGHEST, preferred_element_type=F32)


def _dot(a, b):
    return jnp.dot(a, b, preferred_element_type=F32)


def _ada_kernel(c_ref, w_ref, b_ref, o_ref):
    o_ref[0] = _dot_hi(_silu(c_ref[...]), w_ref[0]) + b_ref[0]


def _ada(c_all, w, b):
    nl, _, n3 = w.shape
    rows = c_all.shape[0]
    cw = 768
    return pl.pallas_call(
        _ada_kernel,
        grid=(nl, n3 // cw),
        in_specs=[pl.BlockSpec((rows, D), lambda l, j: (0, 0)),
                  pl.BlockSpec((1, D, cw), lambda l, j: (l, 0, j)),
                  pl.BlockSpec((1, 1, cw), lambda l, j: (l, 0, j))],
        out_specs=pl.BlockSpec((1, rows, cw), lambda l, j: (l, 0, j)),
        out_shape=jax.ShapeDtypeStruct((nl, rows, n3), F32),
        compiler_params=_cparams(("arbitrary", "arbitrary"), 32),
        name="ada_mod",
    )(c_all, w, b.reshape(nl, 1, n3))


def _mix0_kernel(x_ref, xh_ref, mod_ref, gpre_ref, gpost_ref, pw_ref, ps_ref,
                 x1_ref, pool_ref, ext_ref):
    i = pl.program_id(1)
    tt = x_ref.shape[1]
    sh, sc, gt = _split3(mod_ref[0, 0])
    gpre = gpre_ref[...]
    x = x_ref[0]
    hm = _rms_scale(x) * gpre * (1.0 + sc) + sh
    hh = _rms_scale(xh_ref[0]) * gpre * (1.0 + sc) + sh
    ext_ref[0:HALO, :] = jnp.where(i > 0, hh, 0.0)
    ext_ref[HALO:, :] = hm
    t = i * tt + lax.broadcasted_iota(I32, (tt, 1), 0)
    parts = []
    for g, w in enumerate(POOL_WINDOWS):
        lo = g * PG
        tok = hm[:, lo:lo + PG]
        s = tok
        for j in range(1, w):
            s = s + ext_ref[HALO - j:HALO - j + tt, lo:lo + PG]
        cnt = jnp.minimum(w, t + 1).astype(F32)
        parts.append(_dot((s / cnt - tok).astype(BF16), pw_ref[g]))
    mix = jnp.concatenate(parts, axis=1) * ps_ref[...]
    x1_ref[0] = x + gt * (_rms_scale(mix) * gpost_ref[...])

    @pl.when(i == pl.num_programs(1) - 1)
    def _():
        pool_ref[0] = ext_ref[tt:tt + HALO, :]


def _mix0(x, mods, gpre, gpost, pw_bf, ps):
    b, t, _ = x.shape
    tt = TT_MIX
    hb = tt // HALO
    return pl.pallas_call(
        _mix0_kernel,
        grid=(b, t // tt),
        in_specs=[pl.BlockSpec((1, tt, D), lambda bb, i: (bb, i, 0)),
                  pl.BlockSpec((1, HALO, D), lambda bb, i: (bb, jnp.maximum(i * hb - 1, 0), 0)),
                  pl.BlockSpec((1, 1, 1, 3 * D), lambda bb, i: (0, bb, 0, 0)),
                  pl.BlockSpec((1, D), lambda bb, i: (0, 0)),
                  pl.BlockSpec((1, D), lambda bb, i: (0, 0)),
                  pl.BlockSpec((len(POOL_WINDOWS), PG, PG), lambda bb, i: (0, 0, 0)),
                  pl.BlockSpec((1, D), lambda bb, i: (0, 0))],
        out_specs=[pl.BlockSpec((1, tt, D), lambda bb, i: (bb, i, 0)),
                   pl.BlockSpec((1, HALO, D), lambda bb, i: (bb, 0, 0))],
        out_shape=[jax.ShapeDtypeStruct((b, t, D), F32),
                   jax.ShapeDtypeStruct((b, HALO, D), F32)],
        scratch_shapes=[pltpu.VMEM((tt + HALO, D), F32)],
        compiler_params=_cparams(("arbitrary", "arbitrary"), 40),
        name="mix0",
    )(x, x, mods, gpre, gpost, pw_bf, ps)


def _store_dilated(val, plane_ref, dils, out_refs):
    tm = val.shape[0]
    for hp in range(NH // 2):
        plane_ref[hp] = val[:, hp * LANES:(hp + 1) * LANES]
    for dil, out_ref in zip(dils, out_refs):
        for hp in range(NH // 2):
            for r in range(dil):
                out_ref[0, hp, r] = plane_ref[hp, pl.ds(r, tm // dil, stride=dil), :].astype(BF16)


def _dilated_specs(b, t, tm):
    per_b = t // tm
    specs = [pl.BlockSpec((1, NH // 2, dil, tm // dil, LANES), lambda i: (i // per_b, 0, 0, i % per_b, 0))
             for _, dil in ATTN_GROUPS]
    shapes = [jax.ShapeDtypeStruct((b, NH // 2, dil, t // dil, LANES), BF16) for _, dil in ATTN_GROUPS]
    return specs, shapes


def _ffn0_kernel(x1_ref, modf_ref, gpre_ref, gpost_ref, w1_ref, w3_ref, w2_ref, x2_ref, acc_ref):
    sh, sc, gt = _split3(modf_ref[0, 0])
    x1 = x1_ref[...]
    hf = (_rms_scale(x1) * gpre_ref[...] * (1.0 + sc) + sh).astype(BF16)
    for c in range(w1_ref.shape[1] // FC_FFN):
        cs = slice(c * FC_FFN, (c + 1) * FC_FFN)
        a = (_silu(_dot(hf, w1_ref[:, cs])) * _dot(hf, w3_ref[:, cs])).astype(BF16)
        contrib = _dot(a, w2_ref[cs, :])
        if c == 0:
            acc_ref[...] = contrib
        else:
            acc_ref[...] += contrib
    x2 = x1 + gt * (_rms_scale(acc_ref[...]) * gpost_ref[...])
    x2_ref[...] = x2


def _ffn0(x1, mods_ffn, gpre, gpost, w1, w3, w2, t):
    n = x1.shape[0]
    tm = TM_FFN
    per_b = t // tm
    dff = w1.shape[1]
    row = lambda i: (i, 0)
    fixed = lambda i: (0, 0)
    resident = lambda shape: pl.BlockSpec(shape, fixed, pipeline_mode=pl.Buffered(1))
    vec = pl.BlockSpec((1, D), fixed)
    return pl.pallas_call(
        _ffn0_kernel,
        grid=(n // tm,),
        in_specs=[pl.BlockSpec((tm, D), row),
                  pl.BlockSpec((1, 1, 1, 3 * D), lambda i: (0, i // per_b, 0, 0)),
                  vec, vec,
                  resident((D, dff)), resident((D, dff)), resident((dff, D))],
        out_specs=pl.BlockSpec((tm, D), row),
        out_shape=jax.ShapeDtypeStruct((n, D), F32),
        scratch_shapes=[pltpu.VMEM((tm, D), F32)],
        compiler_params=_cparams(("arbitrary",), 56),
        name="ffn0",
    )(x1, mods_ffn, gpre, gpost, w1, w3, w2)


def _kvproj_kernel(x2_ref, gkv_ref, cos_ref, sin_ref, wkv_ref,
                   kt_ref, vt_ref, k0_ref, k1_ref, k2_ref, v0_ref, v1_ref, v2_ref, plane_ref):
    tm = x2_ref.shape[0]
    kv = _dot((_rms_scale(x2_ref[...]) * gkv_ref[...]).astype(BF16), wkv_ref[...])
    k = _rope_cols(kv[:, :D], cos_ref[...], sin_ref[...])
    v = kv[:, D:]
    kt_ref[0] = k.T.reshape(NH, HD, tm)
    vt_ref[0] = v.T.reshape(NH, HD, tm)
    dils = [dil for _, dil in ATTN_GROUPS]
    _store_dilated(k, plane_ref, dils, (k0_ref, k1_ref, k2_ref))
    _store_dilated(v, plane_ref, dils, (v0_ref, v1_ref, v2_ref))


def _kvproj(x2, gkv, cos, sin, wkv, b, t):
    n = x2.shape[0]
    tm = TM_POST
    per_b = t // tm
    fixed = lambda i: (0, 0)
    dspecs, dshapes = _dilated_specs(b, t, tm)
    tspec = pl.BlockSpec((1, NH, HD, tm), lambda i: (i // per_b, 0, 0, i % per_b))
    tshape = jax.ShapeDtypeStruct((b, NH, HD, t), F32)
    return pl.pallas_call(
        _kvproj_kernel,
        grid=(n // tm,),
        in_specs=[pl.BlockSpec((tm, D), lambda i: (i, 0)),
                  pl.BlockSpec((1, D), fixed),
                  pl.BlockSpec((tm, LANES), lambda i: (i % per_b, 0)),
                  pl.BlockSpec((tm, LANES), lambda i: (i % per_b, 0)),
                  pl.BlockSpec((D, 2 * D), fixed, pipeline_mode=pl.Buffered(1))],
        out_specs=[tspec, tspec] + dspecs + dspecs,
        out_shape=[tshape, tshape] + dshapes + dshapes,
        scratch_shapes=[pltpu.VMEM((NH // 2, tm, LANES), F32)],
        compiler_params=_cparams(("arbitrary",), 48),
        name="kv_proj",
    )(x2, gkv, cos, sin, wkv)


def _qproj_kernel(x2_ref, modm_ref, gpre1_ref, cos_ref, sin_ref, wq_ref, q0_ref, q1_ref, q2_ref, scr_ref):
    sh1, sc1, _ = _split3(modm_ref[0, 0])
    hm1 = (_rms_scale(x2_ref[...]) * gpre1_ref[...] * (1.0 + sc1) + sh1).astype(BF16)
    q = _rope_cols(_dot(hm1, wq_ref[...]), cos_ref[...], sin_ref[...]) * SCALE
    for g, ((_, dil), q_ref) in enumerate(zip(ATTN_GROUPS, (q0_ref, q1_ref, q2_ref))):
        _store_dilated(q[:, g * D:(g + 1) * D], scr_ref, (dil,), (q_ref,))


def _qproj(x2, mods_mix, gpre1, cos, sin, wq, b, t):
    n = x2.shape[0]
    tm = TM_POST
    per_b = t // tm
    fixed = lambda i: (0, 0)
    dspecs, dshapes = _dilated_specs(b, t, tm)
    return pl.pallas_call(
        _qproj_kernel,
        grid=(n // tm,),
        in_specs=[pl.BlockSpec((tm, D), lambda i: (i, 0)),
                  pl.BlockSpec((1, 1, 1, 3 * D), lambda i: (1, i // per_b, 0, 0)),
                  pl.BlockSpec((1, D), fixed),
                  pl.BlockSpec((tm, LANES), lambda i: (i % per_b, 0)),
                  pl.BlockSpec((tm, LANES), lambda i: (i % per_b, 0)),
                  pl.BlockSpec((D, 3 * D), fixed, pipeline_mode=pl.Buffered(1))],
        out_specs=dspecs,
        out_shape=dshapes,
        scratch_shapes=[pltpu.VMEM((NH // 2, tm, LANES), F32)],
        compiler_params=_cparams(("arbitrary",), 48),
        name="q_proj",
    )(x2, mods_mix, gpre1, cos, sin, wq)


def _attn_kernel(q0_ref, q1_ref, q2_ref, k0_ref, k1_ref, k2_ref, v0_ref, v1_ref, v2_ref,
                 o_ref, og_ref, lg_ref, bias_ref):
    t = o_ref.shape[2]
    first_head = lax.broadcasted_iota(I32, (BLK, LANES), 1) < HD
    nt = (((1,), (1,)), ((), ()))

    @pl.when((pl.program_id(0) == 0) & (pl.program_id(1) == 0))
    def _():
        qi = lax.broadcasted_iota(I32, (2 * BLK, 2 * BLK), 0) % BLK
        kj = lax.broadcasted_iota(I32, (2 * BLK, 2 * BLK), 1)
        for v in range(2):
            dist = v * BLK + qi - kj
            bias_ref[v] = jnp.where((dist >= 0) & (dist <= BLK), 0.0, -jnp.inf)

    groups = zip(ATTN_GROUPS, (q0_ref, q1_ref, q2_ref), (k0_ref, k1_ref, k2_ref), (v0_ref, v1_ref, v2_ref))
    for g, ((win, dil), q_ref, k_ref, v_ref) in enumerate(groups):
        assert win // dil == BLK
        sub = t // dil
        nb = sub // BLK
        kw = min(2 * BLK, sub)

        def scores(u, nb=nb, kw=kw, q_ref=q_ref, k_ref=k_ref):
            r = u // nb
            n = u % nb
            q_lo = pl.multiple_of(n * BLK, BLK)
            k_lo = pl.multiple_of(jnp.maximum(n - 1, 0) * BLK, BLK)
            qp = q_ref[0, 0, r, pl.ds(q_lo, BLK), :]
            zero = jnp.zeros_like(qp)
            qs = jnp.concatenate([jnp.where(first_head, qp, zero), jnp.where(first_head, zero, qp)], axis=0)
            s = lax.dot_general(qs, k_ref[0, 0, r, pl.ds(k_lo, kw), :], nt, preferred_element_type=F32)
            return s + bias_ref[jnp.minimum(n, 1), :, 0:kw], r, q_lo, k_lo

        def finish(s, r, q_lo, k_lo, dil=dil, kw=kw, v_ref=v_ref, g=g):
            m = jnp.max(s, axis=-1, keepdims=True)
            p = jnp.exp(s - m)
            den = jnp.sum(p, axis=-1, keepdims=True)
            o2 = _dot(p.astype(BF16), v_ref[0, 0, r, pl.ds(k_lo, kw), :]) * (1.0 / den)
            l2 = jnp.broadcast_to(m + jnp.log(den), (2 * BLK, LANES))
            rows = pl.ds(q_lo * dil + r, BLK, stride=dil) if dil > 1 else pl.ds(q_lo, BLK)
            og_ref[g, rows, :] = jnp.where(first_head, o2[:BLK], o2[BLK:])
            lg_ref[g, rows, :] = jnp.where(first_head, l2[:BLK], l2[BLK:])

        def trip(i, carry, scores=scores, finish=finish):
            started = [scores(i * ATTN_UNITS + j) for j in range(ATTN_UNITS)]
            for args in started:
                finish(*args)
            return carry

        lax.fori_loop(0, dil * nb // ATTN_UNITS, trip, 0)

    mt = 2 * BLK

    def merge(c, carry):
        rows = pl.ds(pl.multiple_of(c * mt, mt), mt)
        ls = [lg_ref[g, rows, :] for g in range(len(ATTN_GROUPS))]
        mx = jnp.maximum(jnp.maximum(ls[0], ls[1]), ls[2])
        es = [jnp.exp(l - mx) for l in ls]
        tot = es[0] + es[1] + es[2]
        out = (es[0] / tot) * og_ref[0, rows, :]
        for g in range(1, len(ATTN_GROUPS)):
            out = out + (es[g] / tot) * og_ref[g, rows, :]
        o_ref[0, 0, rows, :] = out.astype(BF16)
        return carry

    lax.fori_loop(0, t // mt, merge, 0)


def _attention(q_l, k_l, v_l, b, t):
    specs = [pl.BlockSpec((1, 1, dil, t // dil, LANES), lambda bb, hp: (bb, hp, 0, 0, 0))
             for _, dil in ATTN_GROUPS]
    return pl.pallas_call(
        _attn_kernel,
        grid=(b, NH // 2),
        in_specs=specs * 3,
        out_specs=pl.BlockSpec((1, 1, t, LANES), lambda bb, hp: (bb, hp, 0, 0)),
        out_shape=jax.ShapeDtypeStruct((b, NH // 2, t, LANES), BF16),
        scratch_shapes=[pltpu.VMEM((len(ATTN_GROUPS), t, LANES), F32),
                        pltpu.VMEM((len(ATTN_GROUPS), t, LANES), F32),
                        pltpu.VMEM((2, 2 * BLK, 2 * BLK), F32)],
        compiler_params=_cparams(("arbitrary", "arbitrary"), 40),
        name="dilated_attn",
    )(*q_l, *k_l, *v_l)


def _router_top2(logits_t):
    idx = lax.broadcasted_iota(I32, logits_t.shape, 0)
    v1 = jnp.max(logits_t, axis=0, keepdims=True)
    i1 = jnp.min(jnp.where(logits_t == v1, idx, N_EXPERTS), axis=0, keepdims=True)
    m1 = idx == i1
    rest = jnp.where(m1, -jnp.inf, logits_t)
    v2 = jnp.max(rest, axis=0, keepdims=True)
    i2 = jnp.min(jnp.where(rest == v2, idx, N_EXPERTS), axis=0, keepdims=True)
    m2 = idx == i2
    e2 = jnp.exp(v2 - v1)
    den = 1.0 + e2
    return jnp.where(m1, 1.0 / den, 0.0) + jnp.where(m2, e2 / den, 0.0)


def _post1_kernel(a_ref, x2_ref, modm_ref, modf_ref, gpost_ref, gpre_ref, wo_ref, rwt_ref, rb_ref,
                  x3_ref, hf_ref, gt_ref, gc_ref):
    o = jnp.concatenate([a_ref[0, hp] for hp in range(NH // 2)], axis=1)
    mix = _dot(o, wo_ref[...])
    _, _, gt = _split3(modm_ref[0, 0])
    x3 = x2_ref[...] + gt * (_rms_scale(mix) * gpost_ref[...])
    x3_ref[...] = x3
    sh, sc, _ = _split3(modf_ref[0, 0])
    hf = _rms_scale(x3) * gpre_ref[...] * (1.0 + sc) + sh
    hf_ref[...] = hf.astype(BF16)
    logits_t = lax.dot_general(rwt_ref[...], hf, (((1,), (1,)), ((), ())),
                               precision=HIGHEST, preferred_element_type=F32) + rb_ref[...]
    gates_t = _router_top2(logits_t)
    gt_ref[...] = gates_t
    padded = jnp.concatenate([gates_t, jnp.zeros((LANES - N_EXPERTS, gates_t.shape[1]), F32)], axis=0)
    gc_ref[...] = padded.T


def _post1(attn, x2, mods_mix, mods_ffn, gpost, gpre, wo_bf, rwt, rb, t):
    n = x2.shape[0]
    tm = TM_POST
    per_b = t // tm
    row = lambda i: (i, 0)
    fixed = lambda i: (0, 0)
    vec = pl.BlockSpec((1, D), fixed)
    tile = pl.BlockSpec((tm, D), row)
    return pl.pallas_call(
        _post1_kernel,
        grid=(n // tm,),
        in_specs=[pl.BlockSpec((1, NH // 2, tm, LANES), lambda i: (i // per_b, 0, i % per_b, 0)),
                  tile,
                  pl.BlockSpec((1, 1, 1, 3 * D), lambda i: (1, i // per_b, 0, 0)),
                  pl.BlockSpec((1, 1, 1, 3 * D), lambda i: (1, i // per_b, 0, 0)),
                  vec, vec,
                  pl.BlockSpec((D, D), fixed),
                  pl.BlockSpec((N_EXPERTS, D), fixed),
                  pl.BlockSpec((N_EXPERTS, 1), fixed)],
        out_specs=[tile, tile, pl.BlockSpec((N_EXPERTS, tm), lambda i: (0, i)),
                   pl.BlockSpec((tm, LANES), row)],
        out_shape=[jax.ShapeDtypeStruct((n, D), F32), jax.ShapeDtypeStruct((n, D), BF16),
                   jax.ShapeDtypeStruct((N_EXPERTS, n), F32), jax.ShapeDtypeStruct((n, LANES), F32)],
        compiler_params=_cparams(("arbitrary",), 40),
        name="wo_router",
    )(attn, x2, mods_mix, mods_ffn, gpost, gpre, wo_bf, rwt, rb)


def _route_plan(gates_t, n):
    nb = n // TB
    cnt = jnp.sum((gates_t > 0.0).reshape(N_EXPERTS, nb, TB), axis=-1).astype(I32)
    c_al = (cnt + ROW_ALIGN - 1) // ROW_ALIGN * ROW_ALIGN
    within = jnp.cumsum(c_al, axis=1) - c_al
    tot = jnp.sum(c_al, axis=1)
    region = (tot + SUB + TMX - 1) // TMX * TMX
    start = jnp.cumsum(region) - region
    off = (start[:, None] + within).T.reshape(-1)
    n_used = (jnp.sum(region) // TMX).astype(I32)
    tile_row = jnp.arange(_max_sorted_rows(n) // TMX, dtype=I32) * TMX
    tile_e = jnp.sum(tile_row[:, None] >= (start + region)[None, :], axis=1).astype(I32)
    tile_e = jnp.minimum(tile_e, N_EXPERTS - 1)
    last_e = tile_e[jnp.maximum(n_used - 1, 0)]
    tile_e = jnp.where(jnp.arange(tile_e.shape[0]) < n_used, tile_e, last_e)
    return off.astype(I32), cnt.T.reshape(-1), tile_e, n_used.reshape(1)


def _max_sorted_rows(n):
    nb = n // TB
    worst = 2 * n + nb * N_EXPERTS * (ROW_ALIGN - 1) + N_EXPERTS * (SUB + TMX - 1)
    return (worst + TMX - 1) // TMX * TMX


def _dispatch_kernel(off_ref, cnt_ref, hf_ref, gt_ref, xs_in_ref, xs_ref, stage_ref, sem):
    del xs_in_ref
    b = pl.program_id(0)
    hf = hf_ref[...]
    sel = gt_ref[...] > 0.0
    before = lax.broadcasted_iota(I32, (TB, TB), 0) < lax.broadcasted_iota(I32, (TB, TB), 1)
    rank = _dot(sel.astype(F32), before.astype(F32)).astype(I32)
    jj = lax.broadcasted_iota(I32, (SUB, TB), 0)

    def copy(e, s):
        dst = pl.multiple_of(off_ref[b * N_EXPERTS + e] + s * SUB, ROW_ALIGN)
        return pltpu.make_async_copy(stage_ref.at[e * NS + s], xs_ref.at[pl.ds(dst, SUB)], sem.at[e * NS + s])

    for e in range(N_EXPERTS):
        c = cnt_ref[b * N_EXPERTS + e]
        for s in range(NS):
            @pl.when(s * SUB < c)
            def _(e=e, s=s):
                pick = ((rank[e:e + 1, :] == jj + s * SUB) & sel[e:e + 1, :]).astype(BF16)
                stage_ref[e * NS + s] = _dot(pick, hf).astype(BF16)
                copy(e, s).start()
    for e in range(N_EXPERTS):
        c = cnt_ref[b * N_EXPERTS + e]
        for s in range(NS):
            @pl.when(s * SUB < c)
            def _(e=e, s=s):
                copy(e, s).wait()


def _dispatch(off, cnt, hf_bf, gates_t):
    n = hf_bf.shape[0]
    rows = _max_sorted_rows(n)
    zeros = jnp.zeros((rows, D), BF16)
    return pl.pallas_call(
        _dispatch_kernel,
        grid_spec=pltpu.PrefetchScalarGridSpec(
            num_scalar_prefetch=2,
            grid=(n // TB,),
            in_specs=[pl.BlockSpec((TB, D), lambda b, o, c: (b, 0)),
                      pl.BlockSpec((N_EXPERTS, TB), lambda b, o, c: (0, b)),
                      pl.BlockSpec(memory_space=pl.ANY)],
            out_specs=pl.BlockSpec(memory_space=pl.ANY),
            scratch_shapes=[pltpu.VMEM((N_EXPERTS * NS, SUB, D), BF16),
                            pltpu.SemaphoreType.DMA((N_EXPERTS * NS,))]),
        out_shape=jax.ShapeDtypeStruct((rows, D), BF16),
        input_output_aliases={4: 0},
        compiler_params=_cparams(("arbitrary",), 40),
        name="moe_dispatch",
    )(off, cnt, hf_bf, gates_t, zeros)


def _expert_kernel(te_ref, nu_ref, x_ref, w1_ref, w3_ref, w2_ref, y_ref, acc_ref):
    del te_ref
    i, j = pl.program_id(0), pl.program_id(1)

    @pl.when(i < nu_ref[0])
    def _():
        x = x_ref[...]
        a = (_silu(_dot(x, w1_ref[0].astype(BF16))) * _dot(x, w3_ref[0].astype(BF16))).astype(BF16)
        contrib = _dot(a, w2_ref[0].astype(BF16))

        @pl.when(j == 0)
        def _():
            acc_ref[...] = contrib

        @pl.when(j > 0)
        def _():
            acc_ref[...] += contrib

        @pl.when(j == pl.num_programs(1) - 1)
        def _():
            y_ref[...] = acc_ref[...].astype(BF16)

    @pl.when((i >= nu_ref[0]) & (j == 0))
    def _():
        y_ref[...] = jnp.zeros(y_ref.shape, BF16)


def _experts(tile_e, n_used, xs, w1, w3, w2):
    rows = xs.shape[0]
    dff = w1.shape[2]
    nj = dff // FCX

    def tile_idx(i, j, te, nu):
        return (jnp.minimum(i, nu[0] - 1), 0)

    def chunk(i, j, nu):
        return jnp.where(i < nu[0], j, nj - 1)

    return pl.pallas_call(
        _expert_kernel,
        grid_spec=pltpu.PrefetchScalarGridSpec(
            num_scalar_prefetch=2,
            grid=(rows // TMX, nj),
            in_specs=[pl.BlockSpec((TMX, D), tile_idx),
                      pl.BlockSpec((1, D, FCX), lambda i, j, te, nu: (te[i], 0, chunk(i, j, nu))),
                      pl.BlockSpec((1, D, FCX), lambda i, j, te, nu: (te[i], 0, chunk(i, j, nu))),
                      pl.BlockSpec((1, FCX, D), lambda i, j, te, nu: (te[i], chunk(i, j, nu), 0))],
            out_specs=pl.BlockSpec((TMX, D), lambda i, j, te, nu: (i, 0)),
            scratch_shapes=[pltpu.VMEM((TMX, D), F32)]),
        out_shape=jax.ShapeDtypeStruct((rows, D), BF16),
        compiler_params=_cparams(("arbitrary", "arbitrary"), 56),
        name="moe_experts",
    )(tile_e, n_used, xs, w1, w3, w2)


def _combine_kernel(off_ref, cnt_ref, gc_ref, x3_ref, mod_ref, gpost_ref, ys_ref,
                    out_ref, ybuf_ref, acc_ref, sem):
    b = pl.program_id(0)

    def copy(e, s):
        src = pl.multiple_of(off_ref[b * N_EXPERTS + e] + s * SUB, ROW_ALIGN)
        return pltpu.make_async_copy(ys_ref.at[pl.ds(src, SUB)], ybuf_ref.at[e * NS + s], sem.at[e * NS + s])

    for e in range(N_EXPERTS):
        c = cnt_ref[b * N_EXPERTS + e]
        for s in range(NS):
            @pl.when(s * SUB < c)
            def _(e=e, s=s):
                copy(e, s).start()

    gate = gc_ref[...]
    sel = gate > 0.0
    before = lax.broadcasted_iota(I32, (TB, TB), 1) < lax.broadcasted_iota(I32, (TB, TB), 0)
    rank = _dot(before.astype(F32), sel.astype(F32)).astype(I32)
    lane = lax.broadcasted_iota(I32, (TB, SUB), 1)
    acc_ref[...] = jnp.zeros((TB, D), F32)
    for e in range(N_EXPERTS):
        c = cnt_ref[b * N_EXPERTS + e]
        for s in range(NS):
            @pl.when(s * SUB < c)
            def _(e=e, s=s):
                copy(e, s).wait()
                place = ((rank[:, e:e + 1] == lane + s * SUB) & sel[:, e:e + 1]).astype(BF16)
                acc_ref[...] += gate[:, e:e + 1] * _dot(place, ybuf_ref[e * NS + s])
    _, _, gt = _split3(mod_ref[0, 0])
    out_ref[...] = x3_ref[...] + gt * (_rms_scale(acc_ref[...]) * gpost_ref[...])


def _combine(off, cnt, gates_c, x3, mods_ffn, gpost, ys, t):
    n = x3.shape[0]
    per_b = t // TB
    return pl.pallas_call(
        _combine_kernel,
        grid_spec=pltpu.PrefetchScalarGridSpec(
            num_scalar_prefetch=2,
            grid=(n // TB,),
            in_specs=[pl.BlockSpec((TB, LANES), lambda b, o, c: (b, 0)),
                      pl.BlockSpec((TB, D), lambda b, o, c: (b, 0)),
                      pl.BlockSpec((1, 1, 1, 3 * D), lambda b, o, c: (1, b // per_b, 0, 0)),
                      pl.BlockSpec((1, D), lambda b, o, c: (0, 0)),
                      pl.BlockSpec(memory_space=pl.ANY)],
            out_specs=pl.BlockSpec((TB, D), lambda b, o, c: (b, 0)),
            scratch_shapes=[pltpu.VMEM((N_EXPERTS * NS, SUB, D), BF16),
                            pltpu.VMEM((TB, D), F32),
                            pltpu.SemaphoreType.DMA((N_EXPERTS * NS,))]),
        out_shape=jax.ShapeDtypeStruct((n, D), F32),
        compiler_params=_cparams(("arbitrary",), 40),
        name="moe_combine",
    )(off, cnt, gates_c, x3, mods_ffn, gpost, ys)


def _s_mix0_kernel(x_ref, st_ref, modm_ref, modf_ref, gpre_ref, gpost_ref, gpref_ref, pw_ref, ps_ref,
                   x1_ref, pool_ref, hf_ref, *, pos):
    sh, sc, gt = _split3(modm_ref[...])
    x = x_ref[...]
    hm = _rms_scale(x) * gpre_ref[...] * (1.0 + sc) + sh
    parts = []
    for g, w in enumerate(POOL_WINDOWS):
        lo = g * PG
        tok = hm[:, lo:lo + PG]
        s = tok
        for j in range(1, w):
            s = s + st_ref[POOL_STATE - j, :, lo:lo + PG]
        cnt = float(min(w, pos + 1))
        parts.append(_dot_hi(s / cnt - tok, pw_ref[g]))
    mix = jnp.concatenate(parts, axis=1) * ps_ref[...]
    x1 = x + gt * (_rms_scale(mix) * gpost_ref[...])
    x1_ref[...] = x1
    for j in range(POOL_STATE - 1):
        pool_ref[j] = st_ref[j + 1]
    pool_ref[POOL_STATE - 1] = hm
    shf, scf, _ = _split3(modf_ref[...])
    hf_ref[...] = _rms_scale(x1) * gpref_ref[...] * (1.0 + scf) + shf


def _s_mix0(x_s, state_t, modm, modf, gpre, gpost, gpref, pw, ps, pos):
    rows = x_s.shape[0]
    return pl.pallas_call(
        functools.partial(_s_mix0_kernel, pos=pos),
        out_shape=[jax.ShapeDtypeStruct((rows, D), F32),
                   jax.ShapeDtypeStruct((POOL_STATE, rows, D), F32),
                   jax.ShapeDtypeStruct((rows, D), F32)],
        compiler_params=pltpu.CompilerParams(vmem_limit_bytes=40 * MIB),
        name="s_mix0",
    )(x_s, state_t, modm, modf, gpre, gpost, gpref, pw, ps)


def _s_ffn_kernel(h_ref, w1_ref, w3_ref, w2_ref, f_ref):
    j = pl.program_id(0)
    h = h_ref[...]
    contrib = _dot_hi(_silu(_dot_hi(h, w1_ref[...])) * _dot_hi(h, w3_ref[...]), w2_ref[...])

    @pl.when(j == 0)
    def _():
        f_ref[...] = contrib

    @pl.when(j > 0)
    def _():
        f_ref[...] += contrib


def _s_ffn(h, w1, w3, w2):
    rows = h.shape[0]
    dff = w1.shape[1]
    fc = 256
    return pl.pallas_call(
        _s_ffn_kernel,
        grid=(dff // fc,),
        in_specs=[pl.BlockSpec((rows, D), lambda j: (0, 0)),
                  pl.BlockSpec((D, fc), lambda j: (0, j)),
                  pl.BlockSpec((D, fc), lambda j: (0, j)),
                  pl.BlockSpec((fc, D), lambda j: (j, 0))],
        out_specs=pl.BlockSpec((rows, D), lambda j: (0, 0)),
        out_shape=jax.ShapeDtypeStruct((rows, D), F32),
        compiler_params=_cparams(("arbitrary",), 32),
        name="s_ffn0",
    )(h, w1, w3, w2)


def _s_post0_kernel(x1_ref, f_ref, modf_ref, modm_ref, gpost_ref, gkv_ref, gpre1_ref,
                    x2_ref, kvin_ref, hm1_ref):
    _, _, gt = _split3(modf_ref[...])
    x2 = x1_ref[...] + gt * (_rms_scale(f_ref[...]) * gpost_ref[...])
    x2_ref[...] = x2
    xn = _rms_scale(x2)
    kvin_ref[...] = xn * gkv_ref[...]
    sh1, sc1, _ = _split3(modm_ref[...])
    hm1_ref[...] = xn * gpre1_ref[...] * (1.0 + sc1) + sh1


def _s_post0(x1, f, modf, modm, gpost, gkv, gpre1):
    rows = x1.shape[0]
    return pl.pallas_call(
        _s_post0_kernel,
        out_shape=[jax.ShapeDtypeStruct((rows, D), F32)] * 3,
        name="s_post0",
    )(x1, f, modf, modm, gpost, gkv, gpre1)


def _s_linear_kernel(h_ref, w_ref, cos_ref, sin_ref, o_ref):
    o_ref[...] = _rope_cols(_dot_hi(h_ref[...], w_ref[...]), cos_ref[...], sin_ref[...])


def _s_linear_plain_kernel(h_ref, w_ref, o_ref):
    o_ref[...] = _dot_hi(h_ref[...], w_ref[...])


def _s_linear(h, w, rope=None):
    rows, k = h.shape
    nw = w.shape[1]
    cw = 512
    in_specs = [pl.BlockSpec((rows, k), lambda j: (0, 0)), pl.BlockSpec((k, cw), lambda j: (0, j))]
    args = [h, w]
    body = _s_linear_plain_kernel
    if rope is not None:
        body = _s_linear_kernel
        in_specs += [pl.BlockSpec((1, cw), lambda j: (0, j)), pl.BlockSpec((1, cw), lambda j: (0, j))]
        args += [rope[0], rope[1]]
    return pl.pallas_call(
        body,
        grid=(nw // cw,),
        in_specs=in_specs,
        out_specs=pl.BlockSpec((rows, cw), lambda j: (0, j)),
        out_shape=jax.ShapeDtypeStruct((rows, nw), F32),
        compiler_params=_cparams(("arbitrary",), 32),
        name="s_linear",
    )(*args)


def _s_attn_kernel(q_ref, kn_ref, vn_ref, kc_ref, vc_ref, o_ref):
    n_past = kc_ref.shape[3]
    dist = n_past - lax.broadcasted_iota(I32, (1, n_past), 1)
    masks = [((dist & (dil - 1)) == 0) & (dist <= win) for win, dil in ATTN_GROUPS]
    for h in range(kc_ref.shape[1]):
        kc, vc = kc_ref[0, h], vc_ref[0, h]
        k_new, v_new = kn_ref[0, h], vn_ref[0, h]
        outs, lses = [], []
        for g in range(len(ATTN_GROUPS)):
            qg = q_ref[0, g, h]
            s_c = jnp.where(masks[g], jnp.sum(kc * qg, axis=0, keepdims=True) * SCALE, -jnp.inf)
            s_n = jnp.sum(k_new * qg, axis=0, keepdims=True) * SCALE
            m = jnp.maximum(jnp.max(s_c, axis=1, keepdims=True), s_n)
            p_c = jnp.exp(s_c - m)
            p_n = jnp.exp(s_n - m)
            den = jnp.sum(p_c, axis=1, keepdims=True) + p_n
            outs.append((jnp.sum(vc * p_c, axis=1, keepdims=True) + p_n * v_new) / den)
            lses.append(m + jnp.log(den))
        mx = jnp.maximum(jnp.maximum(lses[0], lses[1]), lses[2])
        es = [jnp.exp(l - mx) for l in lses]
        tot = es[0] + es[1] + es[2]
        out = (es[0] / tot) * outs[0]
        for g in range(1, len(ATTN_GROUPS)):
            out = out + (es[g] / tot) * outs[g]
        o_ref[0, h] = out


def _s_attn(q_s, k_new, v_new, cache_kt, cache_vt):
    rows, _, _, n_past = cache_kt.shape
    for win, dil in ATTN_GROUPS:
        assert n_past >= win and dil & (dil - 1) == 0
    col = pl.BlockSpec((1, HB_S, HD, 1), lambda b, h: (b, h, 0, 0))
    cache = pl.BlockSpec((1, HB_S, HD, n_past), lambda b, h: (b, h, 0, 0))
    return pl.pallas_call(
        _s_attn_kernel,
        grid=(rows, NH // HB_S),
        in_specs=[pl.BlockSpec((1, len(ATTN_GROUPS), HB_S, HD, 1), lambda b, h: (b, 0, h, 0, 0)),
                  col, col, cache, cache],
        out_specs=col,
        out_shape=jax.ShapeDtypeStruct((rows, NH, HD, 1), F32),
        compiler_params=_cparams(("arbitrary", "arbitrary"), 32),
        name="s_attn",
    )(q_s, k_new, v_new, cache_kt, cache_vt)


def _s_post1_kernel(x2_ref, mix_ref, modm_ref, modf_ref, gpost_ref, gpre_ref, rwt_ref, rb_ref,
                    x3_ref, hf_ref, gc_ref):
    _, _, gt = _split3(modm_ref[...])
    x3 = x2_ref[...] + gt * (_rms_scale(mix_ref[...]) * gpost_ref[...])
    x3_ref[...] = x3
    sh, sc, _ = _split3(modf_ref[...])
    hf = _rms_scale(x3) * gpre_ref[...] * (1.0 + sc) + sh
    hf_ref[...] = hf
    logits_t = lax.dot_general(rwt_ref[...], hf, (((1,), (1,)), ((), ())),
                               precision=HIGHEST, preferred_element_type=F32) + rb_ref[...]
    gc_ref[...] = _router_top2(logits_t)


def _s_post1(x2, mix, modm, modf, gpost, gpre, rwt, rb):
    rows = x2.shape[0]
    return pl.pallas_call(
        _s_post1_kernel,
        out_shape=[jax.ShapeDtypeStruct((rows, D), F32), jax.ShapeDtypeStruct((rows, D), F32),
                   jax.ShapeDtypeStruct((N_EXPERTS, rows), F32)],
        name="s_post1",
    )(x2, mix, modm, modf, gpost, gpre, rwt, rb)


def _s_moe_kernel(h_ref, g_ref, w1_ref, w3_ref, w2_ref, o_ref):
    e, j = pl.program_id(0), pl.program_id(1)
    h = h_ref[...]
    y = _dot_hi(_silu(_dot_hi(h, w1_ref[0])) * _dot_hi(h, w3_ref[0]), w2_ref[0])
    contrib = g_ref[0] * y

    @pl.when((e == 0) & (j == 0))
    def _():
        o_ref[...] = contrib

    @pl.when((e > 0) | (j > 0))
    def _():
        o_ref[...] += contrib


def _s_moe(h, gates_e, w1, w3, w2):
    rows = h.shape[0]
    dff = w1.shape[2]
    return pl.pallas_call(
        _s_moe_kernel,
        grid=(N_EXPERTS, dff // FCX),
        in_specs=[pl.BlockSpec((rows, D), lambda e, j: (0, 0)),
                  pl.BlockSpec((1, rows, 1), lambda e, j: (e, 0, 0)),
                  pl.BlockSpec((1, D, FCX), lambda e, j: (e, 0, j)),
                  pl.BlockSpec((1, D, FCX), lambda e, j: (e, 0, j)),
                  pl.BlockSpec((1, FCX, D), lambda e, j: (e, j, 0))],
        out_specs=pl.BlockSpec((rows, D), lambda e, j: (0, 0)),
        out_shape=jax.ShapeDtypeStruct((rows, D), F32),
        compiler_params=_cparams(("arbitrary", "arbitrary"), 40),
        name="s_moe",
    )(h, gates_e, w1, w3, w2)


def _s_post2_kernel(x3_ref, f_ref, modf_ref, gpost_ref, y_ref):
    _, _, gt = _split3(modf_ref[...])
    y_ref[...] = x3_ref[...] + gt * (_rms_scale(f_ref[...]) * gpost_ref[...])


def _s_post2(x3, f, modf, gpost):
    return pl.pallas_call(
        _s_post2_kernel,
        out_shape=jax.ShapeDtypeStruct(x3.shape, F32),
        name="s_post2",
    )(x3, f, modf, gpost)


def _rope_tables(pos):
    half = HD // 2
    inv = ROPE_THETA ** (-jnp.arange(half, dtype=F32) / half)
    ang = pos.astype(F32)[:, None] * inv[None, :]
    cos, sin = jnp.cos(ang), jnp.sin(ang)
    cos = jnp.tile(cos, (1, LANES // half))
    sin = jnp.tile(jnp.concatenate([-sin, sin], axis=1), (1, LANES // HD))
    return cos, sin


def kernel(x_prompt, x_sample, state_pool, cache_k, cache_v, c_prompt, c_sample, ada_mix_w, ada_mix_b, g_pre_mix, g_post_mix, pool_w, pool_scale, g_kv, w_kv, w_q, w_o, ada_ffn_w, ada_ffn_b, g_pre_ffn, g_post_ffn, ffn_w1, ffn_w3, ffn_w2, router_w, router_b, moe_w1, moe_w3, moe_w2):
    b, t, _ = x_prompt.shape
    nsmp = x_sample.shape[0]
    n = b * t
    past_len = 16384
    assert x_sample.shape[1] == 1 and g_pre_mix.shape[0] == 2 and t % (ATTN_GROUPS[-1][1] * BLK) == 0
    vec = lambda a: a.reshape(1, D)

    c_all = jnp.concatenate([c_prompt, c_sample], axis=0)
    mods_mix = _ada(c_all, ada_mix_w, ada_mix_b)
    mods_ffn = _ada(c_all, ada_ffn_w, ada_ffn_b)
    pm_mix = mods_mix[:, :b].reshape(2, b, 1, 3 * D)
    pm_ffn = mods_ffn[:, :b].reshape(2, b, 1, 3 * D)
    sm_mix = mods_mix[:, b:]
    sm_ffn = mods_ffn[:, b:]

    rwt = router_w[0].T
    rb = router_b[0].reshape(N_EXPERTS, 1)

    x1, pool16 = _mix0(x_prompt, pm_mix, vec(g_pre_mix[0]), vec(g_post_mix[0]),
                       pool_w[0].astype(BF16), vec(pool_scale[0]))
    cos_p, sin_p = _rope_tables(jnp.arange(t, dtype=jnp.int32))
    x2 = _ffn0(x1.reshape(n, D), pm_ffn, vec(g_pre_ffn[0]), vec(g_post_ffn[0]),
               ffn_w1[0].astype(BF16), ffn_w3[0].astype(BF16), ffn_w2[0].astype(BF16), t)
    kt_p, vt_p, *kv_l = _kvproj(x2, vec(g_kv), cos_p, sin_p, w_kv.astype(BF16), b, t)
    q_l = _qproj(x2, pm_mix, vec(g_pre_mix[1]), cos_p, sin_p, w_q[0].astype(BF16), b, t)
    attn = _attention(q_l, kv_l[:3], kv_l[3:], b, t)
    x3, hf1, gates_t, gates_c = _post1(attn, x2, pm_mix, pm_ffn, vec(g_post_mix[1]),
                                       vec(g_pre_ffn[1]), w_o[0].astype(BF16), rwt, rb, t)
    off, cnt, tile_e, n_used = _route_plan(gates_t, n)
    xs = _dispatch(off, cnt, hf1, gates_t)
    ys = _experts(tile_e, n_used, xs, moe_w1[0], moe_w3[0], moe_w2[0])
    y_prompt = _combine(off, cnt, gates_c, x3, pm_ffn, vec(g_post_ffn[1]), ys, t).reshape(b, t, D)

    state_t = jnp.transpose(state_pool[:, 0], (1, 0, 2))
    x1s, pool_t, hf0s = _s_mix0(x_sample.reshape(nsmp, D), state_t, sm_mix[0], sm_ffn[0],
                                vec(g_pre_mix[0]), vec(g_post_mix[0]), vec(g_pre_ffn[0]),
                                pool_w[0], vec(pool_scale[0]), past_len)
    f0s = _s_ffn(hf0s, ffn_w1[0], ffn_w3[0], ffn_w2[0])
    x2s, kvin_s, hm1s = _s_post0(x1s, f0s, sm_ffn[0], sm_mix[1], vec(g_post_ffn[0]), vec(g_kv), vec(g_pre_mix[1]))
    cos_s, sin_s = _rope_tables(jnp.full((1,), past_len, jnp.int32))
    ones, zeros = jnp.ones((1, D), F32), jnp.zeros((1, D), F32)
    kv_s = _s_linear(kvin_s, w_kv, (jnp.concatenate([jnp.tile(cos_s, (1, D // LANES)), ones], axis=1),
                                    jnp.concatenate([jnp.tile(sin_s, (1, D // LANES)), zeros], axis=1)))
    q_s = _s_linear(hm1s, w_q[0], (jnp.tile(cos_s, (1, 3 * D // LANES)), jnp.tile(sin_s, (1, 3 * D // LANES))))
    k_s = kv_s[:, :D].reshape(nsmp, NH, HD)
    v_s = kv_s[:, D:].reshape(nsmp, NH, HD)
    cache_kt = jnp.transpose(cache_k, (0, 2, 3, 1))
    cache_vt = jnp.transpose(cache_v, (0, 2, 3, 1))
    attn_s = _s_attn(q_s.reshape(nsmp, len(ATTN_GROUPS), NH, HD, 1), k_s[..., None], v_s[..., None],
                     cache_kt, cache_vt).reshape(nsmp, D)
    mix1s = _s_linear(attn_s, w_o[0])
    x3s, hf1s, gts = _s_post1(x2s, mix1s, sm_mix[1], sm_ffn[1], vec(g_post_mix[1]), vec(g_pre_ffn[1]), rwt, rb)
    moe_s = _s_moe(hf1s, gts.reshape(N_EXPERTS, nsmp, 1), moe_w1[0], moe_w3[0], moe_w2[0])
    y_sample = _s_post2(x3s, moe_s, sm_ffn[1], vec(g_post_ffn[1])).reshape(nsmp, 1, D)

    pool_prompt = pool16[:, None, 1:, :]
    pool_sample = jnp.transpose(pool_t, (1, 0, 2))[:, None]
    return (y_prompt, y_sample, pool_prompt, pool_sample,
            jnp.transpose(kt_p, (0, 3, 1, 2)), jnp.transpose(vt_p, (0, 3, 1, 2)),
            k_s[:, None], v_s[:, None])
```

```python
import functools

import jax
import jax.numpy as jnp
import numpy as np
from jax import lax
from jax.experimental import pallas as pl
from jax.experimental.pallas import tpu as pltpu

F32, BF16, I32 = jnp.float32, jnp.bfloat16, jnp.int32
HIGHEST = lax.Precision.HIGHEST

D = 1024
POOL_WINDOWS = (2, 4, 8, 16)
PG = D // len(POOL_WINDOWS)
POOL_STATE = max(POOL_WINDOWS) - 1
HALO = POOL_STATE + 1
ATTN_GROUPS = ((128, 1), (512, 4), (2048, 16))
NH = 16
HD = 64
BLK = 128
ROPE_THETA = 10000.0
N_EXPERTS = 8
EPS = 1e-6
SCALE = HD ** -0.5
LANES = 128
MIB = 1024 * 1024

TT_MIX = 512
TM_FFN = 1024
HB_S = 8
ATTN_UNITS = 4
FC_FFN = 256
TM_POST = 512
TB = 512
SUB = 128
NS = TB // SUB
TMX = 1024
FCX = 512
ROW_ALIGN = 16


def _cparams(sem, vmem_mib):
    return pltpu.CompilerParams(dimension_semantics=sem, vmem_limit_bytes=vmem_mib * MIB)


def _rms_scale(x):
    return x * lax.rsqrt(jnp.mean(x * x, axis=-1, keepdims=True) + EPS)


def _split3(m):
    return m[:, :D], m[:, D:2 * D], m[:, 2 * D:]


def _silu(x):
    return x * jax.nn.sigmoid(x)


def _rope_cols(x, cos, sin):
    lane = lax.broadcasted_iota(I32, (x.shape[0], LANES), 1)
    first = (lane % HD) < (HD // 2)
    shared = cos.shape[1] == LANES
    outs = []
    for c in range(x.shape[1] // LANES):
        cols = slice(c * LANES, (c + 1) * LANES)
        xc = x[:, cols]
        swapped = jnp.where(first, pltpu.roll(xc, LANES - HD // 2, 1), pltpu.roll(xc, HD // 2, 1))
        outs.append(xc * (cos if shared else cos[:, cols]) + swapped * (sin if shared else sin[:, cols]))
    return jnp.concatenate(outs, axis=1)


def _dot_hi(a, b):
    return jnp.dot(a, b, precision=HIGHEST, preferred_element_type=F32)


def _dot(a, b):
    return jnp.dot(a, b, preferred_element_type=F32)


def _ada_kernel(c_ref, w_ref, b_ref, o_ref):
    o_ref[0] = _dot_hi(_silu(c_ref[...]), w_ref[0]) + b_ref[0]


def _ada(c_all, w, b):
    nl, _, n3 = w.shape
    rows = c_all.shape[0]
    cw = 768
    return pl.pallas_call(
        _ada_kernel,
        grid=(nl, n3 // cw),
        in_specs=[pl.BlockSpec((rows, D), lambda l, j: (0, 0)),
                  pl.BlockSpec((1, D, cw), lambda l, j: (l, 0, j)),
                  pl.BlockSpec((1, 1, cw), lambda l, j: (l, 0, j))],
        out_specs=pl.BlockSpec((1, rows, cw), lambda l, j: (l, 0, j)),
        out_shape=jax.ShapeDtypeStruct((nl, rows, n3), F32),
        compiler_params=_cparams(("arbitrary", "arbitrary"), 32),
        name="ada_mod",
    )(c_all, w, b.reshape(nl, 1, n3))


def _mix0_kernel(x_ref, xh_ref, mod_ref, gpre_ref, gpost_ref, pw_ref, ps_ref,
                 x1_ref, pool_ref, ext_ref):
    i = pl.program_id(1)
    tt = x_ref.shape[1]
    sh, sc, gt = _split3(mod_ref[0, 0])
    gpre = gpre_ref[...]
    x = x_ref[0]
    hm = _rms_scale(x) * gpre * (1.0 + sc) + sh
    hh = _rms_scale(xh_ref[0]) * gpre * (1.0 + sc) + sh
    ext_ref[0:HALO, :] = jnp.where(i > 0, hh, 0.0)
    ext_ref[HALO:, :] = hm
    t = i * tt + lax.broadcasted_iota(I32, (tt, 1), 0)
    parts = []
    for g, w in enumerate(POOL_WINDOWS):
        lo = g * PG
        tok = hm[:, lo:lo + PG]
        s = tok
        for j in range(1, w):
            s = s + ext_ref[HALO - j:HALO - j + tt, lo:lo + PG]
        cnt = jnp.minimum(w, t + 1).astype(F32)
        parts.append(_dot((s / cnt - tok).astype(BF16), pw_ref[g]))
    mix = jnp.concatenate(parts, axis=1) * ps_ref[...]
    x1_ref[0] = x + gt * (_rms_scale(mix) * gpost_ref[...])

    @pl.when(i == pl.num_programs(1) - 1)
    def _():
        pool_ref[0] = ext_ref[tt:tt + HALO, :]


def _mix0(x, mods, gpre, gpost, pw_bf, ps):
    b, t, _ = x.shape
    tt = TT_MIX
    hb = tt // HALO
    return pl.pallas_call(
        _mix0_kernel,
        grid=(b, t // tt),
        in_specs=[pl.BlockSpec((1, tt, D), lambda bb, i: (bb, i, 0)),
                  pl.BlockSpec((1, HALO, D), lambda bb, i: (bb, jnp.maximum(i * hb - 1, 0), 0)),
                  pl.BlockSpec((1, 1, 1, 3 * D), lambda bb, i: (0, bb, 0, 0)),
                  pl.BlockSpec((1, D), lambda bb, i: (0, 0)),
                  pl.BlockSpec((1, D), lambda bb, i: (0, 0)),
                  pl.BlockSpec((len(POOL_WINDOWS), PG, PG), lambda bb, i: (0, 0, 0)),
                  pl.BlockSpec((1, D), lambda bb, i: (0, 0))],
        out_specs=[pl.BlockSpec((1, tt, D), lambda bb, i: (bb, i, 0)),
                   pl.BlockSpec((1, HALO, D), lambda bb, i: (bb, 0, 0))],
        out_shape=[jax.ShapeDtypeStruct((b, t, D), F32),
                   jax.ShapeDtypeStruct((b, HALO, D), F32)],
        scratch_shapes=[pltpu.VMEM((tt + HALO, D), F32)],
        compiler_params=_cparams(("arbitrary", "arbitrary"), 40),
        name="mix0",
    )(x, x, mods, gpre, gpost, pw_bf, ps)


def _store_dilated(val, plane_ref, dils, out_refs):
    tm = val.shape[0]
    for hp in range(NH // 2):
        plane_ref[hp] = val[:, hp * LANES:(hp + 1) * LANES]
    for dil, out_ref in zip(dils, out_refs):
        for hp in range(NH // 2):
            for r in range(dil):
                out_ref[0, hp, r] = plane_ref[hp, pl.ds(r, tm // dil, stride=dil), :].astype(BF16)


def _dilated_specs(b, t, tm):
    per_b = t // tm
    specs = [pl.BlockSpec((1, NH // 2, dil, tm // dil, LANES), lambda i: (i // per_b, 0, 0, i % per_b, 0))
             for _, dil in ATTN_GROUPS]
    shapes = [jax.ShapeDtypeStruct((b, NH // 2, dil, t // dil, LANES), BF16) for _, dil in ATTN_GROUPS]
    return specs, shapes


def _ffn0_kernel(x1_ref, modf_ref, gpre_ref, gpost_ref, w1_ref, w3_ref, w2_ref, x2_ref, acc_ref):
    sh, sc, gt = _split3(modf_ref[0, 0])
    x1 = x1_ref[...]
    hf = (_rms_scale(x1) * gpre_ref[...] * (1.0 + sc) + sh).astype(BF16)
    for c in range(w1_ref.shape[1] // FC_FFN):
        cs = slice(c * FC_FFN, (c + 1) * FC_FFN)
        a = (_silu(_dot(hf, w1_ref[:, cs])) * _dot(hf, w3_ref[:, cs])).astype(BF16)
        contrib = _dot(a, w2_ref[cs, :])
        if c == 0:
            acc_ref[...] = contrib
        else:
            acc_ref[...] += contrib
    x2 = x1 + gt * (_rms_scale(acc_ref[...]) * gpost_ref[...])
    x2_ref[...] = x2


def _ffn0(x1, mods_ffn, gpre, gpost, w1, w3, w2, t):
    n = x1.shape[0]
    tm = TM_FFN
    per_b = t // tm
    dff = w1.shape[1]
    row = lambda i: (i, 0)
    fixed = lambda i: (0, 0)
    resident = lambda shape: pl.BlockSpec(shape, fixed, pipeline_mode=pl.Buffered(1))
    vec = pl.BlockSpec((1, D), fixed)
    return pl.pallas_call(
        _ffn0_kernel,
        grid=(n // tm,),
        in_specs=[pl.BlockSpec((tm, D), row),
                  pl.BlockSpec((1, 1, 1, 3 * D), lambda i: (0, i // per_b, 0, 0)),
                  vec, vec,
                  resident((D, dff)), resident((D, dff)), resident((dff, D))],
        out_specs=pl.BlockSpec((tm, D), row),
        out_shape=jax.ShapeDtypeStruct((n, D), F32),
        scratch_shapes=[pltpu.VMEM((tm, D), F32)],
        compiler_params=_cparams(("arbitrary",), 56),
        name="ffn0",
    )(x1, mods_ffn, gpre, gpost, w1, w3, w2)


def _kvproj_kernel(x2_ref, gkv_ref, cos_ref, sin_ref, wkv_ref,
                   kt_ref, vt_ref, k0_ref, k1_ref, k2_ref, v0_ref, v1_ref, v2_ref, plane_ref):
    tm = x2_ref.shape[0]
    kv = _dot((_rms_scale(x2_ref[...]) * gkv_ref[...]).astype(BF16), wkv_ref[...])
    k = _rope_cols(kv[:, :D], cos_ref[...], sin_ref[...])
    v = kv[:, D:]
    kt_ref[0] = k.T.reshape(NH, HD, tm)
    vt_ref[0] = v.T.reshape(NH, HD, tm)
    dils = [dil for _, dil in ATTN_GROUPS]
    _store_dilated(k, plane_ref, dils, (k0_ref, k1_ref, k2_ref))
    _store_dilated(v, plane_ref, dils, (v0_ref, v1_ref, v2_ref))


def _kvproj(x2, gkv, cos, sin, wkv, b, t):
    n = x2.shape[0]
    tm = TM_POST
    per_b = t // tm
    fixed = lambda i: (0, 0)
    dspecs, dshapes = _dilated_specs(b, t, tm)
    tspec = pl.BlockSpec((1, NH, HD, tm), lambda i: (i // per_b, 0, 0, i % per_b))
    tshape = jax.ShapeDtypeStruct((b, NH, HD, t), F32)
    return pl.pallas_call(
        _kvproj_kernel,
        grid=(n // tm,),
        in_specs=[pl.BlockSpec((tm, D), lambda i: (i, 0)),
                  pl.BlockSpec((1, D), fixed),
                  pl.BlockSpec((tm, LANES), lambda i: (i % per_b, 0)),
                  pl.BlockSpec((tm, LANES), lambda i: (i % per_b, 0)),
                  pl.BlockSpec((D, 2 * D), fixed, pipeline_mode=pl.Buffered(1))],
        out_specs=[tspec, tspec] + dspecs + dspecs,
        out_shape=[tshape, tshape] + dshapes + dshapes,
        scratch_shapes=[pltpu.VMEM((NH // 2, tm, LANES), F32)],
        compiler_params=_cparams(("arbitrary",), 48),
        name="kv_proj",
    )(x2, gkv, cos, sin, wkv)


def _qproj_kernel(x2_ref, modm_ref, gpre1_ref, cos_ref, sin_ref, wq_ref, q0_ref, q1_ref, q2_ref, scr_ref):
    sh1, sc1, _ = _split3(modm_ref[0, 0])
    hm1 = (_rms_scale(x2_ref[...]) * gpre1_ref[...] * (1.0 + sc1) + sh1).astype(BF16)
    q = _rope_cols(_dot(hm1, wq_ref[...]), cos_ref[...], sin_ref[...]) * SCALE
    for g, ((_, dil), q_ref) in enumerate(zip(ATTN_GROUPS, (q0_ref, q1_ref, q2_ref))):
        _store_dilated(q[:, g * D:(g + 1) * D], scr_ref, (dil,), (q_ref,))


def _qproj(x2, mods_mix, gpre1, cos, sin, wq, b, t):
    n = x2.shape[0]
    tm = TM_POST
    per_b = t // tm
    fixed = lambda i: (0, 0)
    dspecs, dshapes = _dilated_specs(b, t, tm)
    return pl.pallas_call(
        _qproj_kernel,
        grid=(n // tm,),
        in_specs=[pl.BlockSpec((tm, D), lambda i: (i, 0)),
                  pl.BlockSpec((1, 1, 1, 3 * D), lambda i: (1, i // per_b, 0, 0)),
                  pl.BlockSpec((1, D), fixed),
                  pl.BlockSpec((tm, LANES), lambda i: (i % per_b, 0)),
                  pl.BlockSpec((tm, LANES), lambda i: (i % per_b, 0)),
                  pl.BlockSpec((D, 3 * D), fixed, pipeline_mode=pl.Buffered(1))],
        out_specs=dspecs,
        out_shape=dshapes,
        scratch_shapes=[pltpu.VMEM((NH // 2, tm, LANES), F32)],
        compiler_params=_cparams(("arbitrary",), 48),
        name="q_proj",
    )(x2, mods_mix, gpre1, cos, sin, wq)


def _attn_kernel(q0_ref, q1_ref, q2_ref, k0_ref, k1_ref, k2_ref, v0_ref, v1_ref, v2_ref,
                 o_ref, og_ref, lg_ref, bias_ref):
    t = o_ref.shape[2]
    first_head = lax.broadcasted_iota(I32, (BLK, LANES), 1) < HD
    nt = (((1,), (1,)), ((), ()))

    @pl.when((pl.program_id(0) == 0) & (pl.program_id(1) == 0))
    def _():
        qi = lax.broadcasted_iota(I32, (2 * BLK, 2 * BLK), 0) % BLK
        kj = lax.broadcasted_iota(I32, (2 * BLK, 2 * BLK), 1)
        for v in range(2):
            dist = v * BLK + qi - kj
            bias_ref[v] = jnp.where((dist >= 0) & (dist <= BLK), 0.0, -jnp.inf)

    groups = zip(ATTN_GROUPS, (q0_ref, q1_ref, q2_ref), (k0_ref, k1_ref, k2_ref), (v0_ref, v1_ref, v2_ref))
    for g, ((win, dil), q_ref, k_ref, v_ref) in enumerate(groups):
        assert win // dil == BLK
        sub = t // dil
        nb = sub // BLK
        kw = min(2 * BLK, sub)

        def scores(u, nb=nb, kw=kw, q_ref=q_ref, k_ref=k_ref):
            r = u // nb
            n = u % nb
            q_lo = pl.multiple_of(n * BLK, BLK)
            k_lo = pl.multiple_of(jnp.maximum(n - 1, 0) * BLK, BLK)
            qp = q_ref[0, 0, r, pl.ds(q_lo, BLK), :]
            zero = jnp.zeros_like(qp)
            qs = jnp.concatenate([jnp.where(first_head, qp, zero), jnp.where(first_head, zero, qp)], axis=0)
            s = lax.dot_general(qs, k_ref[0, 0, r, pl.ds(k_lo, kw), :], nt, preferred_element_type=F32)
            return s + bias_ref[jnp.minimum(n, 1), :, 0:kw], r, q_lo, k_lo

        def finish(s, r, q_lo, k_lo, dil=dil, kw=kw, v_ref=v_ref, g=g):
            m = jnp.max(s, axis=-1, keepdims=True)
            p = jnp.exp(s - m)
            den = jnp.sum(p, axis=-1, keepdims=True)
            o2 = _dot(p.astype(BF16), v_ref[0, 0, r, pl.ds(k_lo, kw), :]) * (1.0 / den)
            l2 = jnp.broadcast_to(m + jnp.log(den), (2 * BLK, LANES))
            rows = pl.ds(q_lo * dil + r, BLK, stride=dil) if dil > 1 else pl.ds(q_lo, BLK)
            og_ref[g, rows, :] = jnp.where(first_head, o2[:BLK], o2[BLK:])
            lg_ref[g, rows, :] = jnp.where(first_head, l2[:BLK], l2[BLK:])

        def trip(i, carry, scores=scores, finish=finish):
            started = [scores(i * ATTN_UNITS + j) for j in range(ATTN_UNITS)]
            for args in started:
                finish(*args)
            return carry

        lax.fori_loop(0, dil * nb // ATTN_UNITS, trip, 0)

    mt = 2 * BLK

    def merge(c, carry):
        rows = pl.ds(pl.multiple_of(c * mt, mt), mt)
        ls = [lg_ref[g, rows, :] for g in range(len(ATTN_GROUPS))]
        mx = jnp.maximum(jnp.maximum(ls[0], ls[1]), ls[2])
        es = [jnp.exp(l - mx) for l in ls]
        tot = es[0] + es[1] + es[2]
        out = (es[0] / tot) * og_ref[0, rows, :]
        for g in range(1, len(ATTN_GROUPS)):
            out = out + (es[g] / tot) * og_ref[g, rows, :]
        o_ref[0, 0, rows, :] = out.astype(BF16)
        return carry

    lax.fori_loop(0, t // mt, merge, 0)


def _attention(q_l, k_l, v_l, b, t):
    specs = [pl.BlockSpec((1, 1, dil, t // dil, LANES), lambda bb, hp: (bb, hp, 0, 0, 0))
             for _, dil in ATTN_GROUPS]
    return pl.pallas_call(
        _attn_kernel,
        grid=(b, NH // 2),
        in_specs=specs * 3,
        out_specs=pl.BlockSpec((1, 1, t, LANES), lambda bb, hp: (bb, hp, 0, 0)),
        out_shape=jax.ShapeDtypeStruct((b, NH // 2, t, LANES), BF16),
        scratch_shapes=[pltpu.VMEM((len(ATTN_GROUPS), t, LANES), F32),
                        pltpu.VMEM((len(ATTN_GROUPS), t, LANES), F32),
                        pltpu.VMEM((2, 2 * BLK, 2 * BLK), F32)],
        compiler_params=_cparams(("arbitrary", "arbitrary"), 40),
        name="dilated_attn",
    )(*q_l, *k_l, *v_l)


def _router_top2(logits_t):
    idx = lax.broadcasted_iota(I32, logits_t.shape, 0)
    v1 = jnp.max(logits_t, axis=0, keepdims=True)
    i1 = jnp.min(jnp.where(logits_t == v1, idx, N_EXPERTS), axis=0, keepdims=True)
    m1 = idx == i1
    rest = jnp.where(m1, -jnp.inf, logits_t)
    v2 = jnp.max(rest, axis=0, keepdims=True)
    i2 = jnp.min(jnp.where(rest == v2, idx, N_EXPERTS), axis=0, keepdims=True)
    m2 = idx == i2
    e2 = jnp.exp(v2 - v1)
    den = 1.0 + e2
    return jnp.where(m1, 1.0 / den, 0.0) + jnp.where(m2, e2 / den, 0.0)


def _post1_kernel(a_ref, x2_ref, modm_ref, modf_ref, gpost_ref, gpre_ref, wo_ref, rwt_ref, rb_ref,
                  x3_ref, hf_ref, gt_ref, gc_ref):
    o = jnp.concatenate([a_ref[0, hp] for hp in range(NH // 2)], axis=1)
    mix = _dot(o, wo_ref[...])
    _, _, gt = _split3(modm_ref[0, 0])
    x3 = x2_ref[...] + gt * (_rms_scale(mix) * gpost_ref[...])
    x3_ref[...] = x3
    sh, sc, _ = _split3(modf_ref[0, 0])
    hf = _rms_scale(x3) * gpre_ref[...] * (1.0 + sc) + sh
    hf_ref[...] = hf.astype(BF16)
    logits_t = lax.dot_general(rwt_ref[...], hf, (((1,), (1,)), ((), ())),
                               precision=HIGHEST, preferred_element_type=F32) + rb_ref[...]
    gates_t = _router_top2(logits_t)
    gt_ref[...] = gates_t
    padded = jnp.concatenate([gates_t, jnp.zeros((LANES - N_EXPERTS, gates_t.shape[1]), F32)], axis=0)
    gc_ref[...] = padded.T


def _post1(attn, x2, mods_mix, mods_ffn, gpost, gpre, wo_bf, rwt, rb, t):
    n = x2.shape[0]
    tm = TM_POST
    per_b = t // tm
    row = lambda i: (i, 0)
    fixed = lambda i: (0, 0)
    vec = pl.BlockSpec((1, D), fixed)
    tile = pl.BlockSpec((tm, D), row)
    return pl.pallas_call(
        _post1_kernel,
        grid=(n // tm,),
        in_specs=[pl.BlockSpec((1, NH // 2, tm, LANES), lambda i: (i // per_b, 0, i % per_b, 0)),
                  tile,
                  pl.BlockSpec((1, 1, 1, 3 * D), lambda i: (1, i // per_b, 0, 0)),
                  pl.BlockSpec((1, 1, 1, 3 * D), lambda i: (1, i // per_b, 0, 0)),
                  vec, vec,
                  pl.BlockSpec((D, D), fixed),
                  pl.BlockSpec((N_EXPERTS, D), fixed),
                  pl.BlockSpec((N_EXPERTS, 1), fixed)],
        out_specs=[tile, tile, pl.BlockSpec((N_EXPERTS, tm), lambda i: (0, i)),
                   pl.BlockSpec((tm, LANES), row)],
        out_shape=[jax.ShapeDtypeStruct((n, D), F32), jax.ShapeDtypeStruct((n, D), BF16),
                   jax.ShapeDtypeStruct((N_EXPERTS, n), F32), jax.ShapeDtypeStruct((n, LANES), F32)],
        compiler_params=_cparams(("arbitrary",), 40),
        name="wo_router",
    )(attn, x2, mods_mix, mods_ffn, gpost, gpre, wo_bf, rwt, rb)


def _route_plan(gates_t, n):
    nb = n // TB
    cnt = jnp.sum((gates_t > 0.0).reshape(N_EXPERTS, nb, TB), axis=-1).astype(I32)
    c_al = (cnt + ROW_ALIGN - 1) // ROW_ALIGN * ROW_ALIGN
    within = jnp.cumsum(c_al, axis=1) - c_al
    tot = jnp.sum(c_al, axis=1)
    region = (tot + SUB + TMX - 1) // TMX * TMX
    start = jnp.cumsum(region) - region
    off = (start[:, None] + within).T.reshape(-1)
    n_used = (jnp.sum(region) // TMX).astype(I32)
    tile_row = jnp.arange(_max_sorted_rows(n) // TMX, dtype=I32) * TMX
    tile_e = jnp.sum(tile_row[:, None] >= (start + region)[None, :], axis=1).astype(I32)
    tile_e = jnp.minimum(tile_e, N_EXPERTS - 1)
    last_e = tile_e[jnp.maximum(n_used - 1, 0)]
    tile_e = jnp.where(jnp.arange(tile_e.shape[0]) < n_used, tile_e, last_e)
    return off.astype(I32), cnt.T.reshape(-1), tile_e, n_used.reshape(1)


def _max_sorted_rows(n):
    nb = n // TB
    worst = 2 * n + nb * N_EXPERTS * (ROW_ALIGN - 1) + N_EXPERTS * (SUB + TMX - 1)
    return (worst + TMX - 1) // TMX * TMX


def _dispatch_kernel(off_ref, cnt_ref, hf_ref, gt_ref, xs_in_ref, xs_ref, stage_ref, sem, *, block0):
    del xs_in_ref
    b = pl.program_id(0) + block0
    hf = hf_ref[...]
    sel = gt_ref[...] > 0.0
    before = lax.broadcasted_iota(I32, (TB, TB), 0) < lax.broadcasted_iota(I32, (TB, TB), 1)
    rank = _dot(sel.astype(F32), before.astype(F32)).astype(I32)
    jj = lax.broadcasted_iota(I32, (SUB, TB), 0)

    def copy(e, s):
        dst = pl.multiple_of(off_ref[b * N_EXPERTS + e] + s * SUB, ROW_ALIGN)
        return pltpu.make_async_copy(stage_ref.at[e * NS + s], xs_ref.at[pl.ds(dst, SUB)], sem.at[e * NS + s])

    for e in range(N_EXPERTS):
        c = cnt_ref[b * N_EXPERTS + e]
        for s in range(NS):
            @pl.when(s * SUB < c)
            def _(e=e, s=s):
                pick = ((rank[e:e + 1, :] == jj + s * SUB) & sel[e:e + 1, :]).astype(BF16)
                stage_ref[e * NS + s] = _dot(pick, hf).astype(BF16)
                copy(e, s).start()
    for e in range(N_EXPERTS):
        c = cnt_ref[b * N_EXPERTS + e]
        for s in range(NS):
            @pl.when(s * SUB < c)
            def _(e=e, s=s):
                copy(e, s).wait()


def _dispatch(off, cnt, hf_bf, gates_t, xs, block0):
    n = hf_bf.shape[0]
    rows = xs.shape[0]
    return pl.pallas_call(
        functools.partial(_dispatch_kernel, block0=block0),
        grid_spec=pltpu.PrefetchScalarGridSpec(
            num_scalar_prefetch=2,
            grid=(n // TB,),
            in_specs=[pl.BlockSpec((TB, D), lambda b, o, c: (b, 0)),
                      pl.BlockSpec((N_EXPERTS, TB), lambda b, o, c: (0, b)),
                      pl.BlockSpec(memory_space=pl.ANY)],
            out_specs=pl.BlockSpec(memory_space=pl.ANY),
            scratch_shapes=[pltpu.VMEM((N_EXPERTS * NS, SUB, D), BF16),
                            pltpu.SemaphoreType.DMA((N_EXPERTS * NS,))]),
        out_shape=jax.ShapeDtypeStruct((rows, D), BF16),
        input_output_aliases={4: 0},
        compiler_params=_cparams(("arbitrary",), 40),
        name="moe_dispatch",
    )(off, cnt, hf_bf, gates_t, xs)


def _expert_kernel(te_ref, nu_ref, x_ref, w1_ref, w3_ref, w2_ref, y_ref, acc_ref):
    del te_ref
    i, j = pl.program_id(0), pl.program_id(1)

    @pl.when(i < nu_ref[0])
    def _():
        x = x_ref[...]
        a = (_silu(_dot(x, w1_ref[0].astype(BF16))) * _dot(x, w3_ref[0].astype(BF16))).astype(BF16)
        contrib = _dot(a, w2_ref[0].astype(BF16))

        @pl.when(j == 0)
        def _():
            acc_ref[...] = contrib

        @pl.when(j > 0)
        def _():
            acc_ref[...] += contrib

        @pl.when(j == pl.num_programs(1) - 1)
        def _():
            y_ref[...] = acc_ref[...].astype(BF16)

    @pl.when((i >= nu_ref[0]) & (j == 0))
    def _():
        y_ref[...] = jnp.zeros(y_ref.shape, BF16)


def _experts(tile_e, n_used, xs, w1, w3, w2):
    rows = xs.shape[0]
    dff = w1.shape[2]
    nj = dff // FCX

    def tile_idx(i, j, te, nu):
        return (jnp.minimum(i, nu[0] - 1), 0)

    def chunk(i, j, nu):
        return jnp.where(i < nu[0], j, nj - 1)

    return pl.pallas_call(
        _expert_kernel,
        grid_spec=pltpu.PrefetchScalarGridSpec(
            num_scalar_prefetch=2,
            grid=(rows // TMX, nj),
            in_specs=[pl.BlockSpec((TMX, D), tile_idx),
                      pl.BlockSpec((1, D, FCX), lambda i, j, te, nu: (te[i], 0, chunk(i, j, nu))),
                      pl.BlockSpec((1, D, FCX), lambda i, j, te, nu: (te[i], 0, chunk(i, j, nu))),
                      pl.BlockSpec((1, FCX, D), lambda i, j, te, nu: (te[i], chunk(i, j, nu), 0))],
            out_specs=pl.BlockSpec((TMX, D), lambda i, j, te, nu: (i, 0)),
            scratch_shapes=[pltpu.VMEM((TMX, D), F32)]),
        out_shape=jax.ShapeDtypeStruct((rows, D), BF16),
        compiler_params=_cparams(("arbitrary", "arbitrary"), 56),
        name="moe_experts",
    )(tile_e, n_used, xs, w1, w3, w2)


def _gather_expert_rows(b, off_ref, cnt_ref, gc_ref, ys_ref, ybuf_ref, acc_ref, sem):
    def copy(e, s):
        src = pl.multiple_of(off_ref[b * N_EXPERTS + e] + s * SUB, ROW_ALIGN)
        return pltpu.make_async_copy(ys_ref.at[pl.ds(src, SUB)], ybuf_ref.at[e * NS + s], sem.at[e * NS + s])

    for e in range(N_EXPERTS):
        c = cnt_ref[b * N_EXPERTS + e]
        for s in range(NS):
            @pl.when(s * SUB < c)
            def _(e=e, s=s):
                copy(e, s).start()

    gate = gc_ref[...]
    sel = gate > 0.0
    before = lax.broadcasted_iota(I32, (TB, TB), 1) < lax.broadcasted_iota(I32, (TB, TB), 0)
    rank = _dot(before.astype(F32), sel.astype(F32)).astype(I32)
    lane = lax.broadcasted_iota(I32, (TB, SUB), 1)
    acc_ref[...] = jnp.zeros((TB, D), F32)
    for e in range(N_EXPERTS):
        c = cnt_ref[b * N_EXPERTS + e]
        for s in range(NS):
            @pl.when(s * SUB < c)
            def _(e=e, s=s):
                copy(e, s).wait()
                place = ((rank[:, e:e + 1] == lane + s * SUB) & sel[:, e:e + 1]).astype(BF16)
                acc_ref[...] += gate[:, e:e + 1] * _dot(place, ybuf_ref[e * NS + s])


_COMBINE_SCRATCH = [pltpu.VMEM((N_EXPERTS * NS, SUB, D), BF16),
                    pltpu.VMEM((TB, D), F32),
                    pltpu.SemaphoreType.DMA((N_EXPERTS * NS,))]


def _combine_kernel(off_ref, cnt_ref, gc_ref, x3_ref, mod_ref, gpost_ref, ys_ref,
                    out_ref, ybuf_ref, acc_ref, sem):
    _gather_expert_rows(pl.program_id(0), off_ref, cnt_ref, gc_ref, ys_ref, ybuf_ref, acc_ref, sem)
    _, _, gt = _split3(mod_ref[0, 0])
    out_ref[...] = x3_ref[...] + gt * (_rms_scale(acc_ref[...]) * gpost_ref[...])


def _combine(off, cnt, gates_c, x3, mods_ffn, gpost, ys, t):
    n = x3.shape[0]
    per_b = t // TB
    return pl.pallas_call(
        _combine_kernel,
        grid_spec=pltpu.PrefetchScalarGridSpec(
            num_scalar_prefetch=2,
            grid=(n // TB,),
            in_specs=[pl.BlockSpec((TB, LANES), lambda b, o, c: (b, 0)),
                      pl.BlockSpec((TB, D), lambda b, o, c: (b, 0)),
                      pl.BlockSpec((1, 1, 1, 3 * D), lambda b, o, c: (1, b // per_b, 0, 0)),
                      pl.BlockSpec((1, D), lambda b, o, c: (0, 0)),
                      pl.BlockSpec(memory_space=pl.ANY)],
            out_specs=pl.BlockSpec((TB, D), lambda b, o, c: (b, 0)),
            scratch_shapes=_COMBINE_SCRATCH),
        out_shape=jax.ShapeDtypeStruct((n, D), F32),
        compiler_params=_cparams(("arbitrary",), 40),
        name="moe_combine",
    )(off, cnt, gates_c, x3, mods_ffn, gpost, ys)


def _combine_rows_kernel(off_ref, cnt_ref, gc_ref, ys_ref, out_ref, ybuf_ref, acc_ref, sem, *, block0):
    _gather_expert_rows(block0, off_ref, cnt_ref, gc_ref, ys_ref, ybuf_ref, acc_ref, sem)
    out_ref[...] = acc_ref[...]


def _combine_rows(off, cnt, gates_c, ys, block0):
    return pl.pallas_call(
        functools.partial(_combine_rows_kernel, block0=block0),
        grid_spec=pltpu.PrefetchScalarGridSpec(
            num_scalar_prefetch=2,
            grid=(1,),
            in_specs=[pl.BlockSpec((TB, LANES), lambda b, o, c: (0, 0)),
                      pl.BlockSpec(memory_space=pl.ANY)],
            out_specs=pl.BlockSpec((TB, D), lambda b, o, c: (0, 0)),
            scratch_shapes=_COMBINE_SCRATCH),
        out_shape=jax.ShapeDtypeStruct((TB, D), F32),
        compiler_params=_cparams(("arbitrary",), 40),
        name="moe_combine_rows",
    )(off, cnt, gates_c, ys)


def _s_mix0_kernel(x_ref, st_ref, modm_ref, modf_ref, gpre_ref, gpost_ref, gpref_ref, pw_ref, ps_ref,
                   x1_ref, pool_ref, hf_ref, *, pos):
    sh, sc, gt = _split3(modm_ref[...])
    x = x_ref[...]
    hm = _rms_scale(x) * gpre_ref[...] * (1.0 + sc) + sh
    parts = []
    for g, w in enumerate(POOL_WINDOWS):
        lo = g * PG
        tok = hm[:, lo:lo + PG]
        s = tok
        for j in range(1, w):
            s = s + st_ref[POOL_STATE - j, :, lo:lo + PG]
        cnt = float(min(w, pos + 1))
        parts.append(_dot_hi(s / cnt - tok, pw_ref[g]))
    mix = jnp.concatenate(parts, axis=1) * ps_ref[...]
    x1 = x + gt * (_rms_scale(mix) * gpost_ref[...])
    x1_ref[...] = x1
    for j in range(POOL_STATE - 1):
        pool_ref[j] = st_ref[j + 1]
    pool_ref[POOL_STATE - 1] = hm
    shf, scf, _ = _split3(modf_ref[...])
    hf_ref[...] = _rms_scale(x1) * gpref_ref[...] * (1.0 + scf) + shf


def _s_mix0(x_s, state_t, modm, modf, gpre, gpost, gpref, pw, ps, pos):
    rows = x_s.shape[0]
    return pl.pallas_call(
        functools.partial(_s_mix0_kernel, pos=pos),
        out_shape=[jax.ShapeDtypeStruct((rows, D), F32),
                   jax.ShapeDtypeStruct((POOL_STATE, rows, D), F32),
                   jax.ShapeDtypeStruct((rows, D), F32)],
        compiler_params=pltpu.CompilerParams(vmem_limit_bytes=40 * MIB),
        name="s_mix0",
    )(x_s, state_t, modm, modf, gpre, gpost, gpref, pw, ps)


def _s_ffn_kernel(h_ref, w1_ref, w3_ref, w2_ref, f_ref):
    j = pl.program_id(0)
    h = h_ref[...]
    contrib = _dot_hi(_silu(_dot_hi(h, w1_ref[...])) * _dot_hi(h, w3_ref[...]), w2_ref[...])

    @pl.when(j == 0)
    def _():
        f_ref[...] = contrib

    @pl.when(j > 0)
    def _():
        f_ref[...] += contrib


def _s_ffn(h, w1, w3, w2):
    rows = h.shape[0]
    dff = w1.shape[1]
    fc = 256
    return pl.pallas_call(
        _s_ffn_kernel,
        grid=(dff // fc,),
        in_specs=[pl.BlockSpec((rows, D), lambda j: (0, 0)),
                  pl.BlockSpec((D, fc), lambda j: (0, j)),
                  pl.BlockSpec((D, fc), lambda j: (0, j)),
                  pl.BlockSpec((fc, D), lambda j: (j, 0))],
        out_specs=pl.BlockSpec((rows, D), lambda j: (0, 0)),
        out_shape=jax.ShapeDtypeStruct((rows, D), F32),
        compiler_params=_cparams(("arbitrary",), 32),
        name="s_ffn0",
    )(h, w1, w3, w2)


def _s_post0_kernel(x1_ref, f_ref, modf_ref, modm_ref, gpost_ref, gkv_ref, gpre1_ref,
                    x2_ref, kvin_ref, hm1_ref):
    _, _, gt = _split3(modf_ref[...])
    x2 = x1_ref[...] + gt * (_rms_scale(f_ref[...]) * gpost_ref[...])
    x2_ref[...] = x2
    xn = _rms_scale(x2)
    kvin_ref[...] = xn * gkv_ref[...]
    sh1, sc1, _ = _split3(modm_ref[...])
    hm1_ref[...] = xn * gpre1_ref[...] * (1.0 + sc1) + sh1


def _s_post0(x1, f, modf, modm, gpost, gkv, gpre1):
    rows = x1.shape[0]
    return pl.pallas_call(
        _s_post0_kernel,
        out_shape=[jax.ShapeDtypeStruct((rows, D), F32)] * 3,
        name="s_post0",
    )(x1, f, modf, modm, gpost, gkv, gpre1)


def _s_linear_kernel(h_ref, w_ref, cos_ref, sin_ref, o_ref):
    o_ref[...] = _rope_cols(_dot_hi(h_ref[...], w_ref[...]), cos_ref[...], sin_ref[...])


def _s_linear_plain_kernel(h_ref, w_ref, o_ref):
    o_ref[...] = _dot_hi(h_ref[...], w_ref[...])


def _s_linear(h, w, rope=None):
    rows, k = h.shape
    nw = w.shape[1]
    cw = 512
    in_specs = [pl.BlockSpec((rows, k), lambda j: (0, 0)), pl.BlockSpec((k, cw), lambda j: (0, j))]
    args = [h, w]
    body = _s_linear_plain_kernel
    if rope is not None:
        body = _s_linear_kernel
        in_specs += [pl.BlockSpec((1, cw), lambda j: (0, j)), pl.BlockSpec((1, cw), lambda j: (0, j))]
        args += [rope[0], rope[1]]
    return pl.pallas_call(
        body,
        grid=(nw // cw,),
        in_specs=in_specs,
        out_specs=pl.BlockSpec((rows, cw), lambda j: (0, j)),
        out_shape=jax.ShapeDtypeStruct((rows, nw), F32),
        compiler_params=_cparams(("arbitrary",), 32),
        name="s_linear",
    )(*args)


def _s_attn_kernel(qkv_ref, kc_ref, vc_ref, o_ref):
    n_past = kc_ref.shape[3]
    ng = len(ATTN_GROUPS)
    nt = (((1,), (1,)), ((), ()))
    eye = (lax.broadcasted_iota(I32, (LANES, LANES), 0) == lax.broadcasted_iota(I32, (LANES, LANES), 1)).astype(F32)
    lane = lax.broadcasted_iota(I32, (1, LANES), 1)
    out_cols = jnp.zeros((HD, LANES), F32)
    for h in range(kc_ref.shape[1]):
        rows = jnp.concatenate([qkv_ref[0, h], jnp.zeros((LANES - 8, HD), F32)], axis=0)
        cols = lax.dot_general(eye[:HD, :HD], rows, nt, precision=HIGHEST, preferred_element_type=F32)
        k_new, v_new = cols[:, ng:ng + 1], cols[:, ng + 1:ng + 2]
        outs, lses = [], []
        for g, (win, dil) in enumerate(ATTN_GROUPS):
            qg = cols[:, g:g + 1]
            lo = n_past - win
            dist = win - lax.broadcasted_iota(I32, (1, win), 1)
            s_c = jnp.sum(kc_ref[0, h, :, lo:] * qg, axis=0, keepdims=True) * SCALE
            s_c = jnp.where((dist & (dil - 1)) == 0, s_c, -jnp.inf)
            s_n = jnp.sum(k_new * qg, axis=0, keepdims=True) * SCALE
            m = jnp.maximum(jnp.max(s_c, axis=1, keepdims=True), s_n)
            p_c = jnp.exp(s_c - m)
            p_n = jnp.exp(s_n - m)
            den = jnp.sum(p_c, axis=1, keepdims=True) + p_n
            outs.append((jnp.sum(vc_ref[0, h, :, lo:] * p_c, axis=1, keepdims=True) + p_n * v_new) / den)
            lses.append(m + jnp.log(den))
        mx = jnp.maximum(jnp.maximum(lses[0], lses[1]), lses[2])
        es = [jnp.exp(l - mx) for l in lses]
        tot = es[0] + es[1] + es[2]
        out = (es[0] / tot) * outs[0]
        for g in range(1, ng):
            out = out + (es[g] / tot) * outs[g]
        out_cols = out_cols + out * (lane == h).astype(F32)
    o_ref[0] = lax.dot_general(eye[:kc_ref.shape[1]], out_cols, nt, precision=HIGHEST, preferred_element_type=F32)


def _s_attn(qkv, cache_kt, cache_vt):
    rows, _, _, n_past = cache_kt.shape
    for win, dil in ATTN_GROUPS:
        assert n_past >= win and win % LANES == 0 and dil & (dil - 1) == 0
    cache = pl.BlockSpec((1, HB_S, HD, n_past), lambda b, h: (b, h, 0, 0))
    return pl.pallas_call(
        _s_attn_kernel,
        grid=(rows, NH // HB_S),
        in_specs=[pl.BlockSpec((1, HB_S, 8, HD), lambda b, h: (b, h, 0, 0)), cache, cache],
        out_specs=pl.BlockSpec((1, HB_S, HD), lambda b, h: (b, h, 0)),
        out_shape=jax.ShapeDtypeStruct((rows, NH, HD), F32),
        compiler_params=_cparams(("arbitrary", "arbitrary"), 40),
        name="s_attn",
    )(qkv, cache_kt, cache_vt)


def _s_post1_kernel(x2_ref, mix_ref, modm_ref, modf_ref, gpost_ref, gpre_ref, rwt_ref, rb_ref,
                    x3_ref, hf_ref, gc_ref):
    _, _, gt = _split3(modm_ref[...])
    x3 = x2_ref[...] + gt * (_rms_scale(mix_ref[...]) * gpost_ref[...])
    x3_ref[...] = x3
    sh, sc, _ = _split3(modf_ref[...])
    hf = _rms_scale(x3) * gpre_ref[...] * (1.0 + sc) + sh
    hf_ref[...] = hf
    logits_t = lax.dot_general(rwt_ref[...], hf, (((1,), (1,)), ((), ())),
                               precision=HIGHEST, preferred_element_type=F32) + rb_ref[...]
    gc_ref[...] = _router_top2(logits_t)


def _s_post1(x2, mix, modm, modf, gpost, gpre, rwt, rb):
    rows = x2.shape[0]
    return pl.pallas_call(
        _s_post1_kernel,
        out_shape=[jax.ShapeDtypeStruct((rows, D), F32), jax.ShapeDtypeStruct((rows, D), F32),
                   jax.ShapeDtypeStruct((N_EXPERTS, rows), F32)],
        name="s_post1",
    )(x2, mix, modm, modf, gpost, gpre, rwt, rb)


def _s_post2_kernel(x3_ref, f_ref, modf_ref, gpost_ref, y_ref):
    _, _, gt = _split3(modf_ref[...])
    y_ref[...] = x3_ref[...] + gt * (_rms_scale(f_ref[...]) * gpost_ref[...])


def _s_post2(x3, f, modf, gpost):
    return pl.pallas_call(
        _s_post2_kernel,
        out_shape=jax.ShapeDtypeStruct(x3.shape, F32),
        name="s_post2",
    )(x3, f, modf, gpost)


def _rope_tables(pos):
    half = HD // 2
    inv = ROPE_THETA ** (-jnp.arange(half, dtype=F32) / half)
    ang = pos.astype(F32)[:, None] * inv[None, :]
    cos, sin = jnp.cos(ang), jnp.sin(ang)
    cos = jnp.tile(cos, (1, LANES // half))
    sin = jnp.tile(jnp.concatenate([-sin, sin], axis=1), (1, LANES // HD))
    return cos, sin


def kernel(x_prompt, x_sample, state_pool, cache_k, cache_v, c_prompt, c_sample, ada_mix_w, ada_mix_b, g_pre_mix, g_post_mix, pool_w, pool_scale, g_kv, w_kv, w_q, w_o, ada_ffn_w, ada_ffn_b, g_pre_ffn, g_post_ffn, ffn_w1, ffn_w3, ffn_w2, router_w, router_b, moe_w1, moe_w3, moe_w2):
    b, t, _ = x_prompt.shape
    nsmp = x_sample.shape[0]
    n = b * t
    past_len = 16384
    assert x_sample.shape[1] == 1 and g_pre_mix.shape[0] == 2 and t % (ATTN_GROUPS[-1][1] * BLK) == 0
    vec = lambda a: a.reshape(1, D)

    c_all = jnp.concatenate([c_prompt, c_sample], axis=0)
    mods_mix = _ada(c_all, ada_mix_w, ada_mix_b)
    mods_ffn = _ada(c_all, ada_ffn_w, ada_ffn_b)
    pm_mix = mods_mix[:, :b].reshape(2, b, 1, 3 * D)
    pm_ffn = mods_ffn[:, :b].reshape(2, b, 1, 3 * D)
    sm_mix = mods_mix[:, b:]
    sm_ffn = mods_ffn[:, b:]

    rwt = router_w[0].T
    rb = router_b[0].reshape(N_EXPERTS, 1)

    x1, pool16 = _mix0(x_prompt, pm_mix, vec(g_pre_mix[0]), vec(g_post_mix[0]),
                       pool_w[0].astype(BF16), vec(pool_scale[0]))
    cos_p, sin_p = _rope_tables(jnp.arange(t, dtype=jnp.int32))
    x2 = _ffn0(x1.reshape(n, D), pm_ffn, vec(g_pre_ffn[0]), vec(g_post_ffn[0]),
               ffn_w1[0].astype(BF16), ffn_w3[0].astype(BF16), ffn_w2[0].astype(BF16), t)
    kt_p, vt_p, *kv_l = _kvproj(x2, vec(g_kv), cos_p, sin_p, w_kv.astype(BF16), b, t)
    q_l = _qproj(x2, pm_mix, vec(g_pre_mix[1]), cos_p, sin_p, w_q[0].astype(BF16), b, t)
    attn = _attention(q_l, kv_l[:3], kv_l[3:], b, t)
    x3, hf1, gates_t, gates_c = _post1(attn, x2, pm_mix, pm_ffn, vec(g_post_mix[1]),
                                       vec(g_pre_ffn[1]), w_o[0].astype(BF16), rwt, rb, t)

    state_t = jnp.transpose(state_pool[:, 0], (1, 0, 2))
    x1s, pool_t, hf0s = _s_mix0(x_sample.reshape(nsmp, D), state_t, sm_mix[0], sm_ffn[0],
                                vec(g_pre_mix[0]), vec(g_post_mix[0]), vec(g_pre_ffn[0]),
                                pool_w[0], vec(pool_scale[0]), past_len)
    f0s = _s_ffn(hf0s, ffn_w1[0], ffn_w3[0], ffn_w2[0])
    x2s, kvin_s, hm1s = _s_post0(x1s, f0s, sm_ffn[0], sm_mix[1], vec(g_post_ffn[0]), vec(g_kv), vec(g_pre_mix[1]))
    cos_s, sin_s = _rope_tables(jnp.full((1,), past_len, jnp.int32))
    ones, zeros = jnp.ones((1, D), F32), jnp.zeros((1, D), F32)
    kv_s = _s_linear(kvin_s, w_kv, (jnp.concatenate([jnp.tile(cos_s, (1, D // LANES)), ones], axis=1),
                                    jnp.concatenate([jnp.tile(sin_s, (1, D // LANES)), zeros], axis=1)))
    q_s = _s_linear(hm1s, w_q[0], (jnp.tile(cos_s, (1, 3 * D // LANES)), jnp.tile(sin_s, (1, 3 * D // LANES))))
    k_s = kv_s[:, :D].reshape(nsmp, NH, HD)
    v_s = kv_s[:, D:].reshape(nsmp, NH, HD)
    cache_kt = jnp.transpose(cache_k, (0, 2, 3, 1))
    cache_vt = jnp.transpose(cache_v, (0, 2, 3, 1))
    qkv = jnp.concatenate([jnp.transpose(q_s.reshape(nsmp, len(ATTN_GROUPS), NH, HD), (0, 2, 1, 3)),
                           k_s[:, :, None], v_s[:, :, None]], axis=2)
    qkv = jnp.pad(qkv, ((0, 0), (0, 0), (0, 8 - qkv.shape[2]), (0, 0)))
    attn_s = _s_attn(qkv, cache_kt, cache_vt).reshape(nsmp, D)
    mix1s = _s_linear(attn_s, w_o[0])
    x3s, hf1s, gts = _s_post1(x2s, mix1s, sm_mix[1], sm_ffn[1], vec(g_post_mix[1]), vec(g_pre_ffn[1]), rwt, rb)

    hf_blk = jnp.zeros((TB, D), BF16).at[:nsmp].set(hf1s.astype(BF16))
    gt_blk = jnp.zeros((N_EXPERTS, TB), F32).at[:, :nsmp].set(gts)
    gc_blk = jnp.zeros((TB, LANES), F32).at[:nsmp, :N_EXPERTS].set(gts.T)
    off, cnt, tile_e, n_used = _route_plan(jnp.concatenate([gates_t, gt_blk], axis=1), n + TB)
    xs = jnp.zeros((_max_sorted_rows(n + TB), D), BF16)
    xs = _dispatch(off, cnt, hf1, gates_t, xs, 0)
    xs = _dispatch(off, cnt, hf_blk, gt_blk, xs, n // TB)
    ys = _experts(tile_e, n_used, xs, moe_w1[0], moe_w3[0], moe_w2[0])
    y_prompt = _combine(off, cnt, gates_c, x3, pm_ffn, vec(g_post_ffn[1]), ys, t).reshape(b, t, D)
    moe_s = _combine_rows(off, cnt, gc_blk, ys, n // TB)[:nsmp]
    y_sample = _s_post2(x3s, moe_s, sm_ffn[1], vec(g_post_ffn[1])).reshape(nsmp, 1, D)

    pool_prompt = pool16[:, None, 1:, :]
    pool_sample = jnp.transpose(pool_t, (1, 0, 2))[:, None]
    return (y_prompt, y_sample, pool_prompt, pool_sample,
            jnp.transpose(kt_p, (0, 3, 1, 2)), jnp.transpose(vt_p, (0, 3, 1, 2)),
            k_s[:, None], v_s[:, None])
```

```python
import functools

import jax
import jax.numpy as jnp
import numpy as np
from jax import lax
from jax.experimental import pallas as pl
from jax.experimental.pallas import tpu as pltpu

F32, BF16, I32 = jnp.float32, jnp.bfloat16, jnp.int32
HIGHEST = lax.Precision.HIGHEST

D = 1024
POOL_WINDOWS = (2, 4, 8, 16)
PG = D // len(POOL_WINDOWS)
POOL_STATE = max(POOL_WINDOWS) - 1
HALO = POOL_STATE + 1
ATTN_GROUPS = ((128, 1), (512, 4), (2048, 16))
NH = 16
HD = 64
BLK = 128
ROPE_THETA = 10000.0
N_EXPERTS = 8
EPS = 1e-6
SCALE = HD ** -0.5
LANES = 128
MIB = 1024 * 1024

TT_MIX = 512
TM_FFN = 1024
HB_S = 8
ATTN_UNITS = 4
FC_FFN = 256
TM_POST = 512
TB = 512
SUB = 128
NS = TB // SUB
TMX = 1024
FCX = 512
ROW_ALIGN = 16


def _cparams(sem, vmem_mib):
    return pltpu.CompilerParams(dimension_semantics=sem, vmem_limit_bytes=vmem_mib * MIB)


def _rms_scale(x):
    return x * lax.rsqrt(jnp.mean(x * x, axis=-1, keepdims=True) + EPS)


def _split3(m):
    return m[:, :D], m[:, D:2 * D], m[:, 2 * D:]


def _silu(x):
    return x * jax.nn.sigmoid(x)


def _rope_cols(x, cos, sin):
    lane = lax.broadcasted_iota(I32, (x.shape[0], LANES), 1)
    first = (lane % HD) < (HD // 2)
    shared = cos.shape[1] == LANES
    outs = []
    for c in range(x.shape[1] // LANES):
        cols = slice(c * LANES, (c + 1) * LANES)
        xc = x[:, cols]
        swapped = jnp.where(first, pltpu.roll(xc, LANES - HD // 2, 1), pltpu.roll(xc, HD // 2, 1))
        outs.append(xc * (cos if shared else cos[:, cols]) + swapped * (sin if shared else sin[:, cols]))
    return jnp.concatenate(outs, axis=1)


def _dot_hi(a, b):
    return jnp.dot(a, b, precision=HIGHEST, preferred_element_type=F32)


def _dot(a, b):
    return jnp.dot(a, b, preferred_element_type=F32)


def _ada_kernel(c_ref, w_ref, b_ref, o_ref):
    o_ref[0] = _dot_hi(_silu(c_ref[...]), w_ref[0]) + b_ref[0]


def _ada(c_all, w, b):
    nl, _, n3 = w.shape
    rows = c_all.shape[0]
    cw = 768
    return pl.pallas_call(
        _ada_kernel,
        grid=(nl, n3 // cw),
        in_specs=[pl.BlockSpec((rows, D), lambda l, j: (0, 0)),
                  pl.BlockSpec((1, D, cw), lambda l, j: (l, 0, j)),
                  pl.BlockSpec((1, 1, cw), lambda l, j: (l, 0, j))],
        out_specs=pl.BlockSpec((1, rows, cw), lambda l, j: (l, 0, j)),
        out_shape=jax.ShapeDtypeStruct((nl, rows, n3), F32),
        compiler_params=_cparams(("arbitrary", "arbitrary"), 32),
        name="ada_mod",
    )(c_all, w, b.reshape(nl, 1, n3))


def _mix0_kernel(x_ref, xh_ref, mod_ref, gpre_ref, gpost_ref, pw_ref, ps_ref,
                 x1_ref, pool_ref, ext_ref):
    i = pl.program_id(1)
    tt = x_ref.shape[1]
    sh, sc, gt = _split3(mod_ref[0, 0])
    gpre = gpre_ref[...]
    x = x_ref[0]
    hm = _rms_scale(x) * gpre * (1.0 + sc) + sh
    hh = _rms_scale(xh_ref[0]) * gpre * (1.0 + sc) + sh
    ext_ref[0:HALO, :] = jnp.where(i > 0, hh, 0.0)
    ext_ref[HALO:, :] = hm
    t = i * tt + lax.broadcasted_iota(I32, (tt, 1), 0)
    parts = []
    for g, w in enumerate(POOL_WINDOWS):
        lo = g * PG
        tok = hm[:, lo:lo + PG]
        s = tok
        for j in range(1, w):
            s = s + ext_ref[HALO - j:HALO - j + tt, lo:lo + PG]
        cnt = jnp.minimum(w, t + 1).astype(F32)
        parts.append(_dot((s / cnt - tok).astype(BF16), pw_ref[g]))
    mix = jnp.concatenate(parts, axis=1) * ps_ref[...]
    x1_ref[0] = x + gt * (_rms_scale(mix) * gpost_ref[...])

    @pl.when(i == pl.num_programs(1) - 1)
    def _():
        pool_ref[0] = ext_ref[tt:tt + HALO, :]


def _mix0(x, mods, gpre, gpost, pw_bf, ps):
    b, t, _ = x.shape
    tt = TT_MIX
    hb = tt // HALO
    return pl.pallas_call(
        _mix0_kernel,
        grid=(b, t // tt),
        in_specs=[pl.BlockSpec((1, tt, D), lambda bb, i: (bb, i, 0)),
                  pl.BlockSpec((1, HALO, D), lambda bb, i: (bb, jnp.maximum(i * hb - 1, 0), 0)),
                  pl.BlockSpec((1, 1, 1, 3 * D), lambda bb, i: (0, bb, 0, 0)),
                  pl.BlockSpec((1, D), lambda bb, i: (0, 0)),
                  pl.BlockSpec((1, D), lambda bb, i: (0, 0)),
                  pl.BlockSpec((len(POOL_WINDOWS), PG, PG), lambda bb, i: (0, 0, 0)),
                  pl.BlockSpec((1, D), lambda bb, i: (0, 0))],
        out_specs=[pl.BlockSpec((1, tt, D), lambda bb, i: (bb, i, 0)),
                   pl.BlockSpec((1, HALO, D), lambda bb, i: (bb, 0, 0))],
        out_shape=[jax.ShapeDtypeStruct((b, t, D), F32),
                   jax.ShapeDtypeStruct((b, HALO, D), F32)],
        scratch_shapes=[pltpu.VMEM((tt + HALO, D), F32)],
        compiler_params=_cparams(("arbitrary", "arbitrary"), 40),
        name="mix0",
    )(x, x, mods, gpre, gpost, pw_bf, ps)


def _store_dilated(val, plane_ref, dils, out_refs):
    tm = val.shape[0]
    for hp in range(NH // 2):
        plane_ref[hp] = val[:, hp * LANES:(hp + 1) * LANES]
    for dil, out_ref in zip(dils, out_refs):
        for hp in range(NH // 2):
            for r in range(dil):
                out_ref[0, hp, r] = plane_ref[hp, pl.ds(r, tm // dil, stride=dil), :].astype(BF16)


def _dilated_specs(b, t, tm):
    per_b = t // tm
    specs = [pl.BlockSpec((1, NH // 2, dil, tm // dil, LANES), lambda i: (i // per_b, 0, 0, i % per_b, 0))
             for _, dil in ATTN_GROUPS]
    shapes = [jax.ShapeDtypeStruct((b, NH // 2, dil, t // dil, LANES), BF16) for _, dil in ATTN_GROUPS]
    return specs, shapes


def _ffn0_kernel(x1_ref, modf_ref, gpre_ref, gpost_ref, w1_ref, w3_ref, w2_ref, x2_ref, acc_ref):
    sh, sc, gt = _split3(modf_ref[0, 0])
    x1 = x1_ref[...]
    hf = (_rms_scale(x1) * gpre_ref[...] * (1.0 + sc) + sh).astype(BF16)
    for c in range(w1_ref.shape[1] // FC_FFN):
        cs = slice(c * FC_FFN, (c + 1) * FC_FFN)
        a = (_silu(_dot(hf, w1_ref[:, cs])) * _dot(hf, w3_ref[:, cs])).astype(BF16)
        contrib = _dot(a, w2_ref[cs, :])
        if c == 0:
            acc_ref[...] = contrib
        else:
            acc_ref[...] += contrib
    x2 = x1 + gt * (_rms_scale(acc_ref[...]) * gpost_ref[...])
    x2_ref[...] = x2


def _ffn0(x1, mods_ffn, gpre, gpost, w1, w3, w2, t):
    n = x1.shape[0]
    tm = TM_FFN
    per_b = t // tm
    dff = w1.shape[1]
    row = lambda i: (i, 0)
    fixed = lambda i: (0, 0)
    resident = lambda shape: pl.BlockSpec(shape, fixed, pipeline_mode=pl.Buffered(1))
    vec = pl.BlockSpec((1, D), fixed)
    return pl.pallas_call(
        _ffn0_kernel,
        grid=(n // tm,),
        in_specs=[pl.BlockSpec((tm, D), row),
                  pl.BlockSpec((1, 1, 1, 3 * D), lambda i: (0, i // per_b, 0, 0)),
                  vec, vec,
                  resident((D, dff)), resident((D, dff)), resident((dff, D))],
        out_specs=pl.BlockSpec((tm, D), row),
        out_shape=jax.ShapeDtypeStruct((n, D), F32),
        scratch_shapes=[pltpu.VMEM((tm, D), F32)],
        compiler_params=_cparams(("arbitrary",), 56),
        name="ffn0",
    )(x1, mods_ffn, gpre, gpost, w1, w3, w2)


def _kvproj_kernel(x2_ref, gkv_ref, cos_ref, sin_ref, wkv_ref,
                   kt_ref, vt_ref, k0_ref, k1_ref, k2_ref, v0_ref, v1_ref, v2_ref, plane_ref):
    tm = x2_ref.shape[0]
    kv = _dot((_rms_scale(x2_ref[...]) * gkv_ref[...]).astype(BF16), wkv_ref[...])
    k = _rope_cols(kv[:, :D], cos_ref[...], sin_ref[...])
    v = kv[:, D:]
    kt_ref[0] = k.T.reshape(NH, HD, tm)
    vt_ref[0] = v.T.reshape(NH, HD, tm)
    dils = [dil for _, dil in ATTN_GROUPS]
    _store_dilated(k, plane_ref, dils, (k0_ref, k1_ref, k2_ref))
    _store_dilated(v, plane_ref, dils, (v0_ref, v1_ref, v2_ref))


def _kvproj(x2, gkv, cos, sin, wkv, b, t):
    n = x2.shape[0]
    tm = TM_POST
    per_b = t // tm
    fixed = lambda i: (0, 0)
    dspecs, dshapes = _dilated_specs(b, t, tm)
    tspec = pl.BlockSpec((1, NH, HD, tm), lambda i: (i // per_b, 0, 0, i % per_b))
    tshape = jax.ShapeDtypeStruct((b, NH, HD, t), F32)
    return pl.pallas_call(
        _kvproj_kernel,
        grid=(n // tm,),
        in_specs=[pl.BlockSpec((tm, D), lambda i: (i, 0)),
                  pl.BlockSpec((1, D), fixed),
                  pl.BlockSpec((tm, LANES), lambda i: (i % per_b, 0)),
                  pl.BlockSpec((tm, LANES), lambda i: (i % per_b, 0)),
                  pl.BlockSpec((D, 2 * D), fixed, pipeline_mode=pl.Buffered(1))],
        out_specs=[tspec, tspec] + dspecs + dspecs,
        out_shape=[tshape, tshape] + dshapes + dshapes,
        scratch_shapes=[pltpu.VMEM((NH // 2, tm, LANES), F32)],
        compiler_params=_cparams(("arbitrary",), 48),
        name="kv_proj",
    )(x2, gkv, cos, sin, wkv)


def _qproj_kernel(x2_ref, modm_ref, gpre1_ref, cos_ref, sin_ref, wq_ref, q0_ref, q1_ref, q2_ref, scr_ref):
    sh1, sc1, _ = _split3(modm_ref[0, 0])
    hm1 = (_rms_scale(x2_ref[...]) * gpre1_ref[...] * (1.0 + sc1) + sh1).astype(BF16)
    q = _rope_cols(_dot(hm1, wq_ref[...]), cos_ref[...], sin_ref[...]) * SCALE
    for g, ((_, dil), q_ref) in enumerate(zip(ATTN_GROUPS, (q0_ref, q1_ref, q2_ref))):
        _store_dilated(q[:, g * D:(g + 1) * D], scr_ref, (dil,), (q_ref,))


def _qproj(x2, mods_mix, gpre1, cos, sin, wq, b, t):
    n = x2.shape[0]
    tm = TM_POST
    per_b = t // tm
    fixed = lambda i: (0, 0)
    dspecs, dshapes = _dilated_specs(b, t, tm)
    return pl.pallas_call(
        _qproj_kernel,
        grid=(n // tm,),
        in_specs=[pl.BlockSpec((tm, D), lambda i: (i, 0)),
                  pl.BlockSpec((1, 1, 1, 3 * D), lambda i: (1, i // per_b, 0, 0)),
                  pl.BlockSpec((1, D), fixed),
                  pl.BlockSpec((tm, LANES), lambda i: (i % per_b, 0)),
                  pl.BlockSpec((tm, LANES), lambda i: (i % per_b, 0)),
                  pl.BlockSpec((D, 3 * D), fixed, pipeline_mode=pl.Buffered(1))],
        out_specs=dspecs,
        out_shape=dshapes,
        scratch_shapes=[pltpu.VMEM((NH // 2, tm, LANES), F32)],
        compiler_params=_cparams(("arbitrary",), 48),
        name="q_proj",
    )(x2, mods_mix, gpre1, cos, sin, wq)


def _attn_kernel(q0_ref, q1_ref, q2_ref, k0_ref, k1_ref, k2_ref, v0_ref, v1_ref, v2_ref,
                 o_ref, og_ref, lg_ref, bias_ref):
    t = o_ref.shape[2]
    first_head = lax.broadcasted_iota(I32, (BLK, LANES), 1) < HD
    nt = (((1,), (1,)), ((), ()))

    @pl.when((pl.program_id(0) == 0) & (pl.program_id(1) == 0))
    def _():
        qi = lax.broadcasted_iota(I32, (2 * BLK, 2 * BLK), 0) % BLK
        kj = lax.broadcasted_iota(I32, (2 * BLK, 2 * BLK), 1)
        for v in range(2):
            dist = v * BLK + qi - kj
            bias_ref[v] = jnp.where((dist >= 0) & (dist <= BLK), 0.0, -jnp.inf)

    groups = zip(ATTN_GROUPS, (q0_ref, q1_ref, q2_ref), (k0_ref, k1_ref, k2_ref), (v0_ref, v1_ref, v2_ref))
    for g, ((win, dil), q_ref, k_ref, v_ref) in enumerate(groups):
        assert win // dil == BLK
        sub = t // dil
        nb = sub // BLK
        kw = min(2 * BLK, sub)

        def scores(u, nb=nb, kw=kw, q_ref=q_ref, k_ref=k_ref):
            r = u // nb
            n = u % nb
            q_lo = pl.multiple_of(n * BLK, BLK)
            k_lo = pl.multiple_of(jnp.maximum(n - 1, 0) * BLK, BLK)
            qp = q_ref[0, 0, r, pl.ds(q_lo, BLK), :]
            zero = jnp.zeros_like(qp)
            qs = jnp.concatenate([jnp.where(first_head, qp, zero), jnp.where(first_head, zero, qp)], axis=0)
            s = lax.dot_general(qs, k_ref[0, 0, r, pl.ds(k_lo, kw), :], nt, preferred_element_type=F32)
            return s + bias_ref[jnp.minimum(n, 1), :, 0:kw], r, q_lo, k_lo

        def finish(s, r, q_lo, k_lo, dil=dil, kw=kw, v_ref=v_ref, g=g):
            m = jnp.max(s, axis=-1, keepdims=True)
            p = jnp.exp(s - m)
            den = jnp.sum(p, axis=-1, keepdims=True)
            o2 = _dot(p.astype(BF16), v_ref[0, 0, r, pl.ds(k_lo, kw), :]) * (1.0 / den)
            l2 = jnp.broadcast_to(m + jnp.log(den), (2 * BLK, LANES))
            rows = pl.ds(q_lo * dil + r, BLK, stride=dil) if dil > 1 else pl.ds(q_lo, BLK)
            og_ref[g, rows, :] = jnp.where(first_head, o2[:BLK], o2[BLK:])
            lg_ref[g, rows, :] = jnp.where(first_head, l2[:BLK], l2[BLK:])

        def trip(i, carry, scores=scores, finish=finish):
            started = [scores(i * ATTN_UNITS + j) for j in range(ATTN_UNITS)]
            for args in started:
                finish(*args)
            return carry

        lax.fori_loop(0, dil * nb // ATTN_UNITS, trip, 0)

    mt = 2 * BLK

    def merge(c, carry):
        rows = pl.ds(pl.multiple_of(c * mt, mt), mt)
        ls = [lg_ref[g, rows, :] for g in range(len(ATTN_GROUPS))]
        mx = jnp.maximum(jnp.maximum(ls[0], ls[1]), ls[2])
        es = [jnp.exp(l - mx) for l in ls]
        tot = es[0] + es[1] + es[2]
        out = (es[0] / tot) * og_ref[0, rows, :]
        for g in range(1, len(ATTN_GROUPS)):
            out = out + (es[g] / tot) * og_ref[g, rows, :]
        o_ref[0, 0, rows, :] = out.astype(BF16)
        return carry

    lax.fori_loop(0, t // mt, merge, 0)


def _attention(q_l, k_l, v_l, b, t):
    specs = [pl.BlockSpec((1, 1, dil, t // dil, LANES), lambda bb, hp: (bb, hp, 0, 0, 0))
             for _, dil in ATTN_GROUPS]
    return pl.pallas_call(
        _attn_kernel,
        grid=(b, NH // 2),
        in_specs=specs * 3,
        out_specs=pl.BlockSpec((1, 1, t, LANES), lambda bb, hp: (bb, hp, 0, 0)),
        out_shape=jax.ShapeDtypeStruct((b, NH // 2, t, LANES), BF16),
        scratch_shapes=[pltpu.VMEM((len(ATTN_GROUPS), t, LANES), F32),
                        pltpu.VMEM((len(ATTN_GROUPS), t, LANES), F32),
                        pltpu.VMEM((2, 2 * BLK, 2 * BLK), F32)],
        compiler_params=_cparams(("arbitrary", "arbitrary"), 40),
        name="dilated_attn",
    )(*q_l, *k_l, *v_l)


def _router_top2(logits_t):
    idx = lax.broadcasted_iota(I32, logits_t.shape, 0)
    v1 = jnp.max(logits_t, axis=0, keepdims=True)
    i1 = jnp.min(jnp.where(logits_t == v1, idx, N_EXPERTS), axis=0, keepdims=True)
    m1 = idx == i1
    rest = jnp.where(m1, -jnp.inf, logits_t)
    v2 = jnp.max(rest, axis=0, keepdims=True)
    i2 = jnp.min(jnp.where(rest == v2, idx, N_EXPERTS), axis=0, keepdims=True)
    m2 = idx == i2
    e2 = jnp.exp(v2 - v1)
    den = 1.0 + e2
    return jnp.where(m1, 1.0 / den, 0.0) + jnp.where(m2, e2 / den, 0.0)


def _post1_kernel(a_ref, x2_ref, modm_ref, modf_ref, gpost_ref, gpre_ref, wo_ref, rwt_ref, rb_ref,
                  x3_ref, hf_ref, gt_ref, gc_ref):
    o = jnp.concatenate([a_ref[0, hp] for hp in range(NH // 2)], axis=1)
    mix = _dot(o, wo_ref[...])
    _, _, gt = _split3(modm_ref[0, 0])
    x3 = x2_ref[...] + gt * (_rms_scale(mix) * gpost_ref[...])
    x3_ref[...] = x3
    sh, sc, _ = _split3(modf_ref[0, 0])
    hf = _rms_scale(x3) * gpre_ref[...] * (1.0 + sc) + sh
    hf_ref[...] = hf.astype(BF16)
    logits_t = lax.dot_general(rwt_ref[...], hf, (((1,), (1,)), ((), ())),
                               precision=HIGHEST, preferred_element_type=F32) + rb_ref[...]
    gates_t = _router_top2(logits_t)
    gt_ref[...] = gates_t
    padded = jnp.concatenate([gates_t, jnp.zeros((LANES - N_EXPERTS, gates_t.shape[1]), F32)], axis=0)
    gc_ref[...] = padded.T


def _post1(attn, x2, mods_mix, mods_ffn, gpost, gpre, wo_bf, rwt, rb, t):
    n = x2.shape[0]
    tm = TM_POST
    per_b = t // tm
    row = lambda i: (i, 0)
    fixed = lambda i: (0, 0)
    vec = pl.BlockSpec((1, D), fixed)
    tile = pl.BlockSpec((tm, D), row)
    return pl.pallas_call(
        _post1_kernel,
        grid=(n // tm,),
        in_specs=[pl.BlockSpec((1, NH // 2, tm, LANES), lambda i: (i // per_b, 0, i % per_b, 0)),
                  tile,
                  pl.BlockSpec((1, 1, 1, 3 * D), lambda i: (1, i // per_b, 0, 0)),
                  pl.BlockSpec((1, 1, 1, 3 * D), lambda i: (1, i // per_b, 0, 0)),
                  vec, vec,
                  pl.BlockSpec((D, D), fixed),
                  pl.BlockSpec((N_EXPERTS, D), fixed),
                  pl.BlockSpec((N_EXPERTS, 1), fixed)],
        out_specs=[tile, tile, pl.BlockSpec((N_EXPERTS, tm), lambda i: (0, i)),
                   pl.BlockSpec((tm, LANES), row)],
        out_shape=[jax.ShapeDtypeStruct((n, D), F32), jax.ShapeDtypeStruct((n, D), BF16),
                   jax.ShapeDtypeStruct((N_EXPERTS, n), F32), jax.ShapeDtypeStruct((n, LANES), F32)],
        compiler_params=_cparams(("arbitrary",), 40),
        name="wo_router",
    )(attn, x2, mods_mix, mods_ffn, gpost, gpre, wo_bf, rwt, rb)


def _route_plan(gates_t, n):
    nb = n // TB
    cnt = jnp.sum((gates_t > 0.0).reshape(N_EXPERTS, nb, TB), axis=-1).astype(I32)
    c_al = (cnt + ROW_ALIGN - 1) // ROW_ALIGN * ROW_ALIGN
    within = jnp.cumsum(c_al, axis=1) - c_al
    tot = jnp.sum(c_al, axis=1)
    region = (tot + SUB + TMX - 1) // TMX * TMX
    start = jnp.cumsum(region) - region
    off = (start[:, None] + within).T.reshape(-1)
    n_used = (jnp.sum(region) // TMX).astype(I32)
    tile_row = jnp.arange(_max_sorted_rows(n) // TMX, dtype=I32) * TMX
    tile_e = jnp.sum(tile_row[:, None] >= (start + region)[None, :], axis=1).astype(I32)
    tile_e = jnp.minimum(tile_e, N_EXPERTS - 1)
    last_e = tile_e[jnp.maximum(n_used - 1, 0)]
    tile_e = jnp.where(jnp.arange(tile_e.shape[0]) < n_used, tile_e, last_e)
    return off.astype(I32), cnt.T.reshape(-1), tile_e, n_used.reshape(1)


def _max_sorted_rows(n):
    nb = n // TB
    worst = 2 * n + nb * N_EXPERTS * (ROW_ALIGN - 1) + N_EXPERTS * (SUB + TMX - 1)
    return (worst + TMX - 1) // TMX * TMX


def _dispatch_kernel(off_ref, cnt_ref, hf_ref, gt_ref, xs_in_ref, xs_ref, stage_ref, sem, *, block0):
    del xs_in_ref
    b = pl.program_id(0) + block0
    hf = hf_ref[...]
    sel = gt_ref[...] > 0.0
    before = lax.broadcasted_iota(I32, (TB, TB), 0) < lax.broadcasted_iota(I32, (TB, TB), 1)
    rank = _dot(sel.astype(F32), before.astype(F32)).astype(I32)
    jj = lax.broadcasted_iota(I32, (SUB, TB), 0)

    def copy(e, s):
        dst = pl.multiple_of(off_ref[b * N_EXPERTS + e] + s * SUB, ROW_ALIGN)
        return pltpu.make_async_copy(stage_ref.at[e * NS + s], xs_ref.at[pl.ds(dst, SUB)], sem.at[e * NS + s])

    for e in range(N_EXPERTS):
        c = cnt_ref[b * N_EXPERTS + e]
        for s in range(NS):
            @pl.when(s * SUB < c)
            def _(e=e, s=s):
                pick = ((rank[e:e + 1, :] == jj + s * SUB) & sel[e:e + 1, :]).astype(BF16)
                stage_ref[e * NS + s] = _dot(pick, hf).astype(BF16)
                copy(e, s).start()
    for e in range(N_EXPERTS):
        c = cnt_ref[b * N_EXPERTS + e]
        for s in range(NS):
            @pl.when(s * SUB < c)
            def _(e=e, s=s):
                copy(e, s).wait()


def _dispatch(off, cnt, hf_bf, gates_t, xs, block0):
    n = hf_bf.shape[0]
    rows = xs.shape[0]
    return pl.pallas_call(
        functools.partial(_dispatch_kernel, block0=block0),
        grid_spec=pltpu.PrefetchScalarGridSpec(
            num_scalar_prefetch=2,
            grid=(n // TB,),
            in_specs=[pl.BlockSpec((TB, D), lambda b, o, c: (b, 0)),
                      pl.BlockSpec((N_EXPERTS, TB), lambda b, o, c: (0, b)),
                      pl.BlockSpec(memory_space=pl.ANY)],
            out_specs=pl.BlockSpec(memory_space=pl.ANY),
            scratch_shapes=[pltpu.VMEM((N_EXPERTS * NS, SUB, D), BF16),
                            pltpu.SemaphoreType.DMA((N_EXPERTS * NS,))]),
        out_shape=jax.ShapeDtypeStruct((rows, D), BF16),
        input_output_aliases={4: 0},
        compiler_params=_cparams(("arbitrary",), 40),
        name="moe_dispatch",
    )(off, cnt, hf_bf, gates_t, xs)


def _expert_kernel(te_ref, nu_ref, x_ref, w1_ref, w3_ref, w2_ref, y_ref, acc_ref):
    del te_ref
    i, j = pl.program_id(0), pl.program_id(1)

    @pl.when(i < nu_ref[0])
    def _():
        x = x_ref[...]
        a = (_silu(_dot(x, w1_ref[0].astype(BF16))) * _dot(x, w3_ref[0].astype(BF16))).astype(BF16)
        contrib = _dot(a, w2_ref[0].astype(BF16))

        @pl.when(j == 0)
        def _():
            acc_ref[...] = contrib

        @pl.when(j > 0)
        def _():
            acc_ref[...] += contrib

        @pl.when(j == pl.num_programs(1) - 1)
        def _():
            y_ref[...] = acc_ref[...].astype(BF16)

    @pl.when((i >= nu_ref[0]) & (j == 0))
    def _():
        y_ref[...] = jnp.zeros(y_ref.shape, BF16)


def _experts(tile_e, n_used, xs, w1, w3, w2):
    rows = xs.shape[0]
    dff = w1.shape[2]
    nj = dff // FCX

    def tile_idx(i, j, te, nu):
        return (jnp.minimum(i, nu[0] - 1), 0)

    def chunk(i, j, nu):
        return jnp.where(i < nu[0], j, nj - 1)

    return pl.pallas_call(
        _expert_kernel,
        grid_spec=pltpu.PrefetchScalarGridSpec(
            num_scalar_prefetch=2,
            grid=(rows // TMX, nj),
            in_specs=[pl.BlockSpec((TMX, D), tile_idx),
                      pl.BlockSpec((1, D, FCX), lambda i, j, te, nu: (te[i], 0, chunk(i, j, nu))),
                      pl.BlockSpec((1, D, FCX), lambda i, j, te, nu: (te[i], 0, chunk(i, j, nu))),
                      pl.BlockSpec((1, FCX, D), lambda i, j, te, nu: (te[i], chunk(i, j, nu), 0))],
            out_specs=pl.BlockSpec((TMX, D), lambda i, j, te, nu: (i, 0)),
            scratch_shapes=[pltpu.VMEM((TMX, D), F32)]),
        out_shape=jax.ShapeDtypeStruct((rows, D), BF16),
        compiler_params=_cparams(("arbitrary", "arbitrary"), 56),
        name="moe_experts",
    )(tile_e, n_used, xs, w1, w3, w2)


SLOTS = 2 * TB // SUB + N_EXPERTS


def _gather_expert_rows(b, first_step, off_ref, cnt_ref, gt_ref, gc_ref, ys_ref, ybuf_ref, p_ref, sem):
    @pl.when(first_step)
    def _():
        ybuf_ref[...] = jnp.zeros(ybuf_ref.shape, BF16)

    def copy(e, s, slot):
        src = pl.multiple_of(off_ref[b * N_EXPERTS + e] + s * SUB, ROW_ALIGN)
        dst = pl.multiple_of(slot * SUB, SUB)
        return pltpu.make_async_copy(ys_ref.at[pl.ds(src, SUB)], ybuf_ref.at[pl.ds(dst, SUB)], sem.at[slot])

    pairs = [(e, s) for e in range(N_EXPERTS) for s in range(NS)]
    used, slots = [], []
    slot = jnp.int32(0)
    for e, s in pairs:
        u = s * SUB < cnt_ref[b * N_EXPERTS + e]
        used.append(u)
        slots.append(slot)
        slot = slot + u.astype(I32)
    for (e, s), u, k in zip(pairs, used, slots):
        @pl.when(u)
        def _(e=e, s=s, k=k):
            copy(e, s, k).start()

    sel = gt_ref[...] > 0.0
    idx = lax.broadcasted_iota(I32, sel.shape, 0)
    lower = idx == jnp.min(jnp.where(sel, idx, N_EXPERTS), axis=0, keepdims=True)
    before = lax.broadcasted_iota(I32, (TB, TB), 0) < lax.broadcasted_iota(I32, (TB, TB), 1)
    rank = _dot(sel.astype(F32), before.astype(F32)).astype(I32)
    jj = lax.broadcasted_iota(I32, (SUB, TB), 0)
    p_ref[...] = jnp.zeros(p_ref.shape, BF16)
    for (e, s), u, k in zip(pairs, used, slots):
        @pl.when(u)
        def _(e=e, s=s, k=k):
            hit = (rank[e:e + 1, :] == jj + s * SUB) & sel[e:e + 1, :]
            rows = pl.ds(pl.multiple_of(k * SUB, SUB), SUB)
            p_ref[rows, 0:TB] = (hit & lower[e:e + 1, :]).astype(BF16)
            p_ref[rows, TB:2 * TB] = (hit & jnp.logical_not(lower[e:e + 1, :])).astype(BF16)
    for (e, s), u, k in zip(pairs, used, slots):
        @pl.when(u)
        def _(e=e, s=s, k=k):
            copy(e, s, k).wait()

    gate = gc_ref[...]
    lane = lax.broadcasted_iota(I32, gate.shape, 1)
    picked = gate > 0.0
    first = lane == jnp.min(jnp.where(picked, lane, LANES), axis=1, keepdims=True)
    g_lower = jnp.sum(jnp.where(first, gate, 0.0), axis=1, keepdims=True)
    g_upper = jnp.sum(jnp.where(first, 0.0, gate), axis=1, keepdims=True)
    tn = (((0,), (0,)), ((), ()))
    z = lax.dot_general(p_ref[...], ybuf_ref[...], tn, preferred_element_type=F32)
    return g_lower * z[:TB] + g_upper * z[TB:]


_COMBINE_SCRATCH = [pltpu.VMEM((SLOTS * SUB, D), BF16),
                    pltpu.VMEM((SLOTS * SUB, 2 * TB), BF16),
                    pltpu.SemaphoreType.DMA((SLOTS,))]


def _combine_kernel(off_ref, cnt_ref, gt_ref, gc_ref, x3_ref, mod_ref, gpost_ref, ys_ref,
                    out_ref, ybuf_ref, p_ref, sem):
    b = pl.program_id(0)
    moe = _gather_expert_rows(b, b == 0, off_ref, cnt_ref, gt_ref, gc_ref, ys_ref, ybuf_ref, p_ref, sem)
    _, _, gt = _split3(mod_ref[0, 0])
    out_ref[...] = x3_ref[...] + gt * (_rms_scale(moe) * gpost_ref[...])


def _combine(off, cnt, gates_t, gates_c, x3, mods_ffn, gpost, ys, t):
    n = x3.shape[0]
    per_b = t // TB
    return pl.pallas_call(
        _combine_kernel,
        grid_spec=pltpu.PrefetchScalarGridSpec(
            num_scalar_prefetch=2,
            grid=(n // TB,),
            in_specs=[pl.BlockSpec((N_EXPERTS, TB), lambda b, o, c: (0, b)),
                      pl.BlockSpec((TB, LANES), lambda b, o, c: (b, 0)),
                      pl.BlockSpec((TB, D), lambda b, o, c: (b, 0)),
                      pl.BlockSpec((1, 1, 1, 3 * D), lambda b, o, c: (1, b // per_b, 0, 0)),
                      pl.BlockSpec((1, D), lambda b, o, c: (0, 0)),
                      pl.BlockSpec(memory_space=pl.ANY)],
            out_specs=pl.BlockSpec((TB, D), lambda b, o, c: (b, 0)),
            scratch_shapes=_COMBINE_SCRATCH),
        out_shape=jax.ShapeDtypeStruct((n, D), F32),
        compiler_params=_cparams(("arbitrary",), 48),
        name="moe_combine",
    )(off, cnt, gates_t, gates_c, x3, mods_ffn, gpost, ys)


def _combine_rows_kernel(off_ref, cnt_ref, gt_ref, gc_ref, ys_ref, out_ref, ybuf_ref, p_ref, sem,
                         *, block0):
    out_ref[...] = _gather_expert_rows(block0, True, off_ref, cnt_ref, gt_ref, gc_ref, ys_ref,
                                       ybuf_ref, p_ref, sem)


def _combine_rows(off, cnt, gates_t, gates_c, ys, block0):
    return pl.pallas_call(
        functools.partial(_combine_rows_kernel, block0=block0),
        grid_spec=pltpu.PrefetchScalarGridSpec(
            num_scalar_prefetch=2,
            grid=(1,),
            in_specs=[pl.BlockSpec((N_EXPERTS, TB), lambda b, o, c: (0, 0)),
                      pl.BlockSpec((TB, LANES), lambda b, o, c: (0, 0)),
                      pl.BlockSpec(memory_space=pl.ANY)],
            out_specs=pl.BlockSpec((TB, D), lambda b, o, c: (0, 0)),
            scratch_shapes=_COMBINE_SCRATCH),
        out_shape=jax.ShapeDtypeStruct((TB, D), F32),
        compiler_params=_cparams(("arbitrary",), 48),
        name="moe_combine_rows",
    )(off, cnt, gates_t, gates_c, ys)


def _s_mix0_kernel(x_ref, st_ref, modm_ref, modf_ref, gpre_ref, gpost_ref, gpref_ref, pw_ref, ps_ref,
                   x1_ref, pool_ref, hf_ref, *, pos):
    sh, sc, gt = _split3(modm_ref[...])
    x = x_ref[...]
    hm = _rms_scale(x) * gpre_ref[...] * (1.0 + sc) + sh
    parts = []
    for g, w in enumerate(POOL_WINDOWS):
        lo = g * PG
        tok = hm[:, lo:lo + PG]
        s = tok
        for j in range(1, w):
            s = s + st_ref[POOL_STATE - j, :, lo:lo + PG]
        cnt = float(min(w, pos + 1))
        parts.append(_dot_hi(s / cnt - tok, pw_ref[g]))
    mix = jnp.concatenate(parts, axis=1) * ps_ref[...]
    x1 = x + gt * (_rms_scale(mix) * gpost_ref[...])
    x1_ref[...] = x1
    for j in range(POOL_STATE - 1):
        pool_ref[j] = st_ref[j + 1]
    pool_ref[POOL_STATE - 1] = hm
    shf, scf, _ = _split3(modf_ref[...])
    hf_ref[...] = _rms_scale(x1) * gpref_ref[...] * (1.0 + scf) + shf


def _s_mix0(x_s, state_t, modm, modf, gpre, gpost, gpref, pw, ps, pos):
    rows = x_s.shape[0]
    return pl.pallas_call(
        functools.partial(_s_mix0_kernel, pos=pos),
        out_shape=[jax.ShapeDtypeStruct((rows, D), F32),
                   jax.ShapeDtypeStruct((POOL_STATE, rows, D), F32),
                   jax.ShapeDtypeStruct((rows, D), F32)],
        compiler_params=pltpu.CompilerParams(vmem_limit_bytes=40 * MIB),
        name="s_mix0",
    )(x_s, state_t, modm, modf, gpre, gpost, gpref, pw, ps)


def _s_ffn_kernel(h_ref, w1_ref, w3_ref, w2_ref, f_ref):
    j = pl.program_id(0)
    h = h_ref[...]
    contrib = _dot_hi(_silu(_dot_hi(h, w1_ref[...])) * _dot_hi(h, w3_ref[...]), w2_ref[...])

    @pl.when(j == 0)
    def _():
        f_ref[...] = contrib

    @pl.when(j > 0)
    def _():
        f_ref[...] += contrib


def _s_ffn(h, w1, w3, w2):
    rows = h.shape[0]
    dff = w1.shape[1]
    fc = 256
    return pl.pallas_call(
        _s_ffn_kernel,
        grid=(dff // fc,),
        in_specs=[pl.BlockSpec((rows, D), lambda j: (0, 0)),
                  pl.BlockSpec((D, fc), lambda j: (0, j)),
                  pl.BlockSpec((D, fc), lambda j: (0, j)),
                  pl.BlockSpec((fc, D), lambda j: (j, 0))],
        out_specs=pl.BlockSpec((rows, D), lambda j: (0, 0)),
        out_shape=jax.ShapeDtypeStruct((rows, D), F32),
        compiler_params=_cparams(("arbitrary",), 32),
        name="s_ffn0",
    )(h, w1, w3, w2)


def _s_post0_kernel(x1_ref, f_ref, modf_ref, modm_ref, gpost_ref, gkv_ref, gpre1_ref,
                    x2_ref, kvin_ref, hm1_ref):
    _, _, gt = _split3(modf_ref[...])
    x2 = x1_ref[...] + gt * (_rms_scale(f_ref[...]) * gpost_ref[...])
    x2_ref[...] = x2
    xn = _rms_scale(x2)
    kvin_ref[...] = xn * gkv_ref[...]
    sh1, sc1, _ = _split3(modm_ref[...])
    hm1_ref[...] = xn * gpre1_ref[...] * (1.0 + sc1) + sh1


def _s_post0(x1, f, modf, modm, gpost, gkv, gpre1):
    rows = x1.shape[0]
    return pl.pallas_call(
        _s_post0_kernel,
        out_shape=[jax.ShapeDtypeStruct((rows, D), F32)] * 3,
        name="s_post0",
    )(x1, f, modf, modm, gpost, gkv, gpre1)


def _s_linear_kernel(h_ref, w_ref, cos_ref, sin_ref, o_ref):
    o_ref[...] = _rope_cols(_dot_hi(h_ref[...], w_ref[...]), cos_ref[...], sin_ref[...])


def _s_linear_plain_kernel(h_ref, w_ref, o_ref):
    o_ref[...] = _dot_hi(h_ref[...], w_ref[...])


def _s_linear(h, w, rope=None):
    rows, k = h.shape
    nw = w.shape[1]
    cw = 512
    in_specs = [pl.BlockSpec((rows, k), lambda j: (0, 0)), pl.BlockSpec((k, cw), lambda j: (0, j))]
    args = [h, w]
    body = _s_linear_plain_kernel
    if rope is not None:
        body = _s_linear_kernel
        in_specs += [pl.BlockSpec((1, cw), lambda j: (0, j)), pl.BlockSpec((1, cw), lambda j: (0, j))]
        args += [rope[0], rope[1]]
    return pl.pallas_call(
        body,
        grid=(nw // cw,),
        in_specs=in_specs,
        out_specs=pl.BlockSpec((rows, cw), lambda j: (0, j)),
        out_shape=jax.ShapeDtypeStruct((rows, nw), F32),
        compiler_params=_cparams(("arbitrary",), 32),
        name="s_linear",
    )(*args)


def _s_attn_kernel(qkv_ref, kc_ref, vc_ref, o_ref):
    n_past = kc_ref.shape[3]
    hb = kc_ref.shape[1]
    ng = len(ATTN_GROUPS)
    nt = (((1,), (1,)), ((), ()))
    eye = (lax.broadcasted_iota(I32, (LANES, LANES), 0) == lax.broadcasted_iota(I32, (LANES, LANES), 1)).astype(F32)
    tiles = jnp.concatenate([qkv_ref[0].reshape(hb * 8, HD), jnp.zeros((LANES - hb * 8, HD), F32)], axis=0)
    cols = lax.dot_general(eye[:HD, :HD], tiles, nt, precision=HIGHEST, preferred_element_type=F32)
    col = lambda h, j: cols[:, 8 * h + j:8 * h + j + 1]
    outs, lses = [], []
    for g, (win, dil) in enumerate(ATTN_GROUPS):
        lo = n_past - win
        dist = win - lax.broadcasted_iota(I32, (1, win), 1)
        s_c = jnp.concatenate([jnp.sum(kc_ref[0, h, :, lo:] * col(h, g), axis=0, keepdims=True)
                               for h in range(hb)], axis=0) * SCALE
        s_c = jnp.where((dist & (dil - 1)) == 0, s_c, -jnp.inf)
        s_n = jnp.concatenate([jnp.sum(col(h, ng) * col(h, g), axis=0, keepdims=True)
                               for h in range(hb)], axis=0) * SCALE
        m = jnp.maximum(jnp.max(s_c, axis=1, keepdims=True), s_n)
        p_c = jnp.exp(s_c - m)
        p_n = jnp.exp(s_n - m)
        den = jnp.sum(p_c, axis=1, keepdims=True) + p_n
        outs.append([(jnp.sum(vc_ref[0, h, :, lo:] * p_c[h:h + 1], axis=1, keepdims=True)
                      + p_n[h:h + 1] * col(h, ng + 1)) / den[h:h + 1] for h in range(hb)])
        lses.append(m + jnp.log(den))
    mx = jnp.maximum(jnp.maximum(lses[0], lses[1]), lses[2])
    es = [jnp.exp(l - mx) for l in lses]
    tot = es[0] + es[1] + es[2]
    lane = lax.broadcasted_iota(I32, (1, LANES), 1)
    out_cols = jnp.zeros((HD, LANES), F32)
    for h in range(hb):
        out = (es[0] / tot)[h:h + 1] * outs[0][h]
        for g in range(1, ng):
            out = out + (es[g] / tot)[h:h + 1] * outs[g][h]
        out_cols = out_cols + out * (lane == h).astype(F32)
    o_ref[0] = lax.dot_general(eye[:hb], out_cols, nt, precision=HIGHEST, preferred_element_type=F32)


def _s_attn(qkv, cache_kt, cache_vt):
    rows, _, _, n_past = cache_kt.shape
    for win, dil in ATTN_GROUPS:
        assert n_past >= win and win % LANES == 0 and dil & (dil - 1) == 0
    cache = pl.BlockSpec((1, HB_S, HD, n_past), lambda b, h: (b, h, 0, 0))
    return pl.pallas_call(
        _s_attn_kernel,
        grid=(rows, NH // HB_S),
        in_specs=[pl.BlockSpec((1, HB_S, 8, HD), lambda b, h: (b, h, 0, 0)), cache, cache],
        out_specs=pl.BlockSpec((1, HB_S, HD), lambda b, h: (b, h, 0)),
        out_shape=jax.ShapeDtypeStruct((rows, NH, HD), F32),
        compiler_params=_cparams(("arbitrary", "arbitrary"), 40),
        name="s_attn",
    )(qkv, cache_kt, cache_vt)


def _s_post1_kernel(x2_ref, mix_ref, modm_ref, modf_ref, gpost_ref, gpre_ref, rwt_ref, rb_ref,
                    x3_ref, hf_ref, gc_ref):
    _, _, gt = _split3(modm_ref[...])
    x3 = x2_ref[...] + gt * (_rms_scale(mix_ref[...]) * gpost_ref[...])
    x3_ref[...] = x3
    sh, sc, _ = _split3(modf_ref[...])
    hf = _rms_scale(x3) * gpre_ref[...] * (1.0 + sc) + sh
    hf_ref[...] = hf
    logits_t = lax.dot_general(rwt_ref[...], hf, (((1,), (1,)), ((), ())),
                               precision=HIGHEST, preferred_element_type=F32) + rb_ref[...]
    gc_ref[...] = _router_top2(logits_t)


def _s_post1(x2, mix, modm, modf, gpost, gpre, rwt, rb):
    rows = x2.shape[0]
    return pl.pallas_call(
        _s_post1_kernel,
        out_shape=[jax.ShapeDtypeStruct((rows, D), F32), jax.ShapeDtypeStruct((rows, D), F32),
                   jax.ShapeDtypeStruct((N_EXPERTS, rows), F32)],
        name="s_post1",
    )(x2, mix, modm, modf, gpost, gpre, rwt, rb)


def _s_post2_kernel(x3_ref, f_ref, modf_ref, gpost_ref, y_ref):
    _, _, gt = _split3(modf_ref[...])
    y_ref[...] = x3_ref[...] + gt * (_rms_scale(f_ref[...]) * gpost_ref[...])


def _s_post2(x3, f, modf, gpost):
    return pl.pallas_call(
        _s_post2_kernel,
        out_shape=jax.ShapeDtypeStruct(x3.shape, F32),
        name="s_post2",
    )(x3, f, modf, gpost)


def _rope_tables(pos):
    half = HD // 2
    inv = ROPE_THETA ** (-jnp.arange(half, dtype=F32) / half)
    ang = pos.astype(F32)[:, None] * inv[None, :]
    cos, sin = jnp.cos(ang), jnp.sin(ang)
    cos = jnp.tile(cos, (1, LANES // half))
    sin = jnp.tile(jnp.concatenate([-sin, sin], axis=1), (1, LANES // HD))
    return cos, sin


def kernel(x_prompt, x_sample, state_pool, cache_k, cache_v, c_prompt, c_sample, ada_mix_w, ada_mix_b, g_pre_mix, g_post_mix, pool_w, pool_scale, g_kv, w_kv, w_q, w_o, ada_ffn_w, ada_ffn_b, g_pre_ffn, g_post_ffn, ffn_w1, ffn_w3, ffn_w2, router_w, router_b, moe_w1, moe_w3, moe_w2):
    b, t, _ = x_prompt.shape
    nsmp = x_sample.shape[0]
    n = b * t
    past_len = 16384
    assert x_sample.shape[1] == 1 and g_pre_mix.shape[0] == 2 and t % (ATTN_GROUPS[-1][1] * BLK) == 0
    vec = lambda a: a.reshape(1, D)

    c_all = jnp.concatenate([c_prompt, c_sample], axis=0)
    mods_mix = _ada(c_all, ada_mix_w, ada_mix_b)
    mods_ffn = _ada(c_all, ada_ffn_w, ada_ffn_b)
    pm_mix = mods_mix[:, :b].reshape(2, b, 1, 3 * D)
    pm_ffn = mods_ffn[:, :b].reshape(2, b, 1, 3 * D)
    sm_mix = mods_mix[:, b:]
    sm_ffn = mods_ffn[:, b:]

    rwt = router_w[0].T
    rb = router_b[0].reshape(N_EXPERTS, 1)

    x1, pool16 = _mix0(x_prompt, pm_mix, vec(g_pre_mix[0]), vec(g_post_mix[0]),
                       pool_w[0].astype(BF16), vec(pool_scale[0]))
    cos_p, sin_p = _rope_tables(jnp.arange(t, dtype=jnp.int32))
    x2 = _ffn0(x1.reshape(n, D), pm_ffn, vec(g_pre_ffn[0]), vec(g_post_ffn[0]),
               ffn_w1[0].astype(BF16), ffn_w3[0].astype(BF16), ffn_w2[0].astype(BF16), t)
    kt_p, vt_p, *kv_l = _kvproj(x2, vec(g_kv), cos_p, sin_p, w_kv.astype(BF16), b, t)
    q_l = _qproj(x2, pm_mix, vec(g_pre_mix[1]), cos_p, sin_p, w_q[0].astype(BF16), b, t)
    attn = _attention(q_l, kv_l[:3], kv_l[3:], b, t)
    x3, hf1, gates_t, gates_c = _post1(attn, x2, pm_mix, pm_ffn, vec(g_post_mix[1]),
                                       vec(g_pre_ffn[1]), w_o[0].astype(BF16), rwt, rb, t)

    state_t = jnp.transpose(state_pool[:, 0], (1, 0, 2))
    x1s, pool_t, hf0s = _s_mix0(x_sample.reshape(nsmp, D), state_t, sm_mix[0], sm_ffn[0],
                                vec(g_pre_mix[0]), vec(g_post_mix[0]), vec(g_pre_ffn[0]),
                                pool_w[0], vec(pool_scale[0]), past_len)
    f0s = _s_ffn(hf0s, ffn_w1[0], ffn_w3[0], ffn_w2[0])
    x2s, kvin_s, hm1s = _s_post0(x1s, f0s, sm_ffn[0], sm_mix[1], vec(g_post_ffn[0]), vec(g_kv), vec(g_pre_mix[1]))
    cos_s, sin_s = _rope_tables(jnp.full((1,), past_len, jnp.int32))
    ones, zeros = jnp.ones((1, D), F32), jnp.zeros((1, D), F32)
    kv_s = _s_linear(kvin_s, w_kv, (jnp.concatenate([jnp.tile(cos_s, (1, D // LANES)), ones], axis=1),
                                    jnp.concatenate([jnp.tile(sin_s, (1, D // LANES)), zeros], axis=1)))
    q_s = _s_linear(hm1s, w_q[0], (jnp.tile(cos_s, (1, 3 * D // LANES)), jnp.tile(sin_s, (1, 3 * D // LANES))))
    k_s = kv_s[:, :D].reshape(nsmp, NH, HD)
    v_s = kv_s[:, D:].reshape(nsmp, NH, HD)
    cache_kt = jnp.transpose(cache_k, (0, 2, 3, 1))
    cache_vt = jnp.transpose(cache_v, (0, 2, 3, 1))
    qkv = jnp.concatenate([jnp.transpose(q_s.reshape(nsmp, len(ATTN_GROUPS), NH, HD), (0, 2, 1, 3)),
                           k_s[:, :, None], v_s[:, :, None]], axis=2)
    qkv = jnp.pad(qkv, ((0, 0), (0, 0), (0, 8 - qkv.shape[2]), (0, 0)))
    attn_s = _s_attn(qkv, cache_kt, cache_vt).reshape(nsmp, D)
    mix1s = _s_linear(attn_s, w_o[0])
    x3s, hf1s, gts = _s_post1(x2s, mix1s, sm_mix[1], sm_ffn[1], vec(g_post_mix[1]), vec(g_pre_ffn[1]), rwt, rb)

    hf_blk = jnp.zeros((TB, D), BF16).at[:nsmp].set(hf1s.astype(BF16))
    gt_blk = jnp.zeros((N_EXPERTS, TB), F32).at[:, :nsmp].set(gts)
    gc_blk = jnp.zeros((TB, LANES), F32).at[:nsmp, :N_EXPERTS].set(gts.T)
    off, cnt, tile_e, n_used = _route_plan(jnp.concatenate([gates_t, gt_blk], axis=1), n + TB)
    xs = jnp.zeros((_max_sorted_rows(n + TB), D), BF16)
    xs = _dispatch(off, cnt, hf1, gates_t, xs, 0)
    xs = _dispatch(off, cnt, hf_blk, gt_blk, xs, n // TB)
    ys = _experts(tile_e, n_used, xs, moe_w1[0], moe_w3[0], moe_w2[0])
    y_prompt = _combine(off, cnt, gates_t, gates_c, x3, pm_ffn, vec(g_post_ffn[1]), ys, t).reshape(b, t, D)
    moe_s = _combine_rows(off, cnt, gt_blk, gc_blk, ys, n // TB)[:nsmp]
    y_sample = _s_post2(x3s, moe_s, sm_ffn[1], vec(g_post_ffn[1])).reshape(nsmp, 1, D)

    pool_prompt = pool16[:, None, 1:, :]
    pool_sample = jnp.transpose(pool_t, (1, 0, 2))[:, None]
    return (y_prompt, y_sample, pool_prompt, pool_sample,
            jnp.transpose(kt_p, (0, 3, 1, 2)), jnp.transpose(vt_p, (0, 3, 1, 2)),
            k_s[:, None], v_s[:, None])
```

```python
import functools

import jax
import jax.numpy as jnp
import numpy as np
from jax import lax
from jax.experimental import pallas as pl
from jax.experimental.pallas import tpu as pltpu

F32, BF16, I32 = jnp.float32, jnp.bfloat16, jnp.int32
HIGHEST = lax.Precision.HIGHEST

D = 1024
POOL_WINDOWS = (2, 4, 8, 16)
PG = D // len(POOL_WINDOWS)
POOL_STATE = max(POOL_WINDOWS) - 1
POOL_ROWS = POOL_STATE + 1
HALO = 2 * POOL_ROWS
ATTN_GROUPS = ((128, 1), (512, 4), (2048, 16))
NH = 16
HD = 64
BLK = 128
ROPE_THETA = 10000.0
N_EXPERTS = 8
EPS = 1e-6
SCALE = HD ** -0.5
LANES = 128
MIB = 1024 * 1024

TT_MIX = 512
TM_FFN = 1024
HB_S = 8
ATTN_UNITS = 8
FC_FFN = 256
TM_POST = 512
TM_WO = 1024
TB = 512
SUB = 128
NS = TB // SUB
TMX = 768
FCX = 512
ROW_ALIGN = 16


def _cparams(sem, vmem_mib):
    return pltpu.CompilerParams(dimension_semantics=sem, vmem_limit_bytes=vmem_mib * MIB)


def _rms_scale(x):
    return x * lax.rsqrt(jnp.mean(x * x, axis=-1, keepdims=True) + EPS)


def _split3(m):
    return m[:, :D], m[:, D:2 * D], m[:, 2 * D:]


def _silu(x):
    return x * jax.nn.sigmoid(x)


def _rope_cols(x, cos, sin):
    lane = lax.broadcasted_iota(I32, (x.shape[0], LANES), 1)
    first = (lane % HD) < (HD // 2)
    shared = cos.shape[1] == LANES
    outs = []
    for c in range(x.shape[1] // LANES):
        cols = slice(c * LANES, (c + 1) * LANES)
        xc = x[:, cols]
        swapped = jnp.where(first, pltpu.roll(xc, LANES - HD // 2, 1), pltpu.roll(xc, HD // 2, 1))
        outs.append(xc * (cos if shared else cos[:, cols]) + swapped * (sin if shared else sin[:, cols]))
    return jnp.concatenate(outs, axis=1)


def _dot_hi(a, b):
    return jnp.dot(a, b, precision=HIGHEST, preferred_element_type=F32)


def _dot(a, b):
    return jnp.dot(a, b, preferred_element_type=F32)


def _ada_kernel(c_ref, w_ref, b_ref, o_ref):
    o_ref[0] = _dot_hi(_silu(c_ref[...]), w_ref[0]) + b_ref[0]


def _ada(c_all, w, b):
    nl, _, n3 = w.shape
    rows = c_all.shape[0]
    cw = 768
    return pl.pallas_call(
        _ada_kernel,
        grid=(nl, n3 // cw),
        in_specs=[pl.BlockSpec((rows, D), lambda l, j: (0, 0)),
                  pl.BlockSpec((1, D, cw), lambda l, j: (l, 0, j)),
                  pl.BlockSpec((1, 1, cw), lambda l, j: (l, 0, j))],
        out_specs=pl.BlockSpec((1, rows, cw), lambda l, j: (l, 0, j)),
        out_shape=jax.ShapeDtypeStruct((nl, rows, n3), F32),
        compiler_params=_cparams(("arbitrary", "arbitrary"), 32),
        name="ada_mod",
    )(c_all, w, b.reshape(nl, 1, n3))


def _mix0_kernel(x_ref, xh_ref, mod_ref, gpre_ref, gpost_ref, pw_ref, ps_ref,
                 x1_ref, pool_ref, ext_ref, lv_ref):
    i = pl.program_id(1)
    tt = x_ref.shape[1]
    sh, sc, gt = _split3(mod_ref[0, 0])
    gpre = gpre_ref[...]
    x = x_ref[0]
    hm = _rms_scale(x) * gpre * (1.0 + sc) + sh
    hh = _rms_scale(xh_ref[0]) * gpre * (1.0 + sc) + sh
    ext_ref[0:HALO, :] = jnp.where(i > 0, hh, 0.0)
    ext_ref[HALO:, :] = hm
    t = i * tt + lax.broadcasted_iota(I32, (tt, 1), 0)
    rows = tt + HALO
    parts = []
    for g, w in enumerate(POOL_WINDOWS):
        lo = g * PG
        tok = hm[:, lo:lo + PG]
        levels = w.bit_length() - 1
        assert w == 1 << levels
        first = [HALO]
        for k in range(levels - 1, 0, -1):
            first.insert(0, (first[0] - (1 << k)) // 8 * 8)
        for k in range(1, levels + 1):
            r0, back = first[k - 1], 1 << (k - 1)
            if k == 1:
                s = ext_ref[r0:rows, lo:lo + PG] + ext_ref[r0 - back:rows - back, lo:lo + PG]
            else:
                prev = lv_ref.at[k % 2]
                s = prev[r0:rows, :] + prev[r0 - back:rows - back, :]
            if k < levels:
                lv_ref[(k + 1) % 2, r0:rows, :] = s
        inv_cnt = 1.0 / jnp.minimum(w, t + 1).astype(F32)
        parts.append(_dot((s * inv_cnt - tok).astype(BF16), pw_ref[g]))
    mix = jnp.concatenate(parts, axis=1) * ps_ref[...]
    x1_ref[0] = x + gt * (_rms_scale(mix) * gpost_ref[...])

    @pl.when(i == pl.num_programs(1) - 1)
    def _():
        pool_ref[0] = ext_ref[rows - POOL_ROWS:rows, :]


def _mix0(x, mods, gpre, gpost, pw_bf, ps):
    b, t, _ = x.shape
    tt = TT_MIX
    hb = tt // HALO
    return pl.pallas_call(
        _mix0_kernel,
        grid=(b, t // tt),
        in_specs=[pl.BlockSpec((1, tt, D), lambda bb, i: (bb, i, 0)),
                  pl.BlockSpec((1, HALO, D), lambda bb, i: (bb, jnp.maximum(i * hb - 1, 0), 0)),
                  pl.BlockSpec((1, 1, 1, 3 * D), lambda bb, i: (0, bb, 0, 0)),
                  pl.BlockSpec((1, D), lambda bb, i: (0, 0)),
                  pl.BlockSpec((1, D), lambda bb, i: (0, 0)),
                  pl.BlockSpec((len(POOL_WINDOWS), PG, PG), lambda bb, i: (0, 0, 0)),
                  pl.BlockSpec((1, D), lambda bb, i: (0, 0))],
        out_specs=[pl.BlockSpec((1, tt, D), lambda bb, i: (bb, i, 0)),
                   pl.BlockSpec((1, POOL_ROWS, D), lambda bb, i: (bb, 0, 0))],
        out_shape=[jax.ShapeDtypeStruct((b, t, D), F32),
                   jax.ShapeDtypeStruct((b, POOL_ROWS, D), F32)],
        scratch_shapes=[pltpu.VMEM((tt + HALO, D), F32), pltpu.VMEM((2, tt + HALO, PG), F32)],
        compiler_params=_cparams(("arbitrary", "arbitrary"), 40),
        name="mix0",
    )(x, x, mods, gpre, gpost, pw_bf, ps)


def _store_dilated(val, plane_ref, dils, out_refs):
    tm = val.shape[0]
    for hp in range(NH // 2):
        plane_ref[hp] = val[:, hp * LANES:(hp + 1) * LANES]
    for dil, out_ref in zip(dils, out_refs):
        for hp in range(NH // 2):
            for r in range(dil):
                out_ref[0, hp, r] = plane_ref[hp, pl.ds(r, tm // dil, stride=dil), :].astype(BF16)


def _dilated_specs(b, t, tm):
    per_b = t // tm
    specs = [pl.BlockSpec((1, NH // 2, dil, tm // dil, LANES), lambda i: (i // per_b, 0, 0, i % per_b, 0))
             for _, dil in ATTN_GROUPS]
    shapes = [jax.ShapeDtypeStruct((b, NH // 2, dil, t // dil, LANES), BF16) for _, dil in ATTN_GROUPS]
    return specs, shapes


def _ffn0_kernel(x1_ref, modf_ref, gpre_ref, gpost_ref, w1_ref, w3_ref, w2_ref, x2_ref, acc_ref):
    sh, sc, gt = _split3(modf_ref[0, 0])
    x1 = x1_ref[...]
    hf = (_rms_scale(x1) * gpre_ref[...] * (1.0 + sc) + sh).astype(BF16)
    for c in range(w1_ref.shape[1] // FC_FFN):
        cs = slice(c * FC_FFN, (c + 1) * FC_FFN)
        a = (_silu(_dot(hf, w1_ref[:, cs])) * _dot(hf, w3_ref[:, cs])).astype(BF16)
        contrib = _dot(a, w2_ref[cs, :])
        if c == 0:
            acc_ref[...] = contrib
        else:
            acc_ref[...] += contrib
    x2 = x1 + gt * (_rms_scale(acc_ref[...]) * gpost_ref[...])
    x2_ref[...] = x2


def _ffn0(x1, mods_ffn, gpre, gpost, w1, w3, w2, t):
    n = x1.shape[0]
    tm = TM_FFN
    per_b = t // tm
    dff = w1.shape[1]
    row = lambda i: (i, 0)
    fixed = lambda i: (0, 0)
    resident = lambda shape: pl.BlockSpec(shape, fixed, pipeline_mode=pl.Buffered(1))
    vec = pl.BlockSpec((1, D), fixed)
    return pl.pallas_call(
        _ffn0_kernel,
        grid=(n // tm,),
        in_specs=[pl.BlockSpec((tm, D), row),
                  pl.BlockSpec((1, 1, 1, 3 * D), lambda i: (0, i // per_b, 0, 0)),
                  vec, vec,
                  resident((D, dff)), resident((D, dff)), resident((dff, D))],
        out_specs=pl.BlockSpec((tm, D), row),
        out_shape=jax.ShapeDtypeStruct((n, D), F32),
        scratch_shapes=[pltpu.VMEM((tm, D), F32)],
        compiler_params=_cparams(("arbitrary",), 56),
        name="ffn0",
    )(x1, mods_ffn, gpre, gpost, w1, w3, w2)


def _kvproj_kernel(x2_ref, gkv_ref, cos_ref, sin_ref, wkv_ref,
                   kt_ref, vt_ref, k0_ref, k1_ref, k2_ref, v0_ref, v1_ref, v2_ref, plane_ref):
    tm = x2_ref.shape[0]
    kv = _dot((_rms_scale(x2_ref[...]) * gkv_ref[...]).astype(BF16), wkv_ref[...])
    k = _rope_cols(kv[:, :D], cos_ref[...], sin_ref[...])
    v = kv[:, D:]
    kt_ref[0] = k.T.reshape(NH, HD, tm)
    vt_ref[0] = v.T.reshape(NH, HD, tm)
    dils = [dil for _, dil in ATTN_GROUPS]
    _store_dilated(k, plane_ref, dils, (k0_ref, k1_ref, k2_ref))
    _store_dilated(v, plane_ref, dils, (v0_ref, v1_ref, v2_ref))


def _kvproj(x2, gkv, cos, sin, wkv, b, t):
    n = x2.shape[0]
    tm = TM_POST
    per_b = t // tm
    fixed = lambda i: (0, 0)
    dspecs, dshapes = _dilated_specs(b, t, tm)
    tspec = pl.BlockSpec((1, NH, HD, tm), lambda i: (i // per_b, 0, 0, i % per_b))
    tshape = jax.ShapeDtypeStruct((b, NH, HD, t), F32)
    return pl.pallas_call(
        _kvproj_kernel,
        grid=(n // tm,),
        in_specs=[pl.BlockSpec((tm, D), lambda i: (i, 0)),
                  pl.BlockSpec((1, D), fixed),
                  pl.BlockSpec((tm, LANES), lambda i: (i % per_b, 0)),
                  pl.BlockSpec((tm, LANES), lambda i: (i % per_b, 0)),
                  pl.BlockSpec((D, 2 * D), fixed, pipeline_mode=pl.Buffered(1))],
        out_specs=[tspec, tspec] + dspecs + dspecs,
        out_shape=[tshape, tshape] + dshapes + dshapes,
        scratch_shapes=[pltpu.VMEM((NH // 2, tm, LANES), F32)],
        compiler_params=_cparams(("arbitrary",), 48),
        name="kv_proj",
    )(x2, gkv, cos, sin, wkv)


def _qproj_kernel(x2_ref, modm_ref, gpre1_ref, cos_ref, sin_ref, wq_ref, q0_ref, q1_ref, q2_ref, scr_ref):
    sh1, sc1, _ = _split3(modm_ref[0, 0])
    hm1 = (_rms_scale(x2_ref[...]) * gpre1_ref[...] * (1.0 + sc1) + sh1).astype(BF16)
    q = _rope_cols(_dot(hm1, wq_ref[...]), cos_ref[...], sin_ref[...]) * SCALE
    for g, ((_, dil), q_ref) in enumerate(zip(ATTN_GROUPS, (q0_ref, q1_ref, q2_ref))):
        _store_dilated(q[:, g * D:(g + 1) * D], scr_ref, (dil,), (q_ref,))


def _qproj(x2, mods_mix, gpre1, cos, sin, wq, b, t):
    n = x2.shape[0]
    tm = TM_POST
    per_b = t // tm
    fixed = lambda i: (0, 0)
    dspecs, dshapes = _dilated_specs(b, t, tm)
    return pl.pallas_call(
        _qproj_kernel,
        grid=(n // tm,),
        in_specs=[pl.BlockSpec((tm, D), lambda i: (i, 0)),
                  pl.BlockSpec((1, 1, 1, 3 * D), lambda i: (1, i // per_b, 0, 0)),
                  pl.BlockSpec((1, D), fixed),
                  pl.BlockSpec((tm, LANES), lambda i: (i % per_b, 0)),
                  pl.BlockSpec((tm, LANES), lambda i: (i % per_b, 0)),
                  pl.BlockSpec((D, 3 * D), fixed, pipeline_mode=pl.Buffered(1))],
        out_specs=dspecs,
        out_shape=dshapes,
        scratch_shapes=[pltpu.VMEM((NH // 2, tm, LANES), F32)],
        compiler_params=_cparams(("arbitrary",), 48),
        name="q_proj",
    )(x2, mods_mix, gpre1, cos, sin, wq)


def _attn_kernel(q0_ref, q1_ref, q2_ref, k0_ref, k1_ref, k2_ref, v0_ref, v1_ref, v2_ref,
                 o_ref, og_ref, lg_ref, bias_ref):
    t = o_ref.shape[2]
    first_head = lax.broadcasted_iota(I32, (BLK, LANES), 1) < HD
    nt = (((1,), (1,)), ((), ()))

    @pl.when((pl.program_id(0) == 0) & (pl.program_id(1) == 0))
    def _():
        qi = lax.broadcasted_iota(I32, (2 * BLK, 2 * BLK), 0) % BLK
        kj = lax.broadcasted_iota(I32, (2 * BLK, 2 * BLK), 1)
        for v in range(2):
            dist = v * BLK + qi - kj
            bias_ref[v] = jnp.where((dist >= 0) & (dist <= BLK), 0.0, -jnp.inf)

    groups = zip(ATTN_GROUPS, (q0_ref, q1_ref, q2_ref), (k0_ref, k1_ref, k2_ref), (v0_ref, v1_ref, v2_ref))
    for g, ((win, dil), q_ref, k_ref, v_ref) in enumerate(groups):
        assert win // dil == BLK
        sub = t // dil
        nb = sub // BLK
        kw = min(2 * BLK, sub)

        def scores(u, nb=nb, kw=kw, q_ref=q_ref, k_ref=k_ref):
            r = u // nb
            n = u % nb
            q_lo = pl.multiple_of(n * BLK, BLK)
            k_lo = pl.multiple_of(jnp.maximum(n - 1, 0) * BLK, BLK)
            qp = q_ref[0, 0, r, pl.ds(q_lo, BLK), :]
            zero = jnp.zeros_like(qp)
            qs = jnp.concatenate([jnp.where(first_head, qp, zero), jnp.where(first_head, zero, qp)], axis=0)
            s = lax.dot_general(qs, k_ref[0, 0, r, pl.ds(k_lo, kw), :], nt, preferred_element_type=F32)
            return s + bias_ref[jnp.minimum(n, 1), :, 0:kw], r, q_lo, k_lo

        def finish(s, r, q_lo, k_lo, dil=dil, kw=kw, v_ref=v_ref, g=g):
            m = jnp.max(s, axis=-1, keepdims=True)
            p = jnp.exp(s - m)
            den = jnp.sum(p, axis=-1, keepdims=True)
            o2 = _dot(p.astype(BF16), v_ref[0, 0, r, pl.ds(k_lo, kw), :]) * (1.0 / den)
            l2 = jnp.broadcast_to(m + jnp.log(den), (2 * BLK, LANES))
            rows = pl.ds(q_lo * dil + r, BLK, stride=dil) if dil > 1 else pl.ds(q_lo, BLK)
            og_ref[g, rows, :] = jnp.where(first_head, o2[:BLK], o2[BLK:])
            lg_ref[g, rows, :] = jnp.where(first_head, l2[:BLK], l2[BLK:])

        def trip(i, carry, scores=scores, finish=finish):
            started = [scores(i * ATTN_UNITS + j) for j in range(ATTN_UNITS)]
            for args in started:
                finish(*args)
            return carry

        lax.fori_loop(0, dil * nb // ATTN_UNITS, trip, 0)

    mt = 2 * BLK

    def merge(c, carry):
        rows = pl.ds(pl.multiple_of(c * mt, mt), mt)
        ls = [lg_ref[g, rows, :] for g in range(len(ATTN_GROUPS))]
        mx = jnp.maximum(jnp.maximum(ls[0], ls[1]), ls[2])
        es = [jnp.exp(l - mx) for l in ls]
        tot = es[0] + es[1] + es[2]
        out = (es[0] / tot) * og_ref[0, rows, :]
        for g in range(1, len(ATTN_GROUPS)):
            out = out + (es[g] / tot) * og_ref[g, rows, :]
        o_ref[0, 0, rows, :] = out.astype(BF16)
        return carry

    lax.fori_loop(0, t // mt, merge, 0)


def _attention(q_l, k_l, v_l, b, t):
    specs = [pl.BlockSpec((1, 1, dil, t // dil, LANES), lambda bb, hp: (bb, hp, 0, 0, 0))
             for _, dil in ATTN_GROUPS]
    return pl.pallas_call(
        _attn_kernel,
        grid=(b, NH // 2),
        in_specs=specs * 3,
        out_specs=pl.BlockSpec((1, 1, t, LANES), lambda bb, hp: (bb, hp, 0, 0)),
        out_shape=jax.ShapeDtypeStruct((b, NH // 2, t, LANES), BF16),
        scratch_shapes=[pltpu.VMEM((len(ATTN_GROUPS), t, LANES), F32),
                        pltpu.VMEM((len(ATTN_GROUPS), t, LANES), F32),
                        pltpu.VMEM((2, 2 * BLK, 2 * BLK), F32)],
        compiler_params=_cparams(("arbitrary", "arbitrary"), 40),
        name="dilated_attn",
    )(*q_l, *k_l, *v_l)


def _router_top2(logits_t):
    idx = lax.broadcasted_iota(I32, logits_t.shape, 0)
    v1 = jnp.max(logits_t, axis=0, keepdims=True)
    i1 = jnp.min(jnp.where(logits_t == v1, idx, N_EXPERTS), axis=0, keepdims=True)
    m1 = idx == i1
    rest = jnp.where(m1, -jnp.inf, logits_t)
    v2 = jnp.max(rest, axis=0, keepdims=True)
    i2 = jnp.min(jnp.where(rest == v2, idx, N_EXPERTS), axis=0, keepdims=True)
    m2 = idx == i2
    e2 = jnp.exp(v2 - v1)
    den = 1.0 + e2
    return jnp.where(m1, 1.0 / den, 0.0) + jnp.where(m2, e2 / den, 0.0)


def _post1_kernel(a_ref, x2_ref, modm_ref, modf_ref, gpost_ref, gpre_ref, wo_ref, rwt_ref, rb_ref,
                  x3_ref, hf_ref, gt_ref, gc_ref):
    o = jnp.concatenate([a_ref[0, hp] for hp in range(NH // 2)], axis=1)
    mix = _dot(o, wo_ref[...])
    _, _, gt = _split3(modm_ref[0, 0])
    x3 = x2_ref[...] + gt * (_rms_scale(mix) * gpost_ref[...])
    x3_ref[...] = x3
    sh, sc, _ = _split3(modf_ref[0, 0])
    hf = _rms_scale(x3) * gpre_ref[...] * (1.0 + sc) + sh
    hf_ref[...] = hf.astype(BF16)
    logits_t = lax.dot_general(rwt_ref[...], hf, (((1,), (1,)), ((), ())),
                               precision=HIGHEST, preferred_element_type=F32) + rb_ref[...]
    gates_t = _router_top2(logits_t)
    gt_ref[...] = gates_t
    padded = jnp.concatenate([gates_t, jnp.zeros((LANES - N_EXPERTS, gates_t.shape[1]), F32)], axis=0)
    gc_ref[...] = padded.T


def _post1(attn, x2, mods_mix, mods_ffn, gpost, gpre, wo_bf, rwt, rb, t):
    n = x2.shape[0]
    tm = TM_WO
    per_b = t // tm
    row = lambda i: (i, 0)
    fixed = lambda i: (0, 0)
    vec = pl.BlockSpec((1, D), fixed)
    tile = pl.BlockSpec((tm, D), row)
    return pl.pallas_call(
        _post1_kernel,
        grid=(n // tm,),
        in_specs=[pl.BlockSpec((1, NH // 2, tm, LANES), lambda i: (i // per_b, 0, i % per_b, 0)),
                  tile,
                  pl.BlockSpec((1, 1, 1, 3 * D), lambda i: (1, i // per_b, 0, 0)),
                  pl.BlockSpec((1, 1, 1, 3 * D), lambda i: (1, i // per_b, 0, 0)),
                  vec, vec,
                  pl.BlockSpec((D, D), fixed),
                  pl.BlockSpec((N_EXPERTS, D), fixed),
                  pl.BlockSpec((N_EXPERTS, 1), fixed)],
        out_specs=[tile, tile, pl.BlockSpec((N_EXPERTS, tm), lambda i: (0, i)),
                   pl.BlockSpec((tm, LANES), row)],
        out_shape=[jax.ShapeDtypeStruct((n, D), F32), jax.ShapeDtypeStruct((n, D), BF16),
                   jax.ShapeDtypeStruct((N_EXPERTS, n), F32), jax.ShapeDtypeStruct((n, LANES), F32)],
        compiler_params=_cparams(("arbitrary",), 52),
        name="wo_router",
    )(attn, x2, mods_mix, mods_ffn, gpost, gpre, wo_bf, rwt, rb)


def _route_plan(gates_t, n):
    nb = n // TB
    cnt = jnp.sum((gates_t > 0.0).reshape(N_EXPERTS, nb, TB), axis=-1).astype(I32)
    c_al = (cnt + ROW_ALIGN - 1) // ROW_ALIGN * ROW_ALIGN
    within = jnp.cumsum(c_al, axis=1) - c_al
    tot = jnp.sum(c_al, axis=1)
    region = (tot + SUB + TMX - 1) // TMX * TMX
    start = jnp.cumsum(region) - region
    off = (start[:, None] + within).T.reshape(-1)
    n_used = (jnp.sum(region) // TMX).astype(I32)
    tile_row = jnp.arange(_max_sorted_rows(n) // TMX, dtype=I32) * TMX
    tile_e = jnp.sum(tile_row[:, None] >= (start + region)[None, :], axis=1).astype(I32)
    tile_e = jnp.minimum(tile_e, N_EXPERTS - 1)
    last_e = tile_e[jnp.maximum(n_used - 1, 0)]
    tile_e = jnp.where(jnp.arange(tile_e.shape[0]) < n_used, tile_e, last_e)
    return off.astype(I32), cnt.T.reshape(-1), tile_e, n_used.reshape(1)


def _max_sorted_rows(n):
    nb = n // TB
    worst = 2 * n + nb * N_EXPERTS * (ROW_ALIGN - 1) + N_EXPERTS * (SUB + TMX - 1)
    return (worst + TMX - 1) // TMX * TMX


def _dispatch_kernel(off_ref, cnt_ref, hf_ref, gt_ref, xs_in_ref, xs_ref, stage_ref, sem, *, block0):
    del xs_in_ref
    b = pl.program_id(0) + block0
    hf = hf_ref[...]
    sel = gt_ref[...] > 0.0
    before = lax.broadcasted_iota(I32, (TB, TB), 0) < lax.broadcasted_iota(I32, (TB, TB), 1)
    rank = _dot(sel.astype(BF16), before.astype(BF16)).astype(I32)
    jj = lax.broadcasted_iota(I32, (SUB, TB), 0)

    def copy(e, s):
        dst = pl.multiple_of(off_ref[b * N_EXPERTS + e] + s * SUB, ROW_ALIGN)
        return pltpu.make_async_copy(stage_ref.at[e * NS + s], xs_ref.at[pl.ds(dst, SUB)], sem.at[e * NS + s])

    for e in range(N_EXPERTS):
        c = cnt_ref[b * N_EXPERTS + e]
        for s in range(NS):
            @pl.when(s * SUB < c)
            def _(e=e, s=s):
                pick = ((rank[e:e + 1, :] == jj + s * SUB) & sel[e:e + 1, :]).astype(BF16)
                stage_ref[e * NS + s] = _dot(pick, hf).astype(BF16)
                copy(e, s).start()
    for e in range(N_EXPERTS):
        c = cnt_ref[b * N_EXPERTS + e]
        for s in range(NS):
            @pl.when(s * SUB < c)
            def _(e=e, s=s):
                copy(e, s).wait()


def _dispatch(off, cnt, hf_bf, gates_t, xs, block0):
    n = hf_bf.shape[0]
    rows = xs.shape[0]
    return pl.pallas_call(
        functools.partial(_dispatch_kernel, block0=block0),
        grid_spec=pltpu.PrefetchScalarGridSpec(
            num_scalar_prefetch=2,
            grid=(n // TB,),
            in_specs=[pl.BlockSpec((TB, D), lambda b, o, c: (b, 0)),
                      pl.BlockSpec((N_EXPERTS, TB), lambda b, o, c: (0, b)),
                      pl.BlockSpec(memory_space=pl.ANY)],
            out_specs=pl.BlockSpec(memory_space=pl.ANY),
            scratch_shapes=[pltpu.VMEM((N_EXPERTS * NS, SUB, D), BF16),
                            pltpu.SemaphoreType.DMA((N_EXPERTS * NS,))]),
        out_shape=jax.ShapeDtypeStruct((rows, D), BF16),
        input_output_aliases={4: 0},
        compiler_params=_cparams(("arbitrary",), 40),
        name="moe_dispatch",
    )(off, cnt, hf_bf, gates_t, xs)


def _expert_kernel(te_ref, nu_ref, x_ref, w1_ref, w3_ref, w2_ref, y_ref, acc_ref):
    del te_ref
    i, j = pl.program_id(0), pl.program_id(1)

    @pl.when(i < nu_ref[0])
    def _():
        x = x_ref[...]
        a = (_silu(_dot(x, w1_ref[0].astype(BF16))) * _dot(x, w3_ref[0].astype(BF16))).astype(BF16)
        contrib = _dot(a, w2_ref[0].astype(BF16))

        @pl.when(j == 0)
        def _():
            acc_ref[...] = contrib

        @pl.when(j > 0)
        def _():
            acc_ref[...] += contrib

        @pl.when(j == pl.num_programs(1) - 1)
        def _():
            y_ref[...] = acc_ref[...].astype(BF16)

    @pl.when((i >= nu_ref[0]) & (j == 0))
    def _():
        y_ref[...] = jnp.zeros(y_ref.shape, BF16)


def _experts(tile_e, n_used, xs, w1, w3, w2):
    rows = xs.shape[0]
    dff = w1.shape[2]
    nj = dff // FCX

    def tile_idx(i, j, te, nu):
        return (jnp.minimum(i, nu[0] - 1), 0)

    def chunk(i, j, nu):
        return jnp.where(i < nu[0], j, nj - 1)

    return pl.pallas_call(
        _expert_kernel,
        grid_spec=pltpu.PrefetchScalarGridSpec(
            num_scalar_prefetch=2,
            grid=(rows // TMX, nj),
            in_specs=[pl.BlockSpec((TMX, D), tile_idx),
                      pl.BlockSpec((1, D, FCX), lambda i, j, te, nu: (te[i], 0, chunk(i, j, nu))),
                      pl.BlockSpec((1, D, FCX), lambda i, j, te, nu: (te[i], 0, chunk(i, j, nu))),
                      pl.BlockSpec((1, FCX, D), lambda i, j, te, nu: (te[i], chunk(i, j, nu), 0))],
            out_specs=pl.BlockSpec((TMX, D), lambda i, j, te, nu: (i, 0)),
            scratch_shapes=[pltpu.VMEM((TMX, D), F32)]),
        out_shape=jax.ShapeDtypeStruct((rows, D), BF16),
        compiler_params=_cparams(("arbitrary", "arbitrary"), 56),
        name="moe_experts",
    )(tile_e, n_used, xs, w1, w3, w2)


SLOTS = 2 * TB // SUB + N_EXPERTS


def _gather_expert_rows(b, first_step, off_ref, cnt_ref, gt_ref, gc_ref, ys_ref, ybuf_ref, p_ref, sem):
    @pl.when(first_step)
    def _():
        ybuf_ref[...] = jnp.zeros(ybuf_ref.shape, BF16)

    def copy(e, s, slot):
        src = pl.multiple_of(off_ref[b * N_EXPERTS + e] + s * SUB, ROW_ALIGN)
        dst = pl.multiple_of(slot * SUB, SUB)
        return pltpu.make_async_copy(ys_ref.at[pl.ds(src, SUB)], ybuf_ref.at[pl.ds(dst, SUB)], sem.at[slot])

    pairs = [(e, s) for e in range(N_EXPERTS) for s in range(NS)]
    used, slots = [], []
    slot = jnp.int32(0)
    for e, s in pairs:
        u = s * SUB < cnt_ref[b * N_EXPERTS + e]
        used.append(u)
        slots.append(slot)
        slot = slot + u.astype(I32)
    for (e, s), u, k in zip(pairs, used, slots):
        @pl.when(u)
        def _(e=e, s=s, k=k):
            copy(e, s, k).start()

    sel = gt_ref[...] > 0.0
    idx = lax.broadcasted_iota(I32, sel.shape, 0)
    lower = idx == jnp.min(jnp.where(sel, idx, N_EXPERTS), axis=0, keepdims=True)
    before = lax.broadcasted_iota(I32, (TB, TB), 0) < lax.broadcasted_iota(I32, (TB, TB), 1)
    rank = _dot(sel.astype(BF16), before.astype(BF16)).astype(I32)
    jj = lax.broadcasted_iota(I32, (SUB, TB), 0)
    p_ref[...] = jnp.zeros(p_ref.shape, BF16)
    for (e, s), u, k in zip(pairs, used, slots):
        @pl.when(u)
        def _(e=e, s=s, k=k):
            hit = (rank[e:e + 1, :] == jj + s * SUB) & sel[e:e + 1, :]
            rows = pl.ds(pl.multiple_of(k * SUB, SUB), SUB)
            p_ref[rows, 0:TB] = (hit & lower[e:e + 1, :]).astype(BF16)
            p_ref[rows, TB:2 * TB] = (hit & jnp.logical_not(lower[e:e + 1, :])).astype(BF16)
    for (e, s), u, k in zip(pairs, used, slots):
        @pl.when(u)
        def _(e=e, s=s, k=k):
            copy(e, s, k).wait()

    gate = gc_ref[...]
    lane = lax.broadcasted_iota(I32, gate.shape, 1)
    picked = gate > 0.0
    first = lane == jnp.min(jnp.where(picked, lane, LANES), axis=1, keepdims=True)
    g_lower = jnp.sum(jnp.where(first, gate, 0.0), axis=1, keepdims=True)
    g_upper = jnp.sum(jnp.where(first, 0.0, gate), axis=1, keepdims=True)
    tn = (((0,), (0,)), ((), ()))
    z = lax.dot_general(p_ref[...], ybuf_ref[...], tn, preferred_element_type=F32)
    return g_lower * z[:TB] + g_upper * z[TB:]


_COMBINE_SCRATCH = [pltpu.VMEM((SLOTS * SUB, D), BF16),
                    pltpu.VMEM((SLOTS * SUB, 2 * TB), BF16),
                    pltpu.SemaphoreType.DMA((SLOTS,))]


def _combine_kernel(off_ref, cnt_ref, gt_ref, gc_ref, x3_ref, mod_ref, gpost_ref, ys_ref,
                    out_ref, ybuf_ref, p_ref, sem):
    b = pl.program_id(0)
    moe = _gather_expert_rows(b, b == 0, off_ref, cnt_ref, gt_ref, gc_ref, ys_ref, ybuf_ref, p_ref, sem)
    _, _, gt = _split3(mod_ref[0, 0])
    out_ref[...] = x3_ref[...] + gt * (_rms_scale(moe) * gpost_ref[...])


def _combine(off, cnt, gates_t, gates_c, x3, mods_ffn, gpost, ys, t):
    n = x3.shape[0]
    per_b = t // TB
    return pl.pallas_call(
        _combine_kernel,
        grid_spec=pltpu.PrefetchScalarGridSpec(
            num_scalar_prefetch=2,
            grid=(n // TB,),
            in_specs=[pl.BlockSpec((N_EXPERTS, TB), lambda b, o, c: (0, b)),
                      pl.BlockSpec((TB, LANES), lambda b, o, c: (b, 0)),
                      pl.BlockSpec((TB, D), lambda b, o, c: (b, 0)),
                      pl.BlockSpec((1, 1, 1, 3 * D), lambda b, o, c: (1, b // per_b, 0, 0)),
                      pl.BlockSpec((1, D), lambda b, o, c: (0, 0)),
                      pl.BlockSpec(memory_space=pl.ANY)],
            out_specs=pl.BlockSpec((TB, D), lambda b, o, c: (b, 0)),
            scratch_shapes=_COMBINE_SCRATCH),
        out_shape=jax.ShapeDtypeStruct((n, D), F32),
        compiler_params=_cparams(("arbitrary",), 48),
        name="moe_combine",
    )(off, cnt, gates_t, gates_c, x3, mods_ffn, gpost, ys)


def _combine_rows_kernel(off_ref, cnt_ref, gt_ref, gc_ref, ys_ref, out_ref, ybuf_ref, p_ref, sem,
                         *, block0):
    out_ref[...] = _gather_expert_rows(block0, True, off_ref, cnt_ref, gt_ref, gc_ref, ys_ref,
                                       ybuf_ref, p_ref, sem)


def _combine_rows(off, cnt, gates_t, gates_c, ys, block0):
    return pl.pallas_call(
        functools.partial(_combine_rows_kernel, block0=block0),
        grid_spec=pltpu.PrefetchScalarGridSpec(
            num_scalar_prefetch=2,
            grid=(1,),
            in_specs=[pl.BlockSpec((N_EXPERTS, TB), lambda b, o, c: (0, 0)),
                      pl.BlockSpec((TB, LANES), lambda b, o, c: (0, 0)),
                      pl.BlockSpec(memory_space=pl.ANY)],
            out_specs=pl.BlockSpec((TB, D), lambda b, o, c: (0, 0)),
            scratch_shapes=_COMBINE_SCRATCH),
        out_shape=jax.ShapeDtypeStruct((TB, D), F32),
        compiler_params=_cparams(("arbitrary",), 48),
        name="moe_combine_rows",
    )(off, cnt, gates_t, gates_c, ys)


def _s_mix0_kernel(x_ref, st_ref, modm_ref, modf_ref, gpre_ref, gpost_ref, gpref_ref, pw_ref, ps_ref,
                   x1_ref, pool_ref, hf_ref, *, pos):
    sh, sc, gt = _split3(modm_ref[...])
    x = x_ref[...]
    hm = _rms_scale(x) * gpre_ref[...] * (1.0 + sc) + sh
    parts = []
    for g, w in enumerate(POOL_WINDOWS):
        lo = g * PG
        tok = hm[:, lo:lo + PG]
        s = tok
        for j in range(1, w):
            s = s + st_ref[POOL_STATE - j, :, lo:lo + PG]
        cnt = float(min(w, pos + 1))
        parts.append(_dot_hi(s / cnt - tok, pw_ref[g]))
    mix = jnp.concatenate(parts, axis=1) * ps_ref[...]
    x1 = x + gt * (_rms_scale(mix) * gpost_ref[...])
    x1_ref[...] = x1
    for j in range(POOL_STATE - 1):
        pool_ref[j] = st_ref[j + 1]
    pool_ref[POOL_STATE - 1] = hm
    shf, scf, _ = _split3(modf_ref[...])
    hf_ref[...] = _rms_scale(x1) * gpref_ref[...] * (1.0 + scf) + shf


def _s_mix0(x_s, state_t, modm, modf, gpre, gpost, gpref, pw, ps, pos):
    rows = x_s.shape[0]
    return pl.pallas_call(
        functools.partial(_s_mix0_kernel, pos=pos),
        out_shape=[jax.ShapeDtypeStruct((rows, D), F32),
                   jax.ShapeDtypeStruct((POOL_STATE, rows, D), F32),
                   jax.ShapeDtypeStruct((rows, D), F32)],
        compiler_params=pltpu.CompilerParams(vmem_limit_bytes=40 * MIB),
        name="s_mix0",
    )(x_s, state_t, modm, modf, gpre, gpost, gpref, pw, ps)


def _s_ffn_kernel(h_ref, w1_ref, w3_ref, w2_ref, f_ref):
    j = pl.program_id(0)
    h = h_ref[...]
    contrib = _dot_hi(_silu(_dot_hi(h, w1_ref[...])) * _dot_hi(h, w3_ref[...]), w2_ref[...])

    @pl.when(j == 0)
    def _():
        f_ref[...] = contrib

    @pl.when(j > 0)
    def _():
        f_ref[...] += contrib


def _s_ffn(h, w1, w3, w2):
    rows = h.shape[0]
    dff = w1.shape[1]
    fc = 256
    return pl.pallas_call(
        _s_ffn_kernel,
        grid=(dff // fc,),
        in_specs=[pl.BlockSpec((rows, D), lambda j: (0, 0)),
                  pl.BlockSpec((D, fc), lambda j: (0, j)),
                  pl.BlockSpec((D, fc), lambda j: (0, j)),
                  pl.BlockSpec((fc, D), lambda j: (j, 0))],
        out_specs=pl.BlockSpec((rows, D), lambda j: (0, 0)),
        out_shape=jax.ShapeDtypeStruct((rows, D), F32),
        compiler_params=_cparams(("arbitrary",), 32),
        name="s_ffn0",
    )(h, w1, w3, w2)


def _s_post0_kernel(x1_ref, f_ref, modf_ref, modm_ref, gpost_ref, gkv_ref, gpre1_ref,
                    x2_ref, kvin_ref, hm1_ref):
    _, _, gt = _split3(modf_ref[...])
    x2 = x1_ref[...] + gt * (_rms_scale(f_ref[...]) * gpost_ref[...])
    x2_ref[...] = x2
    xn = _rms_scale(x2)
    kvin_ref[...] = xn * gkv_ref[...]
    sh1, sc1, _ = _split3(modm_ref[...])
    hm1_ref[...] = xn * gpre1_ref[...] * (1.0 + sc1) + sh1


def _s_post0(x1, f, modf, modm, gpost, gkv, gpre1):
    rows = x1.shape[0]
    return pl.pallas_call(
        _s_post0_kernel,
        out_shape=[jax.ShapeDtypeStruct((rows, D), F32)] * 3,
        name="s_post0",
    )(x1, f, modf, modm, gpost, gkv, gpre1)


def _s_linear_kernel(h_ref, w_ref, cos_ref, sin_ref, o_ref):
    o_ref[...] = _rope_cols(_dot_hi(h_ref[...], w_ref[...]), cos_ref[...], sin_ref[...])


def _s_linear_plain_kernel(h_ref, w_ref, o_ref):
    o_ref[...] = _dot_hi(h_ref[...], w_ref[...])


def _s_linear(h, w, rope=None):
    rows, k = h.shape
    nw = w.shape[1]
    cw = 512
    in_specs = [pl.BlockSpec((rows, k), lambda j: (0, 0)), pl.BlockSpec((k, cw), lambda j: (0, j))]
    args = [h, w]
    body = _s_linear_plain_kernel
    if rope is not None:
        body = _s_linear_kernel
        in_specs += [pl.BlockSpec((1, cw), lambda j: (0, j)), pl.BlockSpec((1, cw), lambda j: (0, j))]
        args += [rope[0], rope[1]]
    return pl.pallas_call(
        body,
        grid=(nw // cw,),
        in_specs=in_specs,
        out_specs=pl.BlockSpec((rows, cw), lambda j: (0, j)),
        out_shape=jax.ShapeDtypeStruct((rows, nw), F32),
        compiler_params=_cparams(("arbitrary",), 32),
        name="s_linear",
    )(*args)


def _s_attn_kernel(qkv_ref, kc_ref, vc_ref, o_ref):
    n_past = kc_ref.shape[3]
    hb = kc_ref.shape[1]
    ng = len(ATTN_GROUPS)
    nt = (((1,), (1,)), ((), ()))
    eye = (lax.broadcasted_iota(I32, (LANES, LANES), 0) == lax.broadcasted_iota(I32, (LANES, LANES), 1)).astype(F32)
    tiles = jnp.concatenate([qkv_ref[0].reshape(hb * 8, HD), jnp.zeros((LANES - hb * 8, HD), F32)], axis=0)
    cols = lax.dot_general(eye[:HD, :HD], tiles, nt, precision=HIGHEST, preferred_element_type=F32)
    col = lambda h, j: cols[:, 8 * h + j:8 * h + j + 1]
    outs, lses = [], []
    for g, (win, dil) in enumerate(ATTN_GROUPS):
        lo = n_past - win
        dist = win - lax.broadcasted_iota(I32, (1, win), 1)
        s_c = jnp.concatenate([jnp.sum(kc_ref[0, h, :, lo:] * col(h, g), axis=0, keepdims=True)
                               for h in range(hb)], axis=0) * SCALE
        s_c = jnp.where((dist & (dil - 1)) == 0, s_c, -jnp.inf)
        s_n = jnp.concatenate([jnp.sum(col(h, ng) * col(h, g), axis=0, keepdims=True)
                               for h in range(hb)], axis=0) * SCALE
        m = jnp.maximum(jnp.max(s_c, axis=1, keepdims=True), s_n)
        p_c = jnp.exp(s_c - m)
        p_n = jnp.exp(s_n - m)
        den = jnp.sum(p_c, axis=1, keepdims=True) + p_n
        outs.append([(jnp.sum(vc_ref[0, h, :, lo:] * p_c[h:h + 1], axis=1, keepdims=True)
                      + p_n[h:h + 1] * col(h, ng + 1)) / den[h:h + 1] for h in range(hb)])
        lses.append(m + jnp.log(den))
    mx = jnp.maximum(jnp.maximum(lses[0], lses[1]), lses[2])
    es = [jnp.exp(l - mx) for l in lses]
    tot = es[0] + es[1] + es[2]
    lane = lax.broadcasted_iota(I32, (1, LANES), 1)
    out_cols = jnp.zeros((HD, LANES), F32)
    for h in range(hb):
        out = (es[0] / tot)[h:h + 1] * outs[0][h]
        for g in range(1, ng):
            out = out + (es[g] / tot)[h:h + 1] * outs[g][h]
        out_cols = out_cols + out * (lane == h).astype(F32)
    o_ref[0] = lax.dot_general(eye[:hb], out_cols, nt, precision=HIGHEST, preferred_element_type=F32)


def _s_attn(qkv, cache_kt, cache_vt):
    rows, _, _, n_past = cache_kt.shape
    for win, dil in ATTN_GROUPS:
        assert n_past >= win and win % LANES == 0 and dil & (dil - 1) == 0
    cache = pl.BlockSpec((1, HB_S, HD, n_past), lambda b, h: (b, h, 0, 0))
    return pl.pallas_call(
        _s_attn_kernel,
        grid=(rows, NH // HB_S),
        in_specs=[pl.BlockSpec((1, HB_S, 8, HD), lambda b, h: (b, h, 0, 0)), cache, cache],
        out_specs=pl.BlockSpec((1, HB_S, HD), lambda b, h: (b, h, 0)),
        out_shape=jax.ShapeDtypeStruct((rows, NH, HD), F32),
        compiler_params=_cparams(("arbitrary", "arbitrary"), 40),
        name="s_attn",
    )(qkv, cache_kt, cache_vt)


def _s_post1_kernel(x2_ref, mix_ref, modm_ref, modf_ref, gpost_ref, gpre_ref, rwt_ref, rb_ref,
                    x3_ref, hf_ref, gc_ref):
    _, _, gt = _split3(modm_ref[...])
    x3 = x2_ref[...] + gt * (_rms_scale(mix_ref[...]) * gpost_ref[...])
    x3_ref[...] = x3
    sh, sc, _ = _split3(modf_ref[...])
    hf = _rms_scale(x3) * gpre_ref[...] * (1.0 + sc) + sh
    hf_ref[...] = hf
    logits_t = lax.dot_general(rwt_ref[...], hf, (((1,), (1,)), ((), ())),
                               precision=HIGHEST, preferred_element_type=F32) + rb_ref[...]
    gc_ref[...] = _router_top2(logits_t)


def _s_post1(x2, mix, modm, modf, gpost, gpre, rwt, rb):
    rows = x2.shape[0]
    return pl.pallas_call(
        _s_post1_kernel,
        out_shape=[jax.ShapeDtypeStruct((rows, D), F32), jax.ShapeDtypeStruct((rows, D), F32),
                   jax.ShapeDtypeStruct((N_EXPERTS, rows), F32)],
        name="s_post1",
    )(x2, mix, modm, modf, gpost, gpre, rwt, rb)


def _s_post2_kernel(x3_ref, f_ref, modf_ref, gpost_ref, y_ref):
    _, _, gt = _split3(modf_ref[...])
    y_ref[...] = x3_ref[...] + gt * (_rms_scale(f_ref[...]) * gpost_ref[...])


def _s_post2(x3, f, modf, gpost):
    return pl.pallas_call(
        _s_post2_kernel,
        out_shape=jax.ShapeDtypeStruct(x3.shape, F32),
        name="s_post2",
    )(x3, f, modf, gpost)


def _rope_tables(pos):
    half = HD // 2
    inv = ROPE_THETA ** (-jnp.arange(half, dtype=F32) / half)
    ang = pos.astype(F32)[:, None] * inv[None, :]
    cos, sin = jnp.cos(ang), jnp.sin(ang)
    cos = jnp.tile(cos, (1, LANES // half))
    sin = jnp.tile(jnp.concatenate([-sin, sin], axis=1), (1, LANES // HD))
    return cos, sin


def kernel(x_prompt, x_sample, state_pool, cache_k, cache_v, c_prompt, c_sample, ada_mix_w, ada_mix_b, g_pre_mix, g_post_mix, pool_w, pool_scale, g_kv, w_kv, w_q, w_o, ada_ffn_w, ada_ffn_b, g_pre_ffn, g_post_ffn, ffn_w1, ffn_w3, ffn_w2, router_w, router_b, moe_w1, moe_w3, moe_w2):
    b, t, _ = x_prompt.shape
    nsmp = x_sample.shape[0]
    n = b * t
    past_len = 16384
    assert x_sample.shape[1] == 1 and g_pre_mix.shape[0] == 2 and t % (ATTN_GROUPS[-1][1] * BLK) == 0
    vec = lambda a: a.reshape(1, D)

    c_all = jnp.concatenate([c_prompt, c_sample], axis=0)
    mods_mix = _ada(c_all, ada_mix_w, ada_mix_b)
    mods_ffn = _ada(c_all, ada_ffn_w, ada_ffn_b)
    pm_mix = mods_mix[:, :b].reshape(2, b, 1, 3 * D)
    pm_ffn = mods_ffn[:, :b].reshape(2, b, 1, 3 * D)
    sm_mix = mods_mix[:, b:]
    sm_ffn = mods_ffn[:, b:]

    rwt = router_w[0].T
    rb = router_b[0].reshape(N_EXPERTS, 1)

    x1, pool16 = _mix0(x_prompt, pm_mix, vec(g_pre_mix[0]), vec(g_post_mix[0]),
                       pool_w[0].astype(BF16), vec(pool_scale[0]))
    cos_p, sin_p = _rope_tables(jnp.arange(t, dtype=jnp.int32))
    x2 = _ffn0(x1.reshape(n, D), pm_ffn, vec(g_pre_ffn[0]), vec(g_post_ffn[0]),
               ffn_w1[0].astype(BF16), ffn_w3[0].astype(BF16), ffn_w2[0].astype(BF16), t)
    kt_p, vt_p, *kv_l = _kvproj(x2, vec(g_kv), cos_p, sin_p, w_kv.astype(BF16), b, t)
    q_l = _qproj(x2, pm_mix, vec(g_pre_mix[1]), cos_p, sin_p, w_q[0].astype(BF16), b, t)
    attn = _attention(q_l, kv_l[:3], kv_l[3:], b, t)
    x3, hf1, gates_t, gates_c = _post1(attn, x2, pm_mix, pm_ffn, vec(g_post_mix[1]),
                                       vec(g_pre_ffn[1]), w_o[0].astype(BF16), rwt, rb, t)

    state_t = jnp.transpose(state_pool[:, 0], (1, 0, 2))
    x1s, pool_t, hf0s = _s_mix0(x_sample.reshape(nsmp, D), state_t, sm_mix[0], sm_ffn[0],
                                vec(g_pre_mix[0]), vec(g_post_mix[0]), vec(g_pre_ffn[0]),
                                pool_w[0], vec(pool_scale[0]), past_len)
    f0s = _s_ffn(hf0s, ffn_w1[0], ffn_w3[0], ffn_w2[0])
    x2s, kvin_s, hm1s = _s_post0(x1s, f0s, sm_ffn[0], sm_mix[1], vec(g_post_ffn[0]), vec(g_kv), vec(g_pre_mix[1]))
    cos_s, sin_s = _rope_tables(jnp.full((1,), past_len, jnp.int32))
    ones, zeros = jnp.ones((1, D), F32), jnp.zeros((1, D), F32)
    kv_s = _s_linear(kvin_s, w_kv, (jnp.concatenate([jnp.tile(cos_s, (1, D // LANES)), ones], axis=1),
                                    jnp.concatenate([jnp.tile(sin_s, (1, D // LANES)), zeros], axis=1)))
    q_s = _s_linear(hm1s, w_q[0], (jnp.tile(cos_s, (1, 3 * D // LANES)), jnp.tile(sin_s, (1, 3 * D // LANES))))
    k_s = kv_s[:, :D].reshape(nsmp, NH, HD)
    v_s = kv_s[:, D:].reshape(nsmp, NH, HD)
    cache_kt = jnp.transpose(cache_k, (0, 2, 3, 1))
    cache_vt = jnp.transpose(cache_v, (0, 2, 3, 1))
    qkv = jnp.concatenate([jnp.transpose(q_s.reshape(nsmp, len(ATTN_GROUPS), NH, HD), (0, 2, 1, 3)),
                           k_s[:, :, None], v_s[:, :, None]], axis=2)
    qkv = jnp.pad(qkv, ((0, 0), (0, 0), (0, 8 - qkv.shape[2]), (0, 0)))
    attn_s = _s_attn(qkv, cache_kt, cache_vt).reshape(nsmp, D)
    mix1s = _s_linear(attn_s, w_o[0])
    x3s, hf1s, gts = _s_post1(x2s, mix1s, sm_mix[1], sm_ffn[1], vec(g_post_mix[1]), vec(g_pre_ffn[1]), rwt, rb)

    hf_blk = jnp.zeros((TB, D), BF16).at[:nsmp].set(hf1s.astype(BF16))
    gt_blk = jnp.zeros((N_EXPERTS, TB), F32).at[:, :nsmp].set(gts)
    gc_blk = jnp.zeros((TB, LANES), F32).at[:nsmp, :N_EXPERTS].set(gts.T)
    off, cnt, tile_e, n_used = _route_plan(jnp.concatenate([gates_t, gt_blk], axis=1), n + TB)
    xs = jnp.zeros((_max_sorted_rows(n + TB), D), BF16)
    xs = _dispatch(off, cnt, hf1, gates_t, xs, 0)
    xs = _dispatch(off, cnt, hf_blk, gt_blk, xs, n // TB)
    ys = _experts(tile_e, n_used, xs, moe_w1[0], moe_w3[0], moe_w2[0])
    y_prompt = _combine(off, cnt, gates_t, gates_c, x3, pm_ffn, vec(g_post_ffn[1]), ys, t).reshape(b, t, D)
    moe_s = _combine_rows(off, cnt, gt_blk, gc_blk, ys, n // TB)[:nsmp]
    y_sample = _s_post2(x3s, moe_s, sm_ffn[1], vec(g_post_ffn[1])).reshape(nsmp, 1, D)

    pool_prompt = pool16[:, None, 1:, :]
    pool_sample = jnp.transpose(pool_t, (1, 0, 2))[:, None]
    return (y_prompt, y_sample, pool_prompt, pool_sample,
            jnp.transpose(kt_p, (0, 3, 1, 2)), jnp.transpose(vt_p, (0, 3, 1, 2)),
            k_s[:, None], v_s[:, None])
```

```python
import functools

import jax
import jax.numpy as jnp
import numpy as np
from jax import lax
from jax.experimental import pallas as pl
from jax.experimental.pallas import tpu as pltpu

F32, BF16, I32 = jnp.float32, jnp.bfloat16, jnp.int32
HIGHEST = lax.Precision.HIGHEST

D = 1024
POOL_WINDOWS = (2, 4, 8, 16)
PG = D // len(POOL_WINDOWS)
POOL_STATE = max(POOL_WINDOWS) - 1
POOL_ROWS = POOL_STATE + 1
HALO = 2 * POOL_ROWS
ATTN_GROUPS = ((128, 1), (512, 4), (2048, 16))
NH = 16
HD = 64
BLK = 128
ROPE_THETA = 10000.0
N_EXPERTS = 8
EPS = 1e-6
SCALE = HD ** -0.5
LANES = 128
MIB = 1024 * 1024

TT_MIX = 512
TM_FFN = 1024
HB_S = 8
ATTN_UNITS = 8
FC_FFN = 256
TM_POST = 512
TM_WO = 1024
TB = 512
SUB = 128
NS = TB // SUB
TMX = 1024
FCX = 896
ROW_ALIGN = 16


def _cparams(sem, vmem_mib):
    return pltpu.CompilerParams(dimension_semantics=sem, vmem_limit_bytes=vmem_mib * MIB)


def _rms_scale(x):
    return x * lax.rsqrt(jnp.mean(x * x, axis=-1, keepdims=True) + EPS)


def _split3(m):
    return m[:, :D], m[:, D:2 * D], m[:, 2 * D:]


def _silu(x):
    return x * jax.nn.sigmoid(x)


def _rope_cols(x, cos, sin):
    lane = lax.broadcasted_iota(I32, (x.shape[0], LANES), 1)
    first = (lane % HD) < (HD // 2)
    shared = cos.shape[1] == LANES
    outs = []
    for c in range(x.shape[1] // LANES):
        cols = slice(c * LANES, (c + 1) * LANES)
        xc = x[:, cols]
        swapped = jnp.where(first, pltpu.roll(xc, LANES - HD // 2, 1), pltpu.roll(xc, HD // 2, 1))
        outs.append(xc * (cos if shared else cos[:, cols]) + swapped * (sin if shared else sin[:, cols]))
    return jnp.concatenate(outs, axis=1)


def _dot_hi(a, b):
    return jnp.dot(a, b, precision=HIGHEST, preferred_element_type=F32)


def _dot(a, b):
    return jnp.dot(a, b, preferred_element_type=F32)


def _ada_kernel(c_ref, w_ref, b_ref, o_ref):
    o_ref[0] = _dot_hi(_silu(c_ref[...]), w_ref[0]) + b_ref[0]


def _ada(c_all, w, b):
    nl, _, n3 = w.shape
    rows = c_all.shape[0]
    cw = 768
    return pl.pallas_call(
        _ada_kernel,
        grid=(nl, n3 // cw),
        in_specs=[pl.BlockSpec((rows, D), lambda l, j: (0, 0)),
                  pl.BlockSpec((1, D, cw), lambda l, j: (l, 0, j)),
                  pl.BlockSpec((1, 1, cw), lambda l, j: (l, 0, j))],
        out_specs=pl.BlockSpec((1, rows, cw), lambda l, j: (l, 0, j)),
        out_shape=jax.ShapeDtypeStruct((nl, rows, n3), F32),
        compiler_params=_cparams(("arbitrary", "arbitrary"), 32),
        name="ada_mod",
    )(c_all, w, b.reshape(nl, 1, n3))


def _mix0_kernel(x_ref, xh_ref, mod_ref, gpre_ref, gpost_ref, pw_ref, ps_ref,
                 x1_ref, pool_ref, ext_ref, lv_ref):
    i = pl.program_id(1)
    tt = x_ref.shape[1]
    sh, sc, gt = _split3(mod_ref[0, 0])
    gpre = gpre_ref[...]
    x = x_ref[0]
    hm = _rms_scale(x) * gpre * (1.0 + sc) + sh
    hh = _rms_scale(xh_ref[0]) * gpre * (1.0 + sc) + sh
    ext_ref[0:HALO, :] = jnp.where(i > 0, hh, 0.0)
    ext_ref[HALO:, :] = hm
    t = i * tt + lax.broadcasted_iota(I32, (tt, 1), 0)
    rows = tt + HALO
    parts = []
    for g, w in enumerate(POOL_WINDOWS):
        lo = g * PG
        tok = hm[:, lo:lo + PG]
        levels = w.bit_length() - 1
        assert w == 1 << levels
        first = [HALO]
        for k in range(levels - 1, 0, -1):
            first.insert(0, (first[0] - (1 << k)) // 8 * 8)
        for k in range(1, levels + 1):
            r0, back = first[k - 1], 1 << (k - 1)
            if k == 1:
                s = ext_ref[r0:rows, lo:lo + PG] + ext_ref[r0 - back:rows - back, lo:lo + PG]
            else:
                prev = lv_ref.at[k % 2]
                s = prev[r0:rows, :] + prev[r0 - back:rows - back, :]
            if k < levels:
                lv_ref[(k + 1) % 2, r0:rows, :] = s
        inv_cnt = 1.0 / jnp.minimum(w, t + 1).astype(F32)
        parts.append(_dot((s * inv_cnt - tok).astype(BF16), pw_ref[g]))
    mix = jnp.concatenate(parts, axis=1) * ps_ref[...]
    x1_ref[0] = x + gt * (_rms_scale(mix) * gpost_ref[...])

    @pl.when(i == pl.num_programs(1) - 1)
    def _():
        pool_ref[0] = ext_ref[rows - POOL_ROWS:rows, :]


def _mix0(x, mods, gpre, gpost, pw_bf, ps):
    b, t, _ = x.shape
    tt = TT_MIX
    hb = tt // HALO
    return pl.pallas_call(
        _mix0_kernel,
        grid=(b, t // tt),
        in_specs=[pl.BlockSpec((1, tt, D), lambda bb, i: (bb, i, 0)),
                  pl.BlockSpec((1, HALO, D), lambda bb, i: (bb, jnp.maximum(i * hb - 1, 0), 0)),
                  pl.BlockSpec((1, 1, 1, 3 * D), lambda bb, i: (0, bb, 0, 0)),
                  pl.BlockSpec((1, D), lambda bb, i: (0, 0)),
                  pl.BlockSpec((1, D), lambda bb, i: (0, 0)),
                  pl.BlockSpec((len(POOL_WINDOWS), PG, PG), lambda bb, i: (0, 0, 0)),
                  pl.BlockSpec((1, D), lambda bb, i: (0, 0))],
        out_specs=[pl.BlockSpec((1, tt, D), lambda bb, i: (bb, i, 0)),
                   pl.BlockSpec((1, POOL_ROWS, D), lambda bb, i: (bb, 0, 0))],
        out_shape=[jax.ShapeDtypeStruct((b, t, D), F32),
                   jax.ShapeDtypeStruct((b, POOL_ROWS, D), F32)],
        scratch_shapes=[pltpu.VMEM((tt + HALO, D), F32), pltpu.VMEM((2, tt + HALO, PG), F32)],
        compiler_params=_cparams(("arbitrary", "arbitrary"), 40),
        name="mix0",
    )(x, x, mods, gpre, gpost, pw_bf, ps)


def _store_dilated(val, plane_ref, dils, out_refs):
    tm = val.shape[0]
    for hp in range(NH // 2):
        plane_ref[hp] = val[:, hp * LANES:(hp + 1) * LANES]
    for dil, out_ref in zip(dils, out_refs):
        for hp in range(NH // 2):
            for r in range(dil):
                out_ref[0, hp, r] = plane_ref[hp, pl.ds(r, tm // dil, stride=dil), :].astype(BF16)


def _dilated_specs(b, t, tm):
    per_b = t // tm
    specs = [pl.BlockSpec((1, NH // 2, dil, tm // dil, LANES), lambda i: (i // per_b, 0, 0, i % per_b, 0))
             for _, dil in ATTN_GROUPS]
    shapes = [jax.ShapeDtypeStruct((b, NH // 2, dil, t // dil, LANES), BF16) for _, dil in ATTN_GROUPS]
    return specs, shapes


def _ffn0_kernel(x1_ref, modf_ref, gpre_ref, gpost_ref, w1_ref, w3_ref, w2_ref, x2_ref, acc_ref):
    sh, sc, gt = _split3(modf_ref[0, 0])
    x1 = x1_ref[...]
    hf = (_rms_scale(x1) * gpre_ref[...] * (1.0 + sc) + sh).astype(BF16)
    for c in range(w1_ref.shape[1] // FC_FFN):
        cs = slice(c * FC_FFN, (c + 1) * FC_FFN)
        a = (_silu(_dot(hf, w1_ref[:, cs])) * _dot(hf, w3_ref[:, cs])).astype(BF16)
        contrib = _dot(a, w2_ref[cs, :])
        if c == 0:
            acc_ref[...] = contrib
        else:
            acc_ref[...] += contrib
    x2 = x1 + gt * (_rms_scale(acc_ref[...]) * gpost_ref[...])
    x2_ref[...] = x2


def _ffn0(x1, mods_ffn, gpre, gpost, w1, w3, w2, t):
    n = x1.shape[0]
    tm = TM_FFN
    per_b = t // tm
    dff = w1.shape[1]
    row = lambda i: (i, 0)
    fixed = lambda i: (0, 0)
    resident = lambda shape: pl.BlockSpec(shape, fixed, pipeline_mode=pl.Buffered(1))
    vec = pl.BlockSpec((1, D), fixed)
    return pl.pallas_call(
        _ffn0_kernel,
        grid=(n // tm,),
        in_specs=[pl.BlockSpec((tm, D), row),
                  pl.BlockSpec((1, 1, 1, 3 * D), lambda i: (0, i // per_b, 0, 0)),
                  vec, vec,
                  resident((D, dff)), resident((D, dff)), resident((dff, D))],
        out_specs=pl.BlockSpec((tm, D), row),
        out_shape=jax.ShapeDtypeStruct((n, D), F32),
        scratch_shapes=[pltpu.VMEM((tm, D), F32)],
        compiler_params=_cparams(("arbitrary",), 56),
        name="ffn0",
    )(x1, mods_ffn, gpre, gpost, w1, w3, w2)


def _kvproj_kernel(x2_ref, gkv_ref, cos_ref, sin_ref, wkv_ref,
                   kt_ref, vt_ref, k0_ref, k1_ref, k2_ref, v0_ref, v1_ref, v2_ref, plane_ref):
    tm = x2_ref.shape[0]
    kv = _dot((_rms_scale(x2_ref[...]) * gkv_ref[...]).astype(BF16), wkv_ref[...])
    k = _rope_cols(kv[:, :D], cos_ref[...], sin_ref[...])
    v = kv[:, D:]
    kt_ref[0] = k.T.reshape(NH, HD, tm)
    vt_ref[0] = v.T.reshape(NH, HD, tm)
    dils = [dil for _, dil in ATTN_GROUPS]
    _store_dilated(k, plane_ref, dils, (k0_ref, k1_ref, k2_ref))
    _store_dilated(v, plane_ref, dils, (v0_ref, v1_ref, v2_ref))


def _kvproj(x2, gkv, cos, sin, wkv, b, t):
    n = x2.shape[0]
    tm = TM_POST
    per_b = t // tm
    fixed = lambda i: (0, 0)
    dspecs, dshapes = _dilated_specs(b, t, tm)
    tspec = pl.BlockSpec((1, NH, HD, tm), lambda i: (i // per_b, 0, 0, i % per_b))
    tshape = jax.ShapeDtypeStruct((b, NH, HD, t), F32)
    return pl.pallas_call(
        _kvproj_kernel,
        grid=(n // tm,),
        in_specs=[pl.BlockSpec((tm, D), lambda i: (i, 0)),
                  pl.BlockSpec((1, D), fixed),
                  pl.BlockSpec((tm, LANES), lambda i: (i % per_b, 0)),
                  pl.BlockSpec((tm, LANES), lambda i: (i % per_b, 0)),
                  pl.BlockSpec((D, 2 * D), fixed, pipeline_mode=pl.Buffered(1))],
        out_specs=[tspec, tspec] + dspecs + dspecs,
        out_shape=[tshape, tshape] + dshapes + dshapes,
        scratch_shapes=[pltpu.VMEM((NH // 2, tm, LANES), F32)],
        compiler_params=_cparams(("arbitrary",), 48),
        name="kv_proj",
    )(x2, gkv, cos, sin, wkv)


def _qproj_kernel(x2_ref, modm_ref, gpre1_ref, cos_ref, sin_ref, wq_ref, q0_ref, q1_ref, q2_ref, scr_ref):
    sh1, sc1, _ = _split3(modm_ref[0, 0])
    hm1 = (_rms_scale(x2_ref[...]) * gpre1_ref[...] * (1.0 + sc1) + sh1).astype(BF16)
    q = _rope_cols(_dot(hm1, wq_ref[...]), cos_ref[...], sin_ref[...]) * SCALE
    for g, ((_, dil), q_ref) in enumerate(zip(ATTN_GROUPS, (q0_ref, q1_ref, q2_ref))):
        _store_dilated(q[:, g * D:(g + 1) * D], scr_ref, (dil,), (q_ref,))


def _qproj(x2, mods_mix, gpre1, cos, sin, wq, b, t):
    n = x2.shape[0]
    tm = TM_POST
    per_b = t // tm
    fixed = lambda i: (0, 0)
    dspecs, dshapes = _dilated_specs(b, t, tm)
    return pl.pallas_call(
        _qproj_kernel,
        grid=(n // tm,),
        in_specs=[pl.BlockSpec((tm, D), lambda i: (i, 0)),
                  pl.BlockSpec((1, 1, 1, 3 * D), lambda i: (1, i // per_b, 0, 0)),
                  pl.BlockSpec((1, D), fixed),
                  pl.BlockSpec((tm, LANES), lambda i: (i % per_b, 0)),
                  pl.BlockSpec((tm, LANES), lambda i: (i % per_b, 0)),
                  pl.BlockSpec((D, 3 * D), fixed, pipeline_mode=pl.Buffered(1))],
        out_specs=dspecs,
        out_shape=dshapes,
        scratch_shapes=[pltpu.VMEM((NH // 2, tm, LANES), F32)],
        compiler_params=_cparams(("arbitrary",), 48),
        name="q_proj",
    )(x2, mods_mix, gpre1, cos, sin, wq)


def _attn_kernel(q0_ref, q1_ref, q2_ref, k0_ref, k1_ref, k2_ref, v0_ref, v1_ref, v2_ref,
                 o_ref, og_ref, lg_ref, dg_ref, bias_ref):
    t = o_ref.shape[2]
    first_head = lax.broadcasted_iota(I32, (BLK, LANES), 1) < HD
    nt = (((1,), (1,)), ((), ()))

    @pl.when((pl.program_id(0) == 0) & (pl.program_id(1) == 0))
    def _():
        qi = lax.broadcasted_iota(I32, (2 * BLK, 2 * BLK), 0) % BLK
        kj = lax.broadcasted_iota(I32, (2 * BLK, 2 * BLK), 1)
        for v in range(2):
            dist = v * BLK + qi - kj
            bias_ref[v] = jnp.where((dist >= 0) & (dist <= BLK), 0.0, -jnp.inf)

    groups = zip(ATTN_GROUPS, (q0_ref, q1_ref, q2_ref), (k0_ref, k1_ref, k2_ref), (v0_ref, v1_ref, v2_ref))
    for g, ((win, dil), q_ref, k_ref, v_ref) in enumerate(groups):
        assert win // dil == BLK
        sub = t // dil
        nb = sub // BLK
        kw = min(2 * BLK, sub)

        def scores(u, nb=nb, kw=kw, q_ref=q_ref, k_ref=k_ref):
            r = u // nb
            n = u % nb
            q_lo = pl.multiple_of(n * BLK, BLK)
            k_lo = pl.multiple_of(jnp.maximum(n - 1, 0) * BLK, BLK)
            qp = q_ref[0, 0, r, pl.ds(q_lo, BLK), :]
            zero = jnp.zeros_like(qp)
            qs = jnp.concatenate([jnp.where(first_head, qp, zero), jnp.where(first_head, zero, qp)], axis=0)
            s = lax.dot_general(qs, k_ref[0, 0, r, pl.ds(k_lo, kw), :], nt, preferred_element_type=F32)
            return s + bias_ref[jnp.minimum(n, 1), :, 0:kw], r, q_lo, k_lo

        def finish(s, r, q_lo, k_lo, dil=dil, kw=kw, v_ref=v_ref, g=g):
            m = jnp.max(s, axis=-1, keepdims=True)
            p = jnp.exp(s - m)
            den = jnp.sum(p, axis=-1, keepdims=True)
            pv = _dot(p.astype(BF16), v_ref[0, 0, r, pl.ds(k_lo, kw), :])
            m2 = jnp.broadcast_to(m, (2 * BLK, LANES))
            d2 = jnp.broadcast_to(den, (2 * BLK, LANES))
            rows = pl.ds(q_lo * dil + r, BLK, stride=dil) if dil > 1 else pl.ds(q_lo, BLK)
            og_ref[g, rows, :] = jnp.where(first_head, pv[:BLK], pv[BLK:])
            lg_ref[g, rows, :] = jnp.where(first_head, m2[:BLK], m2[BLK:])
            dg_ref[g, rows, :] = jnp.where(first_head, d2[:BLK], d2[BLK:])

        def trip(i, carry, scores=scores, finish=finish):
            started = [scores(i * ATTN_UNITS + j) for j in range(ATTN_UNITS)]
            for args in started:
                finish(*args)
            return carry

        lax.fori_loop(0, dil * nb // ATTN_UNITS, trip, 0)

    mt = 2 * BLK

    def merge(c, carry):
        rows = pl.ds(pl.multiple_of(c * mt, mt), mt)
        ms = [lg_ref[g, rows, :] for g in range(len(ATTN_GROUPS))]
        mx = jnp.maximum(jnp.maximum(ms[0], ms[1]), ms[2])
        ws = [jnp.exp(m - mx) for m in ms]
        num = ws[0] * og_ref[0, rows, :]
        den = ws[0] * dg_ref[0, rows, :]
        for g in range(1, len(ATTN_GROUPS)):
            num = num + ws[g] * og_ref[g, rows, :]
            den = den + ws[g] * dg_ref[g, rows, :]
        o_ref[0, 0, rows, :] = (num / den).astype(BF16)
        return carry

    lax.fori_loop(0, t // mt, merge, 0)


def _attention(q_l, k_l, v_l, b, t):
    specs = [pl.BlockSpec((1, 1, dil, t // dil, LANES), lambda bb, hp: (bb, hp, 0, 0, 0))
             for _, dil in ATTN_GROUPS]
    return pl.pallas_call(
        _attn_kernel,
        grid=(b, NH // 2),
        in_specs=specs * 3,
        out_specs=pl.BlockSpec((1, 1, t, LANES), lambda bb, hp: (bb, hp, 0, 0)),
        out_shape=jax.ShapeDtypeStruct((b, NH // 2, t, LANES), BF16),
        scratch_shapes=[pltpu.VMEM((len(ATTN_GROUPS), t, LANES), F32)] * 3
        + [pltpu.VMEM((2, 2 * BLK, 2 * BLK), F32)],
        compiler_params=_cparams(("arbitrary", "arbitrary"), 40),
        name="dilated_attn",
    )(*q_l, *k_l, *v_l)


def _router_top2(logits_t):
    idx = lax.broadcasted_iota(I32, logits_t.shape, 0)
    v1 = jnp.max(logits_t, axis=0, keepdims=True)
    i1 = jnp.min(jnp.where(logits_t == v1, idx, N_EXPERTS), axis=0, keepdims=True)
    m1 = idx == i1
    rest = jnp.where(m1, -jnp.inf, logits_t)
    v2 = jnp.max(rest, axis=0, keepdims=True)
    i2 = jnp.min(jnp.where(rest == v2, idx, N_EXPERTS), axis=0, keepdims=True)
    m2 = idx == i2
    e2 = jnp.exp(v2 - v1)
    den = 1.0 + e2
    return jnp.where(m1, 1.0 / den, 0.0) + jnp.where(m2, e2 / den, 0.0)


def _post1_kernel(a_ref, x2_ref, modm_ref, modf_ref, gpost_ref, gpre_ref, wo_ref, rwt_ref, rb_ref,
                  x3_ref, hf_ref, gt_ref, gc_ref):
    o = jnp.concatenate([a_ref[0, hp] for hp in range(NH // 2)], axis=1)
    mix = _dot(o, wo_ref[...])
    _, _, gt = _split3(modm_ref[0, 0])
    x3 = x2_ref[...] + gt * (_rms_scale(mix) * gpost_ref[...])
    x3_ref[...] = x3
    sh, sc, _ = _split3(modf_ref[0, 0])
    hf = _rms_scale(x3) * gpre_ref[...] * (1.0 + sc) + sh
    hf_ref[...] = hf.astype(BF16)
    logits_t = lax.dot_general(rwt_ref[...], hf, (((1,), (1,)), ((), ())),
                               precision=HIGHEST, preferred_element_type=F32) + rb_ref[...]
    gates_t = _router_top2(logits_t)
    gt_ref[...] = gates_t
    padded = jnp.concatenate([gates_t, jnp.zeros((LANES - N_EXPERTS, gates_t.shape[1]), F32)], axis=0)
    gc_ref[...] = padded.T


def _post1(attn, x2, mods_mix, mods_ffn, gpost, gpre, wo_bf, rwt, rb, t):
    n = x2.shape[0]
    tm = TM_WO
    per_b = t // tm
    row = lambda i: (i, 0)
    fixed = lambda i: (0, 0)
    vec = pl.BlockSpec((1, D), fixed)
    tile = pl.BlockSpec((tm, D), row)
    return pl.pallas_call(
        _post1_kernel,
        grid=(n // tm,),
        in_specs=[pl.BlockSpec((1, NH // 2, tm, LANES), lambda i: (i // per_b, 0, i % per_b, 0)),
                  tile,
                  pl.BlockSpec((1, 1, 1, 3 * D), lambda i: (1, i // per_b, 0, 0)),
                  pl.BlockSpec((1, 1, 1, 3 * D), lambda i: (1, i // per_b, 0, 0)),
                  vec, vec,
                  pl.BlockSpec((D, D), fixed),
                  pl.BlockSpec((N_EXPERTS, D), fixed),
                  pl.BlockSpec((N_EXPERTS, 1), fixed)],
        out_specs=[tile, tile, pl.BlockSpec((N_EXPERTS, tm), lambda i: (0, i)),
                   pl.BlockSpec((tm, LANES), row)],
        out_shape=[jax.ShapeDtypeStruct((n, D), F32), jax.ShapeDtypeStruct((n, D), BF16),
                   jax.ShapeDtypeStruct((N_EXPERTS, n), F32), jax.ShapeDtypeStruct((n, LANES), F32)],
        compiler_params=_cparams(("arbitrary",), 52),
        name="wo_router",
    )(attn, x2, mods_mix, mods_ffn, gpost, gpre, wo_bf, rwt, rb)


def _route_plan(gates_t, n):
    nb = n // TB
    cnt = jnp.sum((gates_t > 0.0).reshape(N_EXPERTS, nb, TB), axis=-1).astype(I32)
    c_al = (cnt + ROW_ALIGN - 1) // ROW_ALIGN * ROW_ALIGN
    within = jnp.cumsum(c_al, axis=1) - c_al
    tot = jnp.sum(c_al, axis=1)
    region = (tot + SUB + TMX - 1) // TMX * TMX
    start = jnp.cumsum(region) - region
    off = (start[:, None] + within).T.reshape(-1)
    n_used = (jnp.sum(region) // TMX).astype(I32)
    tile_row = jnp.arange(_max_sorted_rows(n) // TMX, dtype=I32) * TMX
    tile_e = jnp.sum(tile_row[:, None] >= (start + region)[None, :], axis=1).astype(I32)
    tile_e = jnp.minimum(tile_e, N_EXPERTS - 1)
    last_e = tile_e[jnp.maximum(n_used - 1, 0)]
    tile_e = jnp.where(jnp.arange(tile_e.shape[0]) < n_used, tile_e, last_e)
    return off.astype(I32), cnt.T.reshape(-1), tile_e, n_used.reshape(1)


def _max_sorted_rows(n):
    nb = n // TB
    worst = 2 * n + nb * N_EXPERTS * (ROW_ALIGN - 1) + N_EXPERTS * (SUB + TMX - 1)
    return (worst + TMX - 1) // TMX * TMX


def _dispatch_kernel(off_ref, cnt_ref, hf_ref, gt_ref, xs_in_ref, xs_ref, stage_ref, sem, *, block0):
    del xs_in_ref
    b = pl.program_id(0) + block0
    hf = hf_ref[...]
    sel = gt_ref[...] > 0.0
    before = lax.broadcasted_iota(I32, (TB, TB), 0) < lax.broadcasted_iota(I32, (TB, TB), 1)
    rank = _dot(sel.astype(BF16), before.astype(BF16)).astype(I32)
    jj = lax.broadcasted_iota(I32, (SUB, TB), 0)

    def copy(e, s):
        dst = pl.multiple_of(off_ref[b * N_EXPERTS + e] + s * SUB, ROW_ALIGN)
        return pltpu.make_async_copy(stage_ref.at[e * NS + s], xs_ref.at[pl.ds(dst, SUB)], sem.at[e * NS + s])

    for e in range(N_EXPERTS):
        c = cnt_ref[b * N_EXPERTS + e]
        for s in range(NS):
            @pl.when(s * SUB < c)
            def _(e=e, s=s):
                pick = ((rank[e:e + 1, :] == jj + s * SUB) & sel[e:e + 1, :]).astype(BF16)
                stage_ref[e * NS + s] = _dot(pick, hf).astype(BF16)
                copy(e, s).start()
    for e in range(N_EXPERTS):
        c = cnt_ref[b * N_EXPERTS + e]
        for s in range(NS):
            @pl.when(s * SUB < c)
            def _(e=e, s=s):
                copy(e, s).wait()


def _dispatch(off, cnt, hf_bf, gates_t, xs, block0):
    n = hf_bf.shape[0]
    rows = xs.shape[0]
    return pl.pallas_call(
        functools.partial(_dispatch_kernel, block0=block0),
        grid_spec=pltpu.PrefetchScalarGridSpec(
            num_scalar_prefetch=2,
            grid=(n // TB,),
            in_specs=[pl.BlockSpec((TB, D), lambda b, o, c: (b, 0)),
                      pl.BlockSpec((N_EXPERTS, TB), lambda b, o, c: (0, b)),
                      pl.BlockSpec(memory_space=pl.ANY)],
            out_specs=pl.BlockSpec(memory_space=pl.ANY),
            scratch_shapes=[pltpu.VMEM((N_EXPERTS * NS, SUB, D), BF16),
                            pltpu.SemaphoreType.DMA((N_EXPERTS * NS,))]),
        out_shape=jax.ShapeDtypeStruct((rows, D), BF16),
        input_output_aliases={4: 0},
        compiler_params=_cparams(("arbitrary",), 40),
        name="moe_dispatch",
    )(off, cnt, hf_bf, gates_t, xs)


def _expert_kernel(te_ref, nu_ref, x_ref, w1_ref, w3_ref, w2_ref, y_ref, acc_ref):
    del te_ref
    i, j = pl.program_id(0), pl.program_id(1)

    @pl.when(i < nu_ref[0])
    def _():
        x = x_ref[...]
        a = (_silu(_dot(x, w1_ref[0].astype(BF16))) * _dot(x, w3_ref[0].astype(BF16))).astype(BF16)
        contrib = _dot(a, w2_ref[0].astype(BF16))

        @pl.when(j == 0)
        def _():
            acc_ref[...] = contrib

        @pl.when(j > 0)
        def _():
            acc_ref[...] += contrib

        @pl.when(j == pl.num_programs(1) - 1)
        def _():
            y_ref[...] = acc_ref[...].astype(BF16)

    @pl.when((i >= nu_ref[0]) & (j == 0))
    def _():
        y_ref[...] = jnp.zeros(y_ref.shape, BF16)


def _experts(tile_e, n_used, xs, w1, w3, w2):
    rows = xs.shape[0]
    dff = w1.shape[2]
    nj = dff // FCX

    def tile_idx(i, j, te, nu):
        return (jnp.minimum(i, nu[0] - 1), 0)

    def chunk(i, j, nu):
        return jnp.where(i < nu[0], j, nj - 1)

    return pl.pallas_call(
        _expert_kernel,
        grid_spec=pltpu.PrefetchScalarGridSpec(
            num_scalar_prefetch=2,
            grid=(rows // TMX, nj),
            in_specs=[pl.BlockSpec((TMX, D), tile_idx),
                      pl.BlockSpec((1, D, FCX), lambda i, j, te, nu: (te[i], 0, chunk(i, j, nu))),
                      pl.BlockSpec((1, D, FCX), lambda i, j, te, nu: (te[i], 0, chunk(i, j, nu))),
                      pl.BlockSpec((1, FCX, D), lambda i, j, te, nu: (te[i], chunk(i, j, nu), 0))],
            out_specs=pl.BlockSpec((TMX, D), lambda i, j, te, nu: (i, 0)),
            scratch_shapes=[pltpu.VMEM((TMX, D), F32)]),
        out_shape=jax.ShapeDtypeStruct((rows, D), BF16),
        compiler_params=_cparams(("arbitrary", "arbitrary"), 56),
        name="moe_experts",
    )(tile_e, n_used, xs, w1, w3, w2)


SLOTS = 2 * TB // SUB + N_EXPERTS


def _gather_expert_rows(b, first_step, off_ref, cnt_ref, gt_ref, gc_ref, ys_ref, ybuf_ref, p_ref, sem):
    @pl.when(first_step)
    def _():
        ybuf_ref[...] = jnp.zeros(ybuf_ref.shape, BF16)

    def copy(e, s, slot):
        src = pl.multiple_of(off_ref[b * N_EXPERTS + e] + s * SUB, ROW_ALIGN)
        dst = pl.multiple_of(slot * SUB, SUB)
        return pltpu.make_async_copy(ys_ref.at[pl.ds(src, SUB)], ybuf_ref.at[pl.ds(dst, SUB)], sem.at[slot])

    pairs = [(e, s) for e in range(N_EXPERTS) for s in range(NS)]
    used, slots = [], []
    slot = jnp.int32(0)
    for e, s in pairs:
        u = s * SUB < cnt_ref[b * N_EXPERTS + e]
        used.append(u)
        slots.append(slot)
        slot = slot + u.astype(I32)
    for (e, s), u, k in zip(pairs, used, slots):
        @pl.when(u)
        def _(e=e, s=s, k=k):
            copy(e, s, k).start()

    sel = gt_ref[...] > 0.0
    idx = lax.broadcasted_iota(I32, sel.shape, 0)
    lower = idx == jnp.min(jnp.where(sel, idx, N_EXPERTS), axis=0, keepdims=True)
    before = lax.broadcasted_iota(I32, (TB, TB), 0) < lax.broadcasted_iota(I32, (TB, TB), 1)
    rank = _dot(sel.astype(BF16), before.astype(BF16)).astype(I32)
    jj = lax.broadcasted_iota(I32, (SUB, TB), 0)
    p_ref[...] = jnp.zeros(p_ref.shape, BF16)
    for (e, s), u, k in zip(pairs, used, slots):
        @pl.when(u)
        def _(e=e, s=s, k=k):
            hit = (rank[e:e + 1, :] == jj + s * SUB) & sel[e:e + 1, :]
            rows = pl.ds(pl.multiple_of(k * SUB, SUB), SUB)
            p_ref[rows, 0:TB] = (hit & lower[e:e + 1, :]).astype(BF16)
            p_ref[rows, TB:2 * TB] = (hit & jnp.logical_not(lower[e:e + 1, :])).astype(BF16)
    for (e, s), u, k in zip(pairs, used, slots):
        @pl.when(u)
        def _(e=e, s=s, k=k):
            copy(e, s, k).wait()

    gate = gc_ref[...]
    lane = lax.broadcasted_iota(I32, gate.shape, 1)
    picked = gate > 0.0
    first = lane == jnp.min(jnp.where(picked, lane, LANES), axis=1, keepdims=True)
    g_lower = jnp.sum(jnp.where(first, gate, 0.0), axis=1, keepdims=True)
    g_upper = jnp.sum(jnp.where(first, 0.0, gate), axis=1, keepdims=True)
    tn = (((0,), (0,)), ((), ()))
    z = lax.dot_general(p_ref[...], ybuf_ref[...], tn, preferred_element_type=F32)
    return g_lower * z[:TB] + g_upper * z[TB:]


_COMBINE_SCRATCH = [pltpu.VMEM((SLOTS * SUB, D), BF16),
                    pltpu.VMEM((SLOTS * SUB, 2 * TB), BF16),
                    pltpu.SemaphoreType.DMA((SLOTS,))]


def _combine_kernel(off_ref, cnt_ref, gt_ref, gc_ref, x3_ref, mod_ref, gpost_ref, ys_ref,
                    out_ref, ybuf_ref, p_ref, sem):
    b = pl.program_id(0)
    moe = _gather_expert_rows(b, b == 0, off_ref, cnt_ref, gt_ref, gc_ref, ys_ref, ybuf_ref, p_ref, sem)
    _, _, gt = _split3(mod_ref[0, 0])
    out_ref[...] = x3_ref[...] + gt * (_rms_scale(moe) * gpost_ref[...])


def _combine(off, cnt, gates_t, gates_c, x3, mods_ffn, gpost, ys, t):
    n = x3.shape[0]
    per_b = t // TB
    return pl.pallas_call(
        _combine_kernel,
        grid_spec=pltpu.PrefetchScalarGridSpec(
            num_scalar_prefetch=2,
            grid=(n // TB,),
            in_specs=[pl.BlockSpec((N_EXPERTS, TB), lambda b, o, c: (0, b)),
                      pl.BlockSpec((TB, LANES), lambda b, o, c: (b, 0)),
                      pl.BlockSpec((TB, D), lambda b, o, c: (b, 0)),
                      pl.BlockSpec((1, 1, 1, 3 * D), lambda b, o, c: (1, b // per_b, 0, 0)),
                      pl.BlockSpec((1, D), lambda b, o, c: (0, 0)),
                      pl.BlockSpec(memory_space=pl.ANY)],
            out_specs=pl.BlockSpec((TB, D), lambda b, o, c: (b, 0)),
            scratch_shapes=_COMBINE_SCRATCH),
        out_shape=jax.ShapeDtypeStruct((n, D), F32),
        compiler_params=_cparams(("arbitrary",), 48),
        name="moe_combine",
    )(off, cnt, gates_t, gates_c, x3, mods_ffn, gpost, ys)


def _combine_rows_kernel(off_ref, cnt_ref, gt_ref, gc_ref, ys_ref, out_ref, ybuf_ref, p_ref, sem,
                         *, block0):
    out_ref[...] = _gather_expert_rows(block0, True, off_ref, cnt_ref, gt_ref, gc_ref, ys_ref,
                                       ybuf_ref, p_ref, sem)


def _combine_rows(off, cnt, gates_t, gates_c, ys, block0):
    return pl.pallas_call(
        functools.partial(_combine_rows_kernel, block0=block0),
        grid_spec=pltpu.PrefetchScalarGridSpec(
            num_scalar_prefetch=2,
            grid=(1,),
            in_specs=[pl.BlockSpec((N_EXPERTS, TB), lambda b, o, c: (0, 0)),
                      pl.BlockSpec((TB, LANES), lambda b, o, c: (0, 0)),
                      pl.BlockSpec(memory_space=pl.ANY)],
            out_specs=pl.BlockSpec((TB, D), lambda b, o, c: (0, 0)),
            scratch_shapes=_COMBINE_SCRATCH),
        out_shape=jax.ShapeDtypeStruct((TB, D), F32),
        compiler_params=_cparams(("arbitrary",), 48),
        name="moe_combine_rows",
    )(off, cnt, gates_t, gates_c, ys)


def _s_mix0_kernel(x_ref, st_ref, modm_ref, modf_ref, gpre_ref, gpost_ref, gpref_ref, pw_ref, ps_ref,
                   x1_ref, pool_ref, hf_ref, *, pos):
    sh, sc, gt = _split3(modm_ref[...])
    x = x_ref[...]
    hm = _rms_scale(x) * gpre_ref[...] * (1.0 + sc) + sh
    parts = []
    for g, w in enumerate(POOL_WINDOWS):
        lo = g * PG
        tok = hm[:, lo:lo + PG]
        s = tok
        for j in range(1, w):
            s = s + st_ref[POOL_STATE - j, :, lo:lo + PG]
        cnt = float(min(w, pos + 1))
        parts.append(_dot_hi(s / cnt - tok, pw_ref[g]))
    mix = jnp.concatenate(parts, axis=1) * ps_ref[...]
    x1 = x + gt * (_rms_scale(mix) * gpost_ref[...])
    x1_ref[...] = x1
    for j in range(POOL_STATE - 1):
        pool_ref[j] = st_ref[j + 1]
    pool_ref[POOL_STATE - 1] = hm
    shf, scf, _ = _split3(modf_ref[...])
    hf_ref[...] = _rms_scale(x1) * gpref_ref[...] * (1.0 + scf) + shf


def _s_mix0(x_s, state_t, modm, modf, gpre, gpost, gpref, pw, ps, pos):
    rows = x_s.shape[0]
    return pl.pallas_call(
        functools.partial(_s_mix0_kernel, pos=pos),
        out_shape=[jax.ShapeDtypeStruct((rows, D), F32),
                   jax.ShapeDtypeStruct((POOL_STATE, rows, D), F32),
                   jax.ShapeDtypeStruct((rows, D), F32)],
        compiler_params=pltpu.CompilerParams(vmem_limit_bytes=40 * MIB),
        name="s_mix0",
    )(x_s, state_t, modm, modf, gpre, gpost, gpref, pw, ps)


def _s_ffn_kernel(h_ref, w1_ref, w3_ref, w2_ref, f_ref):
    j = pl.program_id(0)
    h = h_ref[...]
    contrib = _dot_hi(_silu(_dot_hi(h, w1_ref[...])) * _dot_hi(h, w3_ref[...]), w2_ref[...])

    @pl.when(j == 0)
    def _():
        f_ref[...] = contrib

    @pl.when(j > 0)
    def _():
        f_ref[...] += contrib


def _s_ffn(h, w1, w3, w2):
    rows = h.shape[0]
    dff = w1.shape[1]
    fc = 256
    return pl.pallas_call(
        _s_ffn_kernel,
        grid=(dff // fc,),
        in_specs=[pl.BlockSpec((rows, D), lambda j: (0, 0)),
                  pl.BlockSpec((D, fc), lambda j: (0, j)),
                  pl.BlockSpec((D, fc), lambda j: (0, j)),
                  pl.BlockSpec((fc, D), lambda j: (j, 0))],
        out_specs=pl.BlockSpec((rows, D), lambda j: (0, 0)),
        out_shape=jax.ShapeDtypeStruct((rows, D), F32),
        compiler_params=_cparams(("arbitrary",), 32),
        name="s_ffn0",
    )(h, w1, w3, w2)


def _s_post0_kernel(x1_ref, f_ref, modf_ref, modm_ref, gpost_ref, gkv_ref, gpre1_ref,
                    x2_ref, kvin_ref, hm1_ref):
    _, _, gt = _split3(modf_ref[...])
    x2 = x1_ref[...] + gt * (_rms_scale(f_ref[...]) * gpost_ref[...])
    x2_ref[...] = x2
    xn = _rms_scale(x2)
    kvin_ref[...] = xn * gkv_ref[...]
    sh1, sc1, _ = _split3(modm_ref[...])
    hm1_ref[...] = xn * gpre1_ref[...] * (1.0 + sc1) + sh1


def _s_post0(x1, f, modf, modm, gpost, gkv, gpre1):
    rows = x1.shape[0]
    return pl.pallas_call(
        _s_post0_kernel,
        out_shape=[jax.ShapeDtypeStruct((rows, D), F32)] * 3,
        name="s_post0",
    )(x1, f, modf, modm, gpost, gkv, gpre1)


def _s_linear_kernel(h_ref, w_ref, cos_ref, sin_ref, o_ref):
    o_ref[...] = _rope_cols(_dot_hi(h_ref[...], w_ref[...]), cos_ref[...], sin_ref[...])


def _s_linear_plain_kernel(h_ref, w_ref, o_ref):
    o_ref[...] = _dot_hi(h_ref[...], w_ref[...])


def _s_linear(h, w, rope=None):
    rows, k = h.shape
    nw = w.shape[1]
    cw = 512
    in_specs = [pl.BlockSpec((rows, k), lambda j: (0, 0)), pl.BlockSpec((k, cw), lambda j: (0, j))]
    args = [h, w]
    body = _s_linear_plain_kernel
    if rope is not None:
        body = _s_linear_kernel
        in_specs += [pl.BlockSpec((1, cw), lambda j: (0, j)), pl.BlockSpec((1, cw), lambda j: (0, j))]
        args += [rope[0], rope[1]]
    return pl.pallas_call(
        body,
        grid=(nw // cw,),
        in_specs=in_specs,
        out_specs=pl.BlockSpec((rows, cw), lambda j: (0, j)),
        out_shape=jax.ShapeDtypeStruct((rows, nw), F32),
        compiler_params=_cparams(("arbitrary",), 32),
        name="s_linear",
    )(*args)


def _s_attn_kernel(qkv_ref, kc_ref, vc_ref, o_ref):
    n_past = kc_ref.shape[3]
    hb = kc_ref.shape[1]
    ng = len(ATTN_GROUPS)
    nt = (((1,), (1,)), ((), ()))
    eye = (lax.broadcasted_iota(I32, (LANES, LANES), 0) == lax.broadcasted_iota(I32, (LANES, LANES), 1)).astype(F32)
    tiles = jnp.concatenate([qkv_ref[0].reshape(hb * 8, HD), jnp.zeros((LANES - hb * 8, HD), F32)], axis=0)
    cols = lax.dot_general(eye[:HD, :HD], tiles, nt, precision=HIGHEST, preferred_element_type=F32)
    col = lambda h, j: cols[:, 8 * h + j:8 * h + j + 1]
    outs, lses = [], []
    for g, (win, dil) in enumerate(ATTN_GROUPS):
        lo = n_past - win
        dist = win - lax.broadcasted_iota(I32, (1, win), 1)
        s_c = jnp.concatenate([jnp.sum(kc_ref[0, h, :, lo:] * col(h, g), axis=0, keepdims=True)
                               for h in range(hb)], axis=0) * SCALE
        s_c = jnp.where((dist & (dil - 1)) == 0, s_c, -jnp.inf)
        s_n = jnp.concatenate([jnp.sum(col(h, ng) * col(h, g), axis=0, keepdims=True)
                               for h in range(hb)], axis=0) * SCALE
        m = jnp.maximum(jnp.max(s_c, axis=1, keepdims=True), s_n)
        p_c = jnp.exp(s_c - m)
        p_n = jnp.exp(s_n - m)
        den = jnp.sum(p_c, axis=1, keepdims=True) + p_n
        outs.append([(jnp.sum(vc_ref[0, h, :, lo:] * p_c[h:h + 1], axis=1, keepdims=True)
                      + p_n[h:h + 1] * col(h, ng + 1)) / den[h:h + 1] for h in range(hb)])
        lses.append(m + jnp.log(den))
    mx = jnp.maximum(jnp.maximum(lses[0], lses[1]), lses[2])
    es = [jnp.exp(l - mx) for l in lses]
    tot = es[0] + es[1] + es[2]
    lane = lax.broadcasted_iota(I32, (1, LANES), 1)
    out_cols = jnp.zeros((HD, LANES), F32)
    for h in range(hb):
        out = (es[0] / tot)[h:h + 1] * outs[0][h]
        for g in range(1, ng):
            out = out + (es[g] / tot)[h:h + 1] * outs[g][h]
        out_cols = out_cols + out * (lane == h).astype(F32)
    o_ref[0] = lax.dot_general(eye[:hb], out_cols, nt, precision=HIGHEST, preferred_element_type=F32)


def _s_attn(qkv, cache_kt, cache_vt):
    rows, _, _, n_past = cache_kt.shape
    for win, dil in ATTN_GROUPS:
        assert n_past >= win and win % LANES == 0 and dil & (dil - 1) == 0
    cache = pl.BlockSpec((1, HB_S, HD, n_past), lambda b, h: (b, h, 0, 0))
    return pl.pallas_call(
        _s_attn_kernel,
        grid=(rows, NH // HB_S),
        in_specs=[pl.BlockSpec((1, HB_S, 8, HD), lambda b, h: (b, h, 0, 0)), cache, cache],
        out_specs=pl.BlockSpec((1, HB_S, HD), lambda b, h: (b, h, 0)),
        out_shape=jax.ShapeDtypeStruct((rows, NH, HD), F32),
        compiler_params=_cparams(("arbitrary", "arbitrary"), 40),
        name="s_attn",
    )(qkv, cache_kt, cache_vt)


def _s_post1_kernel(x2_ref, mix_ref, modm_ref, modf_ref, gpost_ref, gpre_ref, rwt_ref, rb_ref,
                    x3_ref, hf_ref, gc_ref):
    _, _, gt = _split3(modm_ref[...])
    x3 = x2_ref[...] + gt * (_rms_scale(mix_ref[...]) * gpost_ref[...])
    x3_ref[...] = x3
    sh, sc, _ = _split3(modf_ref[...])
    hf = _rms_scale(x3) * gpre_ref[...] * (1.0 + sc) + sh
    hf_ref[...] = hf
    logits_t = lax.dot_general(rwt_ref[...], hf, (((1,), (1,)), ((), ())),
                               precision=HIGHEST, preferred_element_type=F32) + rb_ref[...]
    gc_ref[...] = _router_top2(logits_t)


def _s_post1(x2, mix, modm, modf, gpost, gpre, rwt, rb):
    rows = x2.shape[0]
    return pl.pallas_call(
        _s_post1_kernel,
        out_shape=[jax.ShapeDtypeStruct((rows, D), F32), jax.ShapeDtypeStruct((rows, D), F32),
                   jax.ShapeDtypeStruct((N_EXPERTS, rows), F32)],
        name="s_post1",
    )(x2, mix, modm, modf, gpost, gpre, rwt, rb)


def _s_post2_kernel(x3_ref, f_ref, modf_ref, gpost_ref, y_ref):
    _, _, gt = _split3(modf_ref[...])
    y_ref[...] = x3_ref[...] + gt * (_rms_scale(f_ref[...]) * gpost_ref[...])


def _s_post2(x3, f, modf, gpost):
    return pl.pallas_call(
        _s_post2_kernel,
        out_shape=jax.ShapeDtypeStruct(x3.shape, F32),
        name="s_post2",
    )(x3, f, modf, gpost)


def _rope_tables(pos):
    half = HD // 2
    inv = ROPE_THETA ** (-jnp.arange(half, dtype=F32) / half)
    ang = pos.astype(F32)[:, None] * inv[None, :]
    cos, sin = jnp.cos(ang), jnp.sin(ang)
    cos = jnp.tile(cos, (1, LANES // half))
    sin = jnp.tile(jnp.concatenate([-sin, sin], axis=1), (1, LANES // HD))
    return cos, sin


def kernel(x_prompt, x_sample, state_pool, cache_k, cache_v, c_prompt, c_sample, ada_mix_w, ada_mix_b, g_pre_mix, g_post_mix, pool_w, pool_scale, g_kv, w_kv, w_q, w_o, ada_ffn_w, ada_ffn_b, g_pre_ffn, g_post_ffn, ffn_w1, ffn_w3, ffn_w2, router_w, router_b, moe_w1, moe_w3, moe_w2):
    b, t, _ = x_prompt.shape
    nsmp = x_sample.shape[0]
    n = b * t
    past_len = 16384
    assert x_sample.shape[1] == 1 and g_pre_mix.shape[0] == 2 and t % (ATTN_GROUPS[-1][1] * BLK) == 0
    vec = lambda a: a.reshape(1, D)

    c_all = jnp.concatenate([c_prompt, c_sample], axis=0)
    mods_mix = _ada(c_all, ada_mix_w, ada_mix_b)
    mods_ffn = _ada(c_all, ada_ffn_w, ada_ffn_b)
    pm_mix = mods_mix[:, :b].reshape(2, b, 1, 3 * D)
    pm_ffn = mods_ffn[:, :b].reshape(2, b, 1, 3 * D)
    sm_mix = mods_mix[:, b:]
    sm_ffn = mods_ffn[:, b:]

    rwt = router_w[0].T
    rb = router_b[0].reshape(N_EXPERTS, 1)

    x1, pool16 = _mix0(x_prompt, pm_mix, vec(g_pre_mix[0]), vec(g_post_mix[0]),
                       pool_w[0].astype(BF16), vec(pool_scale[0]))
    cos_p, sin_p = _rope_tables(jnp.arange(t, dtype=jnp.int32))
    x2 = _ffn0(x1.reshape(n, D), pm_ffn, vec(g_pre_ffn[0]), vec(g_post_ffn[0]),
               ffn_w1[0].astype(BF16), ffn_w3[0].astype(BF16), ffn_w2[0].astype(BF16), t)
    kt_p, vt_p, *kv_l = _kvproj(x2, vec(g_kv), cos_p, sin_p, w_kv.astype(BF16), b, t)
    q_l = _qproj(x2, pm_mix, vec(g_pre_mix[1]), cos_p, sin_p, w_q[0].astype(BF16), b, t)
    attn = _attention(q_l, kv_l[:3], kv_l[3:], b, t)
    x3, hf1, gates_t, gates_c = _post1(attn, x2, pm_mix, pm_ffn, vec(g_post_mix[1]),
                                       vec(g_pre_ffn[1]), w_o[0].astype(BF16), rwt, rb, t)

    state_t = jnp.transpose(state_pool[:, 0], (1, 0, 2))
    x1s, pool_t, hf0s = _s_mix0(x_sample.reshape(nsmp, D), state_t, sm_mix[0], sm_ffn[0],
                                vec(g_pre_mix[0]), vec(g_post_mix[0]), vec(g_pre_ffn[0]),
                                pool_w[0], vec(pool_scale[0]), past_len)
    f0s = _s_ffn(hf0s, ffn_w1[0], ffn_w3[0], ffn_w2[0])
    x2s, kvin_s, hm1s = _s_post0(x1s, f0s, sm_ffn[0], sm_mix[1], vec(g_post_ffn[0]), vec(g_kv), vec(g_pre_mix[1]))
    cos_s, sin_s = _rope_tables(jnp.full((1,), past_len, jnp.int32))
    ones, zeros = jnp.ones((1, D), F32), jnp.zeros((1, D), F32)
    kv_s = _s_linear(kvin_s, w_kv, (jnp.concatenate([jnp.tile(cos_s, (1, D // LANES)), ones], axis=1),
                                    jnp.concatenate([jnp.tile(sin_s, (1, D // LANES)), zeros], axis=1)))
    q_s = _s_linear(hm1s, w_q[0], (jnp.tile(cos_s, (1, 3 * D // LANES)), jnp.tile(sin_s, (1, 3 * D // LANES))))
    k_s = kv_s[:, :D].reshape(nsmp, NH, HD)
    v_s = kv_s[:, D:].reshape(nsmp, NH, HD)
    cache_kt = jnp.transpose(cache_k, (0, 2, 3, 1))
    cache_vt = jnp.transpose(cache_v, (0, 2, 3, 1))
    qkv = jnp.concatenate([jnp.transpose(q_s.reshape(nsmp, len(ATTN_GROUPS), NH, HD), (0, 2, 1, 3)),
                           k_s[:, :, None], v_s[:, :, None]], axis=2)
    qkv = jnp.pad(qkv, ((0, 0), (0, 0), (0, 8 - qkv.shape[2]), (0, 0)))
    attn_s = _s_attn(qkv, cache_kt, cache_vt).reshape(nsmp, D)
    mix1s = _s_linear(attn_s, w_o[0])
    x3s, hf1s, gts = _s_post1(x2s, mix1s, sm_mix[1], sm_ffn[1], vec(g_post_mix[1]), vec(g_pre_ffn[1]), rwt, rb)

    hf_blk = jnp.zeros((TB, D), BF16).at[:nsmp].set(hf1s.astype(BF16))
    gt_blk = jnp.zeros((N_EXPERTS, TB), F32).at[:, :nsmp].set(gts)
    gc_blk = jnp.zeros((TB, LANES), F32).at[:nsmp, :N_EXPERTS].set(gts.T)
    off, cnt, tile_e, n_used = _route_plan(jnp.concatenate([gates_t, gt_blk], axis=1), n + TB)
    xs = jnp.zeros((_max_sorted_rows(n + TB), D), BF16)
    xs = _dispatch(off, cnt, hf1, gates_t, xs, 0)
    xs = _dispatch(off, cnt, hf_blk, gt_blk, xs, n // TB)
    ys = _experts(tile_e, n_used, xs, moe_w1[0], moe_w3[0], moe_w2[0])
    y_prompt = _combine(off, cnt, gates_t, gates_c, x3, pm_ffn, vec(g_post_ffn[1]), ys, t).reshape(b, t, D)
    moe_s = _combine_rows(off, cnt, gt_blk, gc_blk, ys, n // TB)[:nsmp]
    y_sample = _s_post2(x3s, moe_s, sm_ffn[1], vec(g_post_ffn[1])).reshape(nsmp, 1, D)

    pool_prompt = pool16[:, None, 1:, :]
    pool_sample = jnp.transpose(pool_t, (1, 0, 2))[:, None]
    return (y_prompt, y_sample, pool_prompt, pool_sample,
            jnp.transpose(kt_p, (0, 3, 1, 2)), jnp.transpose(vt_p, (0, 3, 1, 2)),
            k_s[:, None], v_s[:, None])
```

```python
import functools

import jax
import jax.numpy as jnp
import numpy as np
from jax import lax
from jax.experimental import pallas as pl
from jax.experimental.pallas import tpu as pltpu

F32, BF16, I32 = jnp.float32, jnp.bfloat16, jnp.int32
HIGHEST = lax.Precision.HIGHEST

D = 1024
POOL_WINDOWS = (2, 4, 8, 16)
PG = D // len(POOL_WINDOWS)
POOL_STATE = max(POOL_WINDOWS) - 1
POOL_ROWS = POOL_STATE + 1
HALO = 2 * POOL_ROWS
ATTN_GROUPS = ((128, 1), (512, 4), (2048, 16))
NH = 16
HD = 64
BLK = 128
ROPE_THETA = 10000.0
N_EXPERTS = 8
EPS = 1e-6
SCALE = HD ** -0.5
LANES = 128
MIB = 1024 * 1024

TT_MIX = 512
TM_FFN = 1024
HB_S = 8
ATTN_UNITS = 8
FC_FFN = 256
TM_POST = 512
TM_WO = 1024
TB = 512
SUB = 128
NS = TB // SUB
TMX = 1024
FCX = 512
ROW_ALIGN = 16


def _cparams(sem, vmem_mib):
    return pltpu.CompilerParams(dimension_semantics=sem, vmem_limit_bytes=vmem_mib * MIB)


def _rms_scale(x):
    return x * lax.rsqrt(jnp.mean(x * x, axis=-1, keepdims=True) + EPS)


def _split3(m):
    return m[:, :D], m[:, D:2 * D], m[:, 2 * D:]


def _silu(x):
    return x * jax.nn.sigmoid(x)


def _rope_cols(x, cos, sin):
    lane = lax.broadcasted_iota(I32, (x.shape[0], LANES), 1)
    first = (lane % HD) < (HD // 2)
    shared = cos.shape[1] == LANES
    outs = []
    for c in range(x.shape[1] // LANES):
        cols = slice(c * LANES, (c + 1) * LANES)
        xc = x[:, cols]
        swapped = jnp.where(first, pltpu.roll(xc, LANES - HD // 2, 1), pltpu.roll(xc, HD // 2, 1))
        outs.append(xc * (cos if shared else cos[:, cols]) + swapped * (sin if shared else sin[:, cols]))
    return jnp.concatenate(outs, axis=1)


def _rope_paired(x, cos, sin):
    outs = []
    for c in range(x.shape[1] // LANES):
        xc = x[:, c * LANES:(c + 1) * LANES]
        outs.append(xc * cos + pltpu.roll(xc, LANES // 2, 1) * sin)
    return jnp.concatenate(outs, axis=1)


def _pair_heads(w):
    k, n = w.shape
    return w.reshape(k, n // LANES, 2, 2, HD // 2).transpose(0, 1, 3, 2, 4).reshape(k, n)


def _dot_hi(a, b):
    return jnp.dot(a, b, precision=HIGHEST, preferred_element_type=F32)


def _dot(a, b):
    return jnp.dot(a, b, preferred_element_type=F32)


def _ada_kernel(c_ref, w_ref, b_ref, o_ref):
    o_ref[0] = _dot_hi(_silu(c_ref[...]), w_ref[0]) + b_ref[0]


def _ada(c_all, w, b):
    nl, _, n3 = w.shape
    rows = c_all.shape[0]
    cw = 768
    return pl.pallas_call(
        _ada_kernel,
        grid=(nl, n3 // cw),
        in_specs=[pl.BlockSpec((rows, D), lambda l, j: (0, 0)),
                  pl.BlockSpec((1, D, cw), lambda l, j: (l, 0, j)),
                  pl.BlockSpec((1, 1, cw), lambda l, j: (l, 0, j))],
        out_specs=pl.BlockSpec((1, rows, cw), lambda l, j: (l, 0, j)),
        out_shape=jax.ShapeDtypeStruct((nl, rows, n3), F32),
        compiler_params=_cparams(("arbitrary", "arbitrary"), 32),
        name="ada_mod",
    )(c_all, w, b.reshape(nl, 1, n3))


def _mix0_kernel(x_ref, xh_ref, mod_ref, gpre_ref, gpost_ref, pw_ref, ps_ref,
                 x1_ref, pool_ref, ext_ref, lv_ref):
    i = pl.program_id(1)
    tt = x_ref.shape[1]
    sh, sc, gt = _split3(mod_ref[0, 0])
    gpre = gpre_ref[...]
    x = x_ref[0]
    hm = _rms_scale(x) * gpre * (1.0 + sc) + sh
    hh = _rms_scale(xh_ref[0]) * gpre * (1.0 + sc) + sh
    ext_ref[0:HALO, :] = jnp.where(i > 0, hh, 0.0)
    ext_ref[HALO:, :] = hm
    t = i * tt + lax.broadcasted_iota(I32, (tt, 1), 0)
    rows = tt + HALO
    parts = []
    for g, w in enumerate(POOL_WINDOWS):
        lo = g * PG
        tok = hm[:, lo:lo + PG]
        levels = w.bit_length() - 1
        assert w == 1 << levels
        first = [HALO]
        for k in range(levels - 1, 0, -1):
            first.insert(0, (first[0] - (1 << k)) // 8 * 8)
        for k in range(1, levels + 1):
            r0, back = first[k - 1], 1 << (k - 1)
            if k == 1:
                s = ext_ref[r0:rows, lo:lo + PG] + ext_ref[r0 - back:rows - back, lo:lo + PG]
            else:
                prev = lv_ref.at[k % 2]
                s = prev[r0:rows, :] + prev[r0 - back:rows - back, :]
            if k < levels:
                lv_ref[(k + 1) % 2, r0:rows, :] = s
        inv_cnt = 1.0 / jnp.minimum(w, t + 1).astype(F32)
        parts.append(_dot((s * inv_cnt - tok).astype(BF16), pw_ref[g]))
    mix = jnp.concatenate(parts, axis=1) * ps_ref[...]
    x1_ref[0] = x + gt * (_rms_scale(mix) * gpost_ref[...])

    @pl.when(i == pl.num_programs(1) - 1)
    def _():
        pool_ref[0] = ext_ref[rows - POOL_ROWS:rows, :]


def _mix0(x, mods, gpre, gpost, pw_bf, ps):
    b, t, _ = x.shape
    tt = TT_MIX
    hb = tt // HALO
    return pl.pallas_call(
        _mix0_kernel,
        grid=(b, t // tt),
        in_specs=[pl.BlockSpec((1, tt, D), lambda bb, i: (bb, i, 0)),
                  pl.BlockSpec((1, HALO, D), lambda bb, i: (bb, jnp.maximum(i * hb - 1, 0), 0)),
                  pl.BlockSpec((1, 1, 1, 3 * D), lambda bb, i: (0, bb, 0, 0)),
                  pl.BlockSpec((1, D), lambda bb, i: (0, 0)),
                  pl.BlockSpec((1, D), lambda bb, i: (0, 0)),
                  pl.BlockSpec((len(POOL_WINDOWS), PG, PG), lambda bb, i: (0, 0, 0)),
                  pl.BlockSpec((1, D), lambda bb, i: (0, 0))],
        out_specs=[pl.BlockSpec((1, tt, D), lambda bb, i: (bb, i, 0)),
                   pl.BlockSpec((1, POOL_ROWS, D), lambda bb, i: (bb, 0, 0))],
        out_shape=[jax.ShapeDtypeStruct((b, t, D), F32),
                   jax.ShapeDtypeStruct((b, POOL_ROWS, D), F32)],
        scratch_shapes=[pltpu.VMEM((tt + HALO, D), F32), pltpu.VMEM((2, tt + HALO, PG), F32)],
        compiler_params=_cparams(("arbitrary", "arbitrary"), 40),
        name="mix0",
    )(x, x, mods, gpre, gpost, pw_bf, ps)


def _store_dilated(val, plane_ref, dils, out_refs):
    tm = val.shape[0]
    for hp in range(NH // 2):
        plane_ref[hp] = val[:, hp * LANES:(hp + 1) * LANES]
    for dil, out_ref in zip(dils, out_refs):
        for hp in range(NH // 2):
            for r in range(dil):
                out_ref[0, hp, r] = plane_ref[hp, pl.ds(r, tm // dil, stride=dil), :].astype(BF16)


def _dilated_specs(b, t, tm):
    per_b = t // tm
    specs = [pl.BlockSpec((1, NH // 2, dil, tm // dil, LANES), lambda i: (i // per_b, 0, 0, i % per_b, 0))
             for _, dil in ATTN_GROUPS]
    shapes = [jax.ShapeDtypeStruct((b, NH // 2, dil, t // dil, LANES), BF16) for _, dil in ATTN_GROUPS]
    return specs, shapes


def _ffn0_kernel(x1_ref, modf_ref, gpre_ref, gpost_ref, w1_ref, w3_ref, w2_ref, x2_ref, acc_ref):
    sh, sc, gt = _split3(modf_ref[0, 0])
    x1 = x1_ref[...]
    hf = (_rms_scale(x1) * gpre_ref[...] * (1.0 + sc) + sh).astype(BF16)
    for c in range(w1_ref.shape[1] // FC_FFN):
        cs = slice(c * FC_FFN, (c + 1) * FC_FFN)
        a = (_silu(_dot(hf, w1_ref[:, cs])) * _dot(hf, w3_ref[:, cs])).astype(BF16)
        contrib = _dot(a, w2_ref[cs, :])
        if c == 0:
            acc_ref[...] = contrib
        else:
            acc_ref[...] += contrib
    x2 = x1 + gt * (_rms_scale(acc_ref[...]) * gpost_ref[...])
    x2_ref[...] = x2


def _ffn0(x1, mods_ffn, gpre, gpost, w1, w3, w2, t):
    n = x1.shape[0]
    tm = TM_FFN
    per_b = t // tm
    dff = w1.shape[1]
    row = lambda i: (i, 0)
    fixed = lambda i: (0, 0)
    resident = lambda shape: pl.BlockSpec(shape, fixed, pipeline_mode=pl.Buffered(1))
    vec = pl.BlockSpec((1, D), fixed)
    return pl.pallas_call(
        _ffn0_kernel,
        grid=(n // tm,),
        in_specs=[pl.BlockSpec((tm, D), row),
                  pl.BlockSpec((1, 1, 1, 3 * D), lambda i: (0, i // per_b, 0, 0)),
                  vec, vec,
                  resident((D, dff)), resident((D, dff)), resident((dff, D))],
        out_specs=pl.BlockSpec((tm, D), row),
        out_shape=jax.ShapeDtypeStruct((n, D), F32),
        scratch_shapes=[pltpu.VMEM((tm, D), F32)],
        compiler_params=_cparams(("arbitrary",), 56),
        name="ffn0",
    )(x1, mods_ffn, gpre, gpost, w1, w3, w2)


def _kvproj_kernel(x2_ref, gkv_ref, cos_ref, sin_ref, wkv_ref,
                   kt_ref, vt_ref, k0_ref, k1_ref, k2_ref, v0_ref, v1_ref, v2_ref, plane_ref):
    tm = x2_ref.shape[0]
    kv = _dot((_rms_scale(x2_ref[...]) * gkv_ref[...]).astype(BF16), wkv_ref[...])
    k = _rope_paired(kv[:, :D], cos_ref[...], sin_ref[...])
    v = kv[:, D:]
    kt = k.T
    half = HD // 2
    for p in range(NH // 2):
        for j, (head, lo) in enumerate(((2 * p, 0), (2 * p + 1, 0), (2 * p, half), (2 * p + 1, half))):
            kt_ref[0, head, lo:lo + half, :] = kt[p * LANES + j * half:p * LANES + (j + 1) * half, :]
    vt_ref[0] = v.T.reshape(NH, HD, tm)
    dils = [dil for _, dil in ATTN_GROUPS]
    _store_dilated(k, plane_ref, dils, (k0_ref, k1_ref, k2_ref))
    _store_dilated(v, plane_ref, dils, (v0_ref, v1_ref, v2_ref))


def _kvproj(x2, gkv, cos, sin, wkv, b, t):
    n = x2.shape[0]
    tm = TM_POST
    per_b = t // tm
    fixed = lambda i: (0, 0)
    dspecs, dshapes = _dilated_specs(b, t, tm)
    tspec = pl.BlockSpec((1, NH, HD, tm), lambda i: (i // per_b, 0, 0, i % per_b))
    tshape = jax.ShapeDtypeStruct((b, NH, HD, t), F32)
    return pl.pallas_call(
        _kvproj_kernel,
        grid=(n // tm,),
        in_specs=[pl.BlockSpec((tm, D), lambda i: (i, 0)),
                  pl.BlockSpec((1, D), fixed),
                  pl.BlockSpec((tm, LANES), lambda i: (i % per_b, 0)),
                  pl.BlockSpec((tm, LANES), lambda i: (i % per_b, 0)),
                  pl.BlockSpec((D, 2 * D), fixed, pipeline_mode=pl.Buffered(1))],
        out_specs=[tspec, tspec] + dspecs + dspecs,
        out_shape=[tshape, tshape] + dshapes + dshapes,
        scratch_shapes=[pltpu.VMEM((NH // 2, tm, LANES), F32)],
        compiler_params=_cparams(("arbitrary",), 48),
        name="kv_proj",
    )(x2, gkv, cos, sin, wkv)


def _qproj_kernel(x2_ref, modm_ref, gpre1_ref, cos_ref, sin_ref, wq_ref, q0_ref, q1_ref, q2_ref, scr_ref):
    sh1, sc1, _ = _split3(modm_ref[0, 0])
    hm1 = (_rms_scale(x2_ref[...]) * gpre1_ref[...] * (1.0 + sc1) + sh1).astype(BF16)
    q = _rope_paired(_dot(hm1, wq_ref[...]), cos_ref[...], sin_ref[...]) * SCALE
    for g, ((_, dil), q_ref) in enumerate(zip(ATTN_GROUPS, (q0_ref, q1_ref, q2_ref))):
        _store_dilated(q[:, g * D:(g + 1) * D], scr_ref, (dil,), (q_ref,))


def _qproj(x2, mods_mix, gpre1, cos, sin, wq, b, t):
    n = x2.shape[0]
    tm = TM_POST
    per_b = t // tm
    fixed = lambda i: (0, 0)
    dspecs, dshapes = _dilated_specs(b, t, tm)
    return pl.pallas_call(
        _qproj_kernel,
        grid=(n // tm,),
        in_specs=[pl.BlockSpec((tm, D), lambda i: (i, 0)),
                  pl.BlockSpec((1, 1, 1, 3 * D), lambda i: (1, i // per_b, 0, 0)),
                  pl.BlockSpec((1, D), fixed),
                  pl.BlockSpec((tm, LANES), lambda i: (i % per_b, 0)),
                  pl.BlockSpec((tm, LANES), lambda i: (i % per_b, 0)),
                  pl.BlockSpec((D, 3 * D), fixed, pipeline_mode=pl.Buffered(1))],
        out_specs=dspecs,
        out_shape=dshapes,
        scratch_shapes=[pltpu.VMEM((NH // 2, tm, LANES), F32)],
        compiler_params=_cparams(("arbitrary",), 48),
        name="q_proj",
    )(x2, mods_mix, gpre1, cos, sin, wq)


def _attn_kernel(q0_ref, q1_ref, q2_ref, k0_ref, k1_ref, k2_ref, v0_ref, v1_ref, v2_ref,
                 o_ref, og_ref, lg_ref, dg_ref, bias_ref):
    t = o_ref.shape[2]
    lane = lax.broadcasted_iota(I32, (BLK, LANES), 1)
    first_head = lane < HD
    first_head_qk = lane % HD < HD // 2
    nt = (((1,), (1,)), ((), ()))

    @pl.when((pl.program_id(0) == 0) & (pl.program_id(1) == 0))
    def _():
        qi = lax.broadcasted_iota(I32, (2 * BLK, 2 * BLK), 0) % BLK
        kj = lax.broadcasted_iota(I32, (2 * BLK, 2 * BLK), 1)
        for v in range(2):
            dist = v * BLK + qi - kj
            bias_ref[v] = jnp.where((dist >= 0) & (dist <= BLK), 0.0, -jnp.inf)

    groups = zip(ATTN_GROUPS, (q0_ref, q1_ref, q2_ref), (k0_ref, k1_ref, k2_ref), (v0_ref, v1_ref, v2_ref))
    for g, ((win, dil), q_ref, k_ref, v_ref) in enumerate(groups):
        assert win // dil == BLK
        sub = t // dil
        nb = sub // BLK
        kw = min(2 * BLK, sub)

        def scores(u, nb=nb, kw=kw, q_ref=q_ref, k_ref=k_ref):
            r = u // nb
            n = u % nb
            q_lo = pl.multiple_of(n * BLK, BLK)
            k_lo = pl.multiple_of(jnp.maximum(n - 1, 0) * BLK, BLK)
            qp = q_ref[0, 0, r, pl.ds(q_lo, BLK), :]
            zero = jnp.zeros_like(qp)
            qs = jnp.concatenate([jnp.where(first_head_qk, qp, zero), jnp.where(first_head_qk, zero, qp)], axis=0)
            s = lax.dot_general(qs, k_ref[0, 0, r, pl.ds(k_lo, kw), :], nt, preferred_element_type=F32)
            return s + bias_ref[jnp.minimum(n, 1), :, 0:kw], r, q_lo, k_lo

        def finish(s, r, q_lo, k_lo, dil=dil, kw=kw, v_ref=v_ref, g=g):
            m = jnp.max(s, axis=-1, keepdims=True)
            p = jnp.exp(s - m)
            den = jnp.sum(p, axis=-1, keepdims=True)
            pv = _dot(p.astype(BF16), v_ref[0, 0, r, pl.ds(k_lo, kw), :])
            m2 = jnp.broadcast_to(m, (2 * BLK, LANES))
            d2 = jnp.broadcast_to(den, (2 * BLK, LANES))
            rows = pl.ds(q_lo * dil + r, BLK, stride=dil) if dil > 1 else pl.ds(q_lo, BLK)
            og_ref[g, rows, :] = jnp.where(first_head, pv[:BLK], pv[BLK:])
            lg_ref[g, rows, :] = jnp.where(first_head, m2[:BLK], m2[BLK:])
            dg_ref[g, rows, :] = jnp.where(first_head, d2[:BLK], d2[BLK:])

        def trip(i, carry, scores=scores, finish=finish):
            started = [scores(i * ATTN_UNITS + j) for j in range(ATTN_UNITS)]
            for args in started:
                finish(*args)
            return carry

        lax.fori_loop(0, dil * nb // ATTN_UNITS, trip, 0)

    mt = 2 * BLK

    def merge(c, carry):
        rows = pl.ds(pl.multiple_of(c * mt, mt), mt)
        ms = [lg_ref[g, rows, :] for g in range(len(ATTN_GROUPS))]
        mx = jnp.maximum(jnp.maximum(ms[0], ms[1]), ms[2])
        ws = [jnp.exp(m - mx) for m in ms]
        num = ws[0] * og_ref[0, rows, :]
        den = ws[0] * dg_ref[0, rows, :]
        for g in range(1, len(ATTN_GROUPS)):
            num = num + ws[g] * og_ref[g, rows, :]
            den = den + ws[g] * dg_ref[g, rows, :]
        o_ref[0, 0, rows, :] = (num / den).astype(BF16)
        return carry

    lax.fori_loop(0, t // mt, merge, 0)


def _attention(q_l, k_l, v_l, b, t):
    specs = [pl.BlockSpec((1, 1, dil, t // dil, LANES), lambda bb, hp: (bb, hp, 0, 0, 0))
             for _, dil in ATTN_GROUPS]
    return pl.pallas_call(
        _attn_kernel,
        grid=(b, NH // 2),
        in_specs=specs * 3,
        out_specs=pl.BlockSpec((1, 1, t, LANES), lambda bb, hp: (bb, hp, 0, 0)),
        out_shape=jax.ShapeDtypeStruct((b, NH // 2, t, LANES), BF16),
        scratch_shapes=[pltpu.VMEM((len(ATTN_GROUPS), t, LANES), F32)] * 3
        + [pltpu.VMEM((2, 2 * BLK, 2 * BLK), F32)],
        compiler_params=_cparams(("arbitrary", "arbitrary"), 40),
        name="dilated_attn",
    )(*q_l, *k_l, *v_l)


def _router_top2(logits_t):
    idx = lax.broadcasted_iota(I32, logits_t.shape, 0)
    v1 = jnp.max(logits_t, axis=0, keepdims=True)
    i1 = jnp.min(jnp.where(logits_t == v1, idx, N_EXPERTS), axis=0, keepdims=True)
    m1 = idx == i1
    rest = jnp.where(m1, -jnp.inf, logits_t)
    v2 = jnp.max(rest, axis=0, keepdims=True)
    i2 = jnp.min(jnp.where(rest == v2, idx, N_EXPERTS), axis=0, keepdims=True)
    m2 = idx == i2
    e2 = jnp.exp(v2 - v1)
    den = 1.0 + e2
    return jnp.where(m1, 1.0 / den, 0.0) + jnp.where(m2, e2 / den, 0.0)


def _post1_kernel(a_ref, x2_ref, modm_ref, modf_ref, gpost_ref, gpre_ref, wo_ref, rwt_ref, rb_ref,
                  x3_ref, hf_ref, gt_ref):
    o = jnp.concatenate([a_ref[0, hp] for hp in range(NH // 2)], axis=1)
    mix = _dot(o, wo_ref[...])
    _, _, gt = _split3(modm_ref[0, 0])
    x3 = x2_ref[...] + gt * (_rms_scale(mix) * gpost_ref[...])
    x3_ref[...] = x3
    sh, sc, _ = _split3(modf_ref[0, 0])
    hf = _rms_scale(x3) * gpre_ref[...] * (1.0 + sc) + sh
    hf_ref[...] = hf.astype(BF16)
    logits_t = lax.dot_general(rwt_ref[...], hf, (((1,), (1,)), ((), ())),
                               precision=HIGHEST, preferred_element_type=F32) + rb_ref[...]
    gt_ref[...] = _router_top2(logits_t)


def _post1(attn, x2, mods_mix, mods_ffn, gpost, gpre, wo_bf, rwt, rb, t):
    n = x2.shape[0]
    tm = TM_WO
    per_b = t // tm
    row = lambda i: (i, 0)
    fixed = lambda i: (0, 0)
    vec = pl.BlockSpec((1, D), fixed)
    tile = pl.BlockSpec((tm, D), row)
    return pl.pallas_call(
        _post1_kernel,
        grid=(n // tm,),
        in_specs=[pl.BlockSpec((1, NH // 2, tm, LANES), lambda i: (i // per_b, 0, i % per_b, 0)),
                  tile,
                  pl.BlockSpec((1, 1, 1, 3 * D), lambda i: (1, i // per_b, 0, 0)),
                  pl.BlockSpec((1, 1, 1, 3 * D), lambda i: (1, i // per_b, 0, 0)),
                  vec, vec,
                  pl.BlockSpec((D, D), fixed),
                  pl.BlockSpec((N_EXPERTS, D), fixed),
                  pl.BlockSpec((N_EXPERTS, 1), fixed)],
        out_specs=[tile, tile, pl.BlockSpec((N_EXPERTS, tm), lambda i: (0, i))],
        out_shape=[jax.ShapeDtypeStruct((n, D), F32), jax.ShapeDtypeStruct((n, D), BF16),
                   jax.ShapeDtypeStruct((N_EXPERTS, n), F32)],
        compiler_params=_cparams(("arbitrary",), 52),
        name="wo_router",
    )(attn, x2, mods_mix, mods_ffn, gpost, gpre, wo_bf, rwt, rb)


def _route_plan(gates_t, n):
    nb = n // TB
    cnt = jnp.sum((gates_t > 0.0).reshape(N_EXPERTS, nb, TB), axis=-1).astype(I32)
    c_al = (cnt + ROW_ALIGN - 1) // ROW_ALIGN * ROW_ALIGN
    within = jnp.cumsum(c_al, axis=1) - c_al
    tot = jnp.sum(c_al, axis=1)
    region = (tot + SUB + TMX - 1) // TMX * TMX
    start = jnp.cumsum(region) - region
    off = (start[:, None] + within).T.reshape(-1)
    n_used = (jnp.sum(region) // TMX).astype(I32)
    tile_row = jnp.arange(_max_sorted_rows(n) // TMX, dtype=I32) * TMX
    tile_e = jnp.sum(tile_row[:, None] >= (start + region)[None, :], axis=1).astype(I32)
    tile_e = jnp.minimum(tile_e, N_EXPERTS - 1)
    last_e = tile_e[jnp.maximum(n_used - 1, 0)]
    tile_e = jnp.where(jnp.arange(tile_e.shape[0]) < n_used, tile_e, last_e)
    return off.astype(I32), cnt.T.reshape(-1), tile_e, n_used.reshape(1)


def _max_sorted_rows(n):
    nb = n // TB
    worst = 2 * n + nb * N_EXPERTS * (ROW_ALIGN - 1) + N_EXPERTS * (SUB + TMX - 1)
    return (worst + TMX - 1) // TMX * TMX


def _dispatch_kernel(off_ref, cnt_ref, hf_ref, gt_ref, xs_in_ref, gs_in_ref, xs_ref, gs_ref,
                     stage_ref, gstage_ref, sem, gsem, *, block0):
    del xs_in_ref, gs_in_ref
    b = pl.program_id(0) + block0
    hf = hf_ref[...]
    gate = gt_ref[...]
    sel = gate > 0.0
    before = lax.broadcasted_iota(I32, (TB, TB), 0) < lax.broadcasted_iota(I32, (TB, TB), 1)
    rank = _dot(sel.astype(BF16), before.astype(BF16)).astype(I32)
    jj = lax.broadcasted_iota(I32, (SUB, TB), 0)

    def copies(e, s):
        k = e * NS + s
        dst = pl.ds(pl.multiple_of(off_ref[b * N_EXPERTS + e] + s * SUB, ROW_ALIGN), SUB)
        return (pltpu.make_async_copy(stage_ref.at[k], xs_ref.at[dst], sem.at[k]),
                pltpu.make_async_copy(gstage_ref.at[k], gs_ref.at[dst], gsem.at[k]))

    for e in range(N_EXPERTS):
        c = cnt_ref[b * N_EXPERTS + e]
        for s in range(NS):
            @pl.when(s * SUB < c)
            def _(e=e, s=s):
                hit = (rank[e:e + 1, :] == jj + s * SUB) & sel[e:e + 1, :]
                stage_ref[e * NS + s] = _dot(hit.astype(BF16), hf).astype(BF16)
                g_rows = jnp.sum(jnp.where(hit, gate[e:e + 1, :], 0.0), axis=1, keepdims=True)
                gstage_ref[e * NS + s] = jnp.broadcast_to(g_rows, (SUB, LANES))
                for cp in copies(e, s):
                    cp.start()
    for e in range(N_EXPERTS):
        c = cnt_ref[b * N_EXPERTS + e]
        for s in range(NS):
            @pl.when(s * SUB < c)
            def _(e=e, s=s):
                for cp in copies(e, s):
                    cp.wait()


def _dispatch(off, cnt, hf_bf, gates_t, xs, gs, block0):
    n = hf_bf.shape[0]
    rows = xs.shape[0]
    return pl.pallas_call(
        functools.partial(_dispatch_kernel, block0=block0),
        grid_spec=pltpu.PrefetchScalarGridSpec(
            num_scalar_prefetch=2,
            grid=(n // TB,),
            in_specs=[pl.BlockSpec((TB, D), lambda b, o, c: (b, 0)),
                      pl.BlockSpec((N_EXPERTS, TB), lambda b, o, c: (0, b)),
                      pl.BlockSpec(memory_space=pl.ANY),
                      pl.BlockSpec(memory_space=pl.ANY)],
            out_specs=[pl.BlockSpec(memory_space=pl.ANY), pl.BlockSpec(memory_space=pl.ANY)],
            scratch_shapes=[pltpu.VMEM((N_EXPERTS * NS, SUB, D), BF16),
                            pltpu.VMEM((N_EXPERTS * NS, SUB, LANES), F32),
                            pltpu.SemaphoreType.DMA((N_EXPERTS * NS,)),
                            pltpu.SemaphoreType.DMA((N_EXPERTS * NS,))]),
        out_shape=[jax.ShapeDtypeStruct((rows, D), BF16), jax.ShapeDtypeStruct((rows, LANES), F32)],
        input_output_aliases={4: 0, 5: 1},
        compiler_params=_cparams(("arbitrary",), 40),
        name="moe_dispatch",
    )(off, cnt, hf_bf, gates_t, xs, gs)


def _expert_kernel(te_ref, nu_ref, x_ref, g_ref, w1_ref, w3_ref, w2_ref, y_ref, acc_ref):
    del te_ref
    i, j = pl.program_id(0), pl.program_id(1)

    @pl.when(i < nu_ref[0])
    def _():
        x = x_ref[...]
        a = (_silu(_dot(x, w1_ref[0].astype(BF16))) * _dot(x, w3_ref[0].astype(BF16))).astype(BF16)
        contrib = _dot(a, w2_ref[0].astype(BF16))

        @pl.when(j == 0)
        def _():
            acc_ref[...] = contrib

        @pl.when(j > 0)
        def _():
            acc_ref[...] += contrib

        @pl.when(j == pl.num_programs(1) - 1)
        def _():
            y_ref[...] = (g_ref[:, 0:1] * acc_ref[...]).astype(BF16)

    @pl.when((i >= nu_ref[0]) & (j == 0))
    def _():
        y_ref[...] = jnp.zeros(y_ref.shape, BF16)


def _experts(tile_e, n_used, xs, gs, w1, w3, w2):
    rows = xs.shape[0]
    dff = w1.shape[2]
    nj = dff // FCX

    def tile_idx(i, j, te, nu):
        return (jnp.minimum(i, nu[0] - 1), 0)

    def chunk(i, j, nu):
        return jnp.where(i < nu[0], j, nj - 1)

    return pl.pallas_call(
        _expert_kernel,
        grid_spec=pltpu.PrefetchScalarGridSpec(
            num_scalar_prefetch=2,
            grid=(rows // TMX, nj),
            in_specs=[pl.BlockSpec((TMX, D), tile_idx),
                      pl.BlockSpec((TMX, LANES), tile_idx),
                      pl.BlockSpec((1, D, FCX), lambda i, j, te, nu: (te[i], 0, chunk(i, j, nu))),
                      pl.BlockSpec((1, D, FCX), lambda i, j, te, nu: (te[i], 0, chunk(i, j, nu))),
                      pl.BlockSpec((1, FCX, D), lambda i, j, te, nu: (te[i], chunk(i, j, nu), 0))],
            out_specs=pl.BlockSpec((TMX, D), lambda i, j, te, nu: (i, 0)),
            scratch_shapes=[pltpu.VMEM((TMX, D), F32)]),
        out_shape=jax.ShapeDtypeStruct((rows, D), BF16),
        compiler_params=_cparams(("arbitrary", "arbitrary"), 56),
        name="moe_experts",
    )(tile_e, n_used, xs, gs, w1, w3, w2)


SLOTS = 2 * TB // SUB + N_EXPERTS


def _gather_expert_rows(b, first_step, off_ref, cnt_ref, gt_ref, ys_ref, ybuf_ref, p_ref, sem):
    @pl.when(first_step)
    def _():
        ybuf_ref[...] = jnp.zeros(ybuf_ref.shape, BF16)

    def copy(e, s, slot):
        src = pl.multiple_of(off_ref[b * N_EXPERTS + e] + s * SUB, ROW_ALIGN)
        dst = pl.multiple_of(slot * SUB, SUB)
        return pltpu.make_async_copy(ys_ref.at[pl.ds(src, SUB)], ybuf_ref.at[pl.ds(dst, SUB)], sem.at[slot])

    pairs = [(e, s) for e in range(N_EXPERTS) for s in range(NS)]
    used, slots = [], []
    slot = jnp.int32(0)
    for e, s in pairs:
        u = s * SUB < cnt_ref[b * N_EXPERTS + e]
        used.append(u)
        slots.append(slot)
        slot = slot + u.astype(I32)
    for (e, s), u, k in zip(pairs, used, slots):
        @pl.when(u)
        def _(e=e, s=s, k=k):
            copy(e, s, k).start()

    sel = gt_ref[...] > 0.0
    before = lax.broadcasted_iota(I32, (TB, TB), 0) < lax.broadcasted_iota(I32, (TB, TB), 1)
    rank = _dot(sel.astype(BF16), before.astype(BF16)).astype(I32)
    jj = lax.broadcasted_iota(I32, (SUB, TB), 0)
    p_ref[...] = jnp.zeros(p_ref.shape, BF16)
    for (e, s), u, k in zip(pairs, used, slots):
        @pl.when(u)
        def _(e=e, s=s, k=k):
            hit = (rank[e:e + 1, :] == jj + s * SUB) & sel[e:e + 1, :]
            p_ref[pl.ds(pl.multiple_of(k * SUB, SUB), SUB), :] = hit.astype(BF16)
    for (e, s), u, k in zip(pairs, used, slots):
        @pl.when(u)
        def _(e=e, s=s, k=k):
            copy(e, s, k).wait()

    tn = (((0,), (0,)), ((), ()))
    return lax.dot_general(p_ref[...], ybuf_ref[...], tn, preferred_element_type=F32)


_COMBINE_SCRATCH = [pltpu.VMEM((SLOTS * SUB, D), BF16),
                    pltpu.VMEM((SLOTS * SUB, TB), BF16),
                    pltpu.SemaphoreType.DMA((SLOTS,))]


def _combine_kernel(off_ref, cnt_ref, gt_ref, x3_ref, mod_ref, gpost_ref, ys_ref,
                    out_ref, ybuf_ref, p_ref, sem):
    b = pl.program_id(0)
    moe = _gather_expert_rows(b, b == 0, off_ref, cnt_ref, gt_ref, ys_ref, ybuf_ref, p_ref, sem)
    _, _, gt = _split3(mod_ref[0, 0])
    out_ref[...] = x3_ref[...] + gt * (_rms_scale(moe) * gpost_ref[...])


def _combine(off, cnt, gates_t, x3, mods_ffn, gpost, ys, t):
    n = x3.shape[0]
    per_b = t // TB
    return pl.pallas_call(
        _combine_kernel,
        grid_spec=pltpu.PrefetchScalarGridSpec(
            num_scalar_prefetch=2,
            grid=(n // TB,),
            in_specs=[pl.BlockSpec((N_EXPERTS, TB), lambda b, o, c: (0, b)),
                      pl.BlockSpec((TB, D), lambda b, o, c: (b, 0)),
                      pl.BlockSpec((1, 1, 1, 3 * D), lambda b, o, c: (1, b // per_b, 0, 0)),
                      pl.BlockSpec((1, D), lambda b, o, c: (0, 0)),
                      pl.BlockSpec(memory_space=pl.ANY)],
            out_specs=pl.BlockSpec((TB, D), lambda b, o, c: (b, 0)),
            scratch_shapes=_COMBINE_SCRATCH),
        out_shape=jax.ShapeDtypeStruct((n, D), F32),
        compiler_params=_cparams(("arbitrary",), 48),
        name="moe_combine",
    )(off, cnt, gates_t, x3, mods_ffn, gpost, ys)


def _combine_rows_kernel(off_ref, cnt_ref, gt_ref, ys_ref, out_ref, ybuf_ref, p_ref, sem, *, block0):
    out_ref[...] = _gather_expert_rows(block0, True, off_ref, cnt_ref, gt_ref, ys_ref, ybuf_ref, p_ref, sem)


def _combine_rows(off, cnt, gates_t, ys, block0):
    return pl.pallas_call(
        functools.partial(_combine_rows_kernel, block0=block0),
        grid_spec=pltpu.PrefetchScalarGridSpec(
            num_scalar_prefetch=2,
            grid=(1,),
            in_specs=[pl.BlockSpec((N_EXPERTS, TB), lambda b, o, c: (0, 0)),
                      pl.BlockSpec(memory_space=pl.ANY)],
            out_specs=pl.BlockSpec((TB, D), lambda b, o, c: (0, 0)),
            scratch_shapes=_COMBINE_SCRATCH),
        out_shape=jax.ShapeDtypeStruct((TB, D), F32),
        compiler_params=_cparams(("arbitrary",), 48),
        name="moe_combine_rows",
    )(off, cnt, gates_t, ys)


def _s_mix0_kernel(x_ref, st_ref, modm_ref, modf_ref, gpre_ref, gpost_ref, gpref_ref, pw_ref, ps_ref,
                   x1_ref, pool_ref, hf_ref, *, pos):
    sh, sc, gt = _split3(modm_ref[...])
    x = x_ref[...]
    hm = _rms_scale(x) * gpre_ref[...] * (1.0 + sc) + sh
    parts = []
    for g, w in enumerate(POOL_WINDOWS):
        lo = g * PG
        tok = hm[:, lo:lo + PG]
        s = tok
        for j in range(1, w):
            s = s + st_ref[POOL_STATE - j, :, lo:lo + PG]
        cnt = float(min(w, pos + 1))
        parts.append(_dot_hi(s / cnt - tok, pw_ref[g]))
    mix = jnp.concatenate(parts, axis=1) * ps_ref[...]
    x1 = x + gt * (_rms_scale(mix) * gpost_ref[...])
    x1_ref[...] = x1
    for j in range(POOL_STATE - 1):
        pool_ref[j] = st_ref[j + 1]
    pool_ref[POOL_STATE - 1] = hm
    shf, scf, _ = _split3(modf_ref[...])
    hf_ref[...] = _rms_scale(x1) * gpref_ref[...] * (1.0 + scf) + shf


def _s_mix0(x_s, state_t, modm, modf, gpre, gpost, gpref, pw, ps, pos):
    rows = x_s.shape[0]
    return pl.pallas_call(
        functools.partial(_s_mix0_kernel, pos=pos),
        out_shape=[jax.ShapeDtypeStruct((rows, D), F32),
                   jax.ShapeDtypeStruct((POOL_STATE, rows, D), F32),
                   jax.ShapeDtypeStruct((rows, D), F32)],
        compiler_params=pltpu.CompilerParams(vmem_limit_bytes=40 * MIB),
        name="s_mix0",
    )(x_s, state_t, modm, modf, gpre, gpost, gpref, pw, ps)


def _s_ffn_kernel(h_ref, w1_ref, w3_ref, w2_ref, f_ref):
    j = pl.program_id(0)
    h = h_ref[...]
    contrib = _dot_hi(_silu(_dot_hi(h, w1_ref[...])) * _dot_hi(h, w3_ref[...]), w2_ref[...])

    @pl.when(j == 0)
    def _():
        f_ref[...] = contrib

    @pl.when(j > 0)
    def _():
        f_ref[...] += contrib


def _s_ffn(h, w1, w3, w2):
    rows = h.shape[0]
    dff = w1.shape[1]
    fc = 256
    return pl.pallas_call(
        _s_ffn_kernel,
        grid=(dff // fc,),
        in_specs=[pl.BlockSpec((rows, D), lambda j: (0, 0)),
                  pl.BlockSpec((D, fc), lambda j: (0, j)),
                  pl.BlockSpec((D, fc), lambda j: (0, j)),
                  pl.BlockSpec((fc, D), lambda j: (j, 0))],
        out_specs=pl.BlockSpec((rows, D), lambda j: (0, 0)),
        out_shape=jax.ShapeDtypeStruct((rows, D), F32),
        compiler_params=_cparams(("arbitrary",), 32),
        name="s_ffn0",
    )(h, w1, w3, w2)


def _s_post0_kernel(x1_ref, f_ref, modf_ref, modm_ref, gpost_ref, gkv_ref, gpre1_ref,
                    x2_ref, kvin_ref, hm1_ref):
    _, _, gt = _split3(modf_ref[...])
    x2 = x1_ref[...] + gt * (_rms_scale(f_ref[...]) * gpost_ref[...])
    x2_ref[...] = x2
    xn = _rms_scale(x2)
    kvin_ref[...] = xn * gkv_ref[...]
    sh1, sc1, _ = _split3(modm_ref[...])
    hm1_ref[...] = xn * gpre1_ref[...] * (1.0 + sc1) + sh1


def _s_post0(x1, f, modf, modm, gpost, gkv, gpre1):
    rows = x1.shape[0]
    return pl.pallas_call(
        _s_post0_kernel,
        out_shape=[jax.ShapeDtypeStruct((rows, D), F32)] * 3,
        name="s_post0",
    )(x1, f, modf, modm, gpost, gkv, gpre1)


def _s_linear_kernel(h_ref, w_ref, cos_ref, sin_ref, o_ref):
    o_ref[...] = _rope_cols(_dot_hi(h_ref[...], w_ref[...]), cos_ref[...], sin_ref[...])


def _s_linear_plain_kernel(h_ref, w_ref, o_ref):
    o_ref[...] = _dot_hi(h_ref[...], w_ref[...])


def _s_linear(h, w, rope=None):
    rows, k = h.shape
    nw = w.shape[1]
    cw = 512
    in_specs = [pl.BlockSpec((rows, k), lambda j: (0, 0)), pl.BlockSpec((k, cw), lambda j: (0, j))]
    args = [h, w]
    body = _s_linear_plain_kernel
    if rope is not None:
        body = _s_linear_kernel
        in_specs += [pl.BlockSpec((1, cw), lambda j: (0, j)), pl.BlockSpec((1, cw), lambda j: (0, j))]
        args += [rope[0], rope[1]]
    return pl.pallas_call(
        body,
        grid=(nw // cw,),
        in_specs=in_specs,
        out_specs=pl.BlockSpec((rows, cw), lambda j: (0, j)),
        out_shape=jax.ShapeDtypeStruct((rows, nw), F32),
        compiler_params=_cparams(("arbitrary",), 32),
        name="s_linear",
    )(*args)


def _s_attn_kernel(qkv_ref, kc_ref, vc_ref, o_ref):
    n_past = kc_ref.shape[3]
    hb = kc_ref.shape[1]
    ng = len(ATTN_GROUPS)
    nt = (((1,), (1,)), ((), ()))
    eye = (lax.broadcasted_iota(I32, (LANES, LANES), 0) == lax.broadcasted_iota(I32, (LANES, LANES), 1)).astype(F32)
    tiles = jnp.concatenate([qkv_ref[0].reshape(hb * 8, HD), jnp.zeros((LANES - hb * 8, HD), F32)], axis=0)
    cols = lax.dot_general(eye[:HD, :HD], tiles, nt, precision=HIGHEST, preferred_element_type=F32)
    col = lambda h, j: cols[:, 8 * h + j:8 * h + j + 1]
    outs, lses = [], []
    for g, (win, dil) in enumerate(ATTN_GROUPS):
        lo = n_past - win
        dist = win - lax.broadcasted_iota(I32, (1, win), 1)
        s_c = jnp.concatenate([jnp.sum(kc_ref[0, h, :, lo:] * col(h, g), axis=0, keepdims=True)
                               for h in range(hb)], axis=0) * SCALE
        s_c = jnp.where((dist & (dil - 1)) == 0, s_c, -jnp.inf)
        s_n = jnp.concatenate([jnp.sum(col(h, ng) * col(h, g), axis=0, keepdims=True)
                               for h in range(hb)], axis=0) * SCALE
        m = jnp.maximum(jnp.max(s_c, axis=1, keepdims=True), s_n)
        p_c = jnp.exp(s_c - m)
        p_n = jnp.exp(s_n - m)
        den = jnp.sum(p_c, axis=1, keepdims=True) + p_n
        outs.append([(jnp.sum(vc_ref[0, h, :, lo:] * p_c[h:h + 1], axis=1, keepdims=True)
                      + p_n[h:h + 1] * col(h, ng + 1)) / den[h:h + 1] for h in range(hb)])
        lses.append(m + jnp.log(den))
    mx = jnp.maximum(jnp.maximum(lses[0], lses[1]), lses[2])
    es = [jnp.exp(l - mx) for l in lses]
    tot = es[0] + es[1] + es[2]
    lane = lax.broadcasted_iota(I32, (1, LANES), 1)
    out_cols = jnp.zeros((HD, LANES), F32)
    for h in range(hb):
        out = (es[0] / tot)[h:h + 1] * outs[0][h]
        for g in range(1, ng):
            out = out + (es[g] / tot)[h:h + 1] * outs[g][h]
        out_cols = out_cols + out * (lane == h).astype(F32)
    o_ref[0] = lax.dot_general(eye[:hb], out_cols, nt, precision=HIGHEST, preferred_element_type=F32)


def _s_attn(qkv, cache_kt, cache_vt):
    rows, _, _, n_past = cache_kt.shape
    for win, dil in ATTN_GROUPS:
        assert n_past >= win and win % LANES == 0 and dil & (dil - 1) == 0
    cache = pl.BlockSpec((1, HB_S, HD, n_past), lambda b, h: (b, h, 0, 0))
    return pl.pallas_call(
        _s_attn_kernel,
        grid=(rows, NH // HB_S),
        in_specs=[pl.BlockSpec((1, HB_S, 8, HD), lambda b, h: (b, h, 0, 0)), cache, cache],
        out_specs=pl.BlockSpec((1, HB_S, HD), lambda b, h: (b, h, 0)),
        out_shape=jax.ShapeDtypeStruct((rows, NH, HD), F32),
        compiler_params=_cparams(("arbitrary", "arbitrary"), 40),
        name="s_attn",
    )(qkv, cache_kt, cache_vt)


def _s_post1_kernel(x2_ref, mix_ref, modm_ref, modf_ref, gpost_ref, gpre_ref, rwt_ref, rb_ref,
                    x3_ref, hf_ref, gc_ref):
    _, _, gt = _split3(modm_ref[...])
    x3 = x2_ref[...] + gt * (_rms_scale(mix_ref[...]) * gpost_ref[...])
    x3_ref[...] = x3
    sh, sc, _ = _split3(modf_ref[...])
    hf = _rms_scale(x3) * gpre_ref[...] * (1.0 + sc) + sh
    hf_ref[...] = hf
    logits_t = lax.dot_general(rwt_ref[...], hf, (((1,), (1,)), ((), ())),
                               precision=HIGHEST, preferred_element_type=F32) + rb_ref[...]
    gc_ref[...] = _router_top2(logits_t)


def _s_post1(x2, mix, modm, modf, gpost, gpre, rwt, rb):
    rows = x2.shape[0]
    return pl.pallas_call(
        _s_post1_kernel,
        out_shape=[jax.ShapeDtypeStruct((rows, D), F32), jax.ShapeDtypeStruct((rows, D), F32),
                   jax.ShapeDtypeStruct((N_EXPERTS, rows), F32)],
        name="s_post1",
    )(x2, mix, modm, modf, gpost, gpre, rwt, rb)


def _s_post2_kernel(x3_ref, f_ref, modf_ref, gpost_ref, y_ref):
    _, _, gt = _split3(modf_ref[...])
    y_ref[...] = x3_ref[...] + gt * (_rms_scale(f_ref[...]) * gpost_ref[...])


def _s_post2(x3, f, modf, gpost):
    return pl.pallas_call(
        _s_post2_kernel,
        out_shape=jax.ShapeDtypeStruct(x3.shape, F32),
        name="s_post2",
    )(x3, f, modf, gpost)


def _rope_tables(pos, paired=False):
    half = HD // 2
    inv = ROPE_THETA ** (-jnp.arange(half, dtype=F32) / half)
    ang = pos.astype(F32)[:, None] * inv[None, :]
    cos, sin = jnp.cos(ang), jnp.sin(ang)
    cos = jnp.tile(cos, (1, LANES // half))
    if paired:
        sin = jnp.concatenate([-sin, -sin, sin, sin], axis=1)
    else:
        sin = jnp.tile(jnp.concatenate([-sin, sin], axis=1), (1, LANES // HD))
    return cos, sin


def kernel(x_prompt, x_sample, state_pool, cache_k, cache_v, c_prompt, c_sample, ada_mix_w, ada_mix_b, g_pre_mix, g_post_mix, pool_w, pool_scale, g_kv, w_kv, w_q, w_o, ada_ffn_w, ada_ffn_b, g_pre_ffn, g_post_ffn, ffn_w1, ffn_w3, ffn_w2, router_w, router_b, moe_w1, moe_w3, moe_w2):
    b, t, _ = x_prompt.shape
    nsmp = x_sample.shape[0]
    n = b * t
    past_len = 16384
    assert x_sample.shape[1] == 1 and g_pre_mix.shape[0] == 2 and t % (ATTN_GROUPS[-1][1] * BLK) == 0
    vec = lambda a: a.reshape(1, D)

    c_all = jnp.concatenate([c_prompt, c_sample], axis=0)
    mods_mix = _ada(c_all, ada_mix_w, ada_mix_b)
    mods_ffn = _ada(c_all, ada_ffn_w, ada_ffn_b)
    pm_mix = mods_mix[:, :b].reshape(2, b, 1, 3 * D)
    pm_ffn = mods_ffn[:, :b].reshape(2, b, 1, 3 * D)
    sm_mix = mods_mix[:, b:]
    sm_ffn = mods_ffn[:, b:]

    rwt = router_w[0].T
    rb = router_b[0].reshape(N_EXPERTS, 1)

    x1, pool16 = _mix0(x_prompt, pm_mix, vec(g_pre_mix[0]), vec(g_post_mix[0]),
                       pool_w[0].astype(BF16), vec(pool_scale[0]))
    cos_p, sin_p = _rope_tables(jnp.arange(t, dtype=jnp.int32), paired=True)
    x2 = _ffn0(x1.reshape(n, D), pm_ffn, vec(g_pre_ffn[0]), vec(g_post_ffn[0]),
               ffn_w1[0].astype(BF16), ffn_w3[0].astype(BF16), ffn_w2[0].astype(BF16), t)
    wkv_bf = jnp.concatenate([_pair_heads(w_kv[:, :D]), w_kv[:, D:]], axis=1).astype(BF16)
    kt_p, vt_p, *kv_l = _kvproj(x2, vec(g_kv), cos_p, sin_p, wkv_bf, b, t)
    q_l = _qproj(x2, pm_mix, vec(g_pre_mix[1]), cos_p, sin_p, _pair_heads(w_q[0]).astype(BF16), b, t)
    attn = _attention(q_l, kv_l[:3], kv_l[3:], b, t)
    x3, hf1, gates_t = _post1(attn, x2, pm_mix, pm_ffn, vec(g_post_mix[1]),
                              vec(g_pre_ffn[1]), w_o[0].astype(BF16), rwt, rb, t)

    state_t = jnp.transpose(state_pool[:, 0], (1, 0, 2))
    x1s, pool_t, hf0s = _s_mix0(x_sample.reshape(nsmp, D), state_t, sm_mix[0], sm_ffn[0],
                                vec(g_pre_mix[0]), vec(g_post_mix[0]), vec(g_pre_ffn[0]),
                                pool_w[0], vec(pool_scale[0]), past_len)
    f0s = _s_ffn(hf0s, ffn_w1[0], ffn_w3[0], ffn_w2[0])
    x2s, kvin_s, hm1s = _s_post0(x1s, f0s, sm_ffn[0], sm_mix[1], vec(g_post_ffn[0]), vec(g_kv), vec(g_pre_mix[1]))
    cos_s, sin_s = _rope_tables(jnp.full((1,), past_len, jnp.int32))
    ones, zeros = jnp.ones((1, D), F32), jnp.zeros((1, D), F32)
    kv_s = _s_linear(kvin_s, w_kv, (jnp.concatenate([jnp.tile(cos_s, (1, D // LANES)), ones], axis=1),
                                    jnp.concatenate([jnp.tile(sin_s, (1, D // LANES)), zeros], axis=1)))
    q_s = _s_linear(hm1s, w_q[0], (jnp.tile(cos_s, (1, 3 * D // LANES)), jnp.tile(sin_s, (1, 3 * D // LANES))))
    k_s = kv_s[:, :D].reshape(nsmp, NH, HD)
    v_s = kv_s[:, D:].reshape(nsmp, NH, HD)
    cache_kt = jnp.transpose(cache_k, (0, 2, 3, 1))
    cache_vt = jnp.transpose(cache_v, (0, 2, 3, 1))
    qkv = jnp.concatenate([jnp.transpose(q_s.reshape(nsmp, len(ATTN_GROUPS), NH, HD), (0, 2, 1, 3)),
                           k_s[:, :, None], v_s[:, :, None]], axis=2)
    qkv = jnp.pad(qkv, ((0, 0), (0, 0), (0, 8 - qkv.shape[2]), (0, 0)))
    attn_s = _s_attn(qkv, cache_kt, cache_vt).reshape(nsmp, D)
    mix1s = _s_linear(attn_s, w_o[0])
    x3s, hf1s, gts = _s_post1(x2s, mix1s, sm_mix[1], sm_ffn[1], vec(g_post_mix[1]), vec(g_pre_ffn[1]), rwt, rb)

    hf_blk = jnp.zeros((TB, D), BF16).at[:nsmp].set(hf1s.astype(BF16))
    gt_blk = jnp.zeros((N_EXPERTS, TB), F32).at[:, :nsmp].set(gts)
    off, cnt, tile_e, n_used = _route_plan(jnp.concatenate([gates_t, gt_blk], axis=1), n + TB)
    xs = jnp.zeros((_max_sorted_rows(n + TB), D), BF16)
    gs = jnp.zeros((xs.shape[0], LANES), F32)
    xs, gs = _dispatch(off, cnt, hf1, gates_t, xs, gs, 0)
    xs, gs = _dispatch(off, cnt, hf_blk, gt_blk, xs, gs, n // TB)
    ys = _experts(tile_e, n_used, xs, gs, moe_w1[0], moe_w3[0], moe_w2[0])
    y_prompt = _combine(off, cnt, gates_t, x3, pm_ffn, vec(g_post_ffn[1]), ys, t).reshape(b, t, D)
    moe_s = _combine_rows(off, cnt, gt_blk, ys, n // TB)[:nsmp]
    y_sample = _s_post2(x3s, moe_s, sm_ffn[1], vec(g_post_ffn[1])).reshape(nsmp, 1, D)

    pool_prompt = pool16[:, None, 1:, :]
    pool_sample = jnp.transpose(pool_t, (1, 0, 2))[:, None]
    return (y_prompt, y_sample, pool_prompt, pool_sample,
            jnp.transpose(kt_p, (0, 3, 1, 2)), jnp.transpose(vt_p, (0, 3, 1, 2)),
            k_s[:, None], v_s[:, None])
```

```python
import functools

import jax
import jax.numpy as jnp
import numpy as np
from jax import lax
from jax.experimental import pallas as pl
from jax.experimental.pallas import tpu as pltpu

F32, BF16, I32 = jnp.float32, jnp.bfloat16, jnp.int32
HIGHEST = lax.Precision.HIGHEST

D = 1024
POOL_WINDOWS = (2, 4, 8, 16)
PG = D // len(POOL_WINDOWS)
POOL_STATE = max(POOL_WINDOWS) - 1
POOL_ROWS = POOL_STATE + 1
HALO = 2 * POOL_ROWS
ATTN_GROUPS = ((128, 1), (512, 4), (2048, 16))
NH = 16
HD = 64
BLK = 128
ROPE_THETA = 10000.0
N_EXPERTS = 8
EPS = 1e-6
SCALE = HD ** -0.5
LANES = 128
MIB = 1024 * 1024

TT_MIX = 512
TM_FFN = 1024
HB_S = 16
ATTN_UNITS = 8
FC_FFN = 256
TM_POST = 512
TM_WO = 1024
TB = 512
SUB = 128
NS = TB // SUB
TMX = 1024
FCX = 512
FCX_SUB = 256
ROW_ALIGN = 16


def _cparams(sem, vmem_mib):
    return pltpu.CompilerParams(dimension_semantics=sem, vmem_limit_bytes=vmem_mib * MIB)


def _rms_scale(x):
    return x * lax.rsqrt(jnp.mean(x * x, axis=-1, keepdims=True) + EPS)


def _split3(m):
    return m[:, :D], m[:, D:2 * D], m[:, 2 * D:]


def _silu(x):
    return x * jax.nn.sigmoid(x)


def _rope_cols(x, cos, sin):
    lane = lax.broadcasted_iota(I32, (x.shape[0], LANES), 1)
    first = (lane % HD) < (HD // 2)
    shared = cos.shape[1] == LANES
    outs = []
    for c in range(x.shape[1] // LANES):
        cols = slice(c * LANES, (c + 1) * LANES)
        xc = x[:, cols]
        swapped = jnp.where(first, pltpu.roll(xc, LANES - HD // 2, 1), pltpu.roll(xc, HD // 2, 1))
        outs.append(xc * (cos if shared else cos[:, cols]) + swapped * (sin if shared else sin[:, cols]))
    return jnp.concatenate(outs, axis=1)


def _rope_paired(x, cos, sin):
    outs = []
    for c in range(x.shape[1] // LANES):
        xc = x[:, c * LANES:(c + 1) * LANES]
        outs.append(xc * cos + pltpu.roll(xc, LANES // 2, 1) * sin)
    return jnp.concatenate(outs, axis=1)


def _pair_heads(w):
    k, n = w.shape
    return w.reshape(k, n // LANES, 2, 2, HD // 2).transpose(0, 1, 3, 2, 4).reshape(k, n)


def _dot_hi(a, b):
    return jnp.dot(a, b, precision=HIGHEST, preferred_element_type=F32)


def _dot(a, b):
    return jnp.dot(a, b, preferred_element_type=F32)


def _ada_kernel(c_ref, w_ref, b_ref, o_ref):
    o_ref[0] = _dot_hi(_silu(c_ref[...]), w_ref[0]) + b_ref[0]


def _ada(c_all, w, b):
    nl, _, n3 = w.shape
    rows = c_all.shape[0]
    cw = 768
    return pl.pallas_call(
        _ada_kernel,
        grid=(nl, n3 // cw),
        in_specs=[pl.BlockSpec((rows, D), lambda l, j: (0, 0)),
                  pl.BlockSpec((1, D, cw), lambda l, j: (l, 0, j)),
                  pl.BlockSpec((1, 1, cw), lambda l, j: (l, 0, j))],
        out_specs=pl.BlockSpec((1, rows, cw), lambda l, j: (l, 0, j)),
        out_shape=jax.ShapeDtypeStruct((nl, rows, n3), F32),
        compiler_params=_cparams(("arbitrary", "arbitrary"), 32),
        name="ada_mod",
    )(c_all, w, b.reshape(nl, 1, n3))


def _mix0_kernel(x_ref, xh_ref, mod_ref, gpre_ref, gpost_ref, pw_ref, ps_ref,
                 x1_ref, pool_ref, ext_ref, lv_ref):
    i = pl.program_id(1)
    tt = x_ref.shape[1]
    sh, sc, gt = _split3(mod_ref[0, 0])
    gpre = gpre_ref[...]
    x = x_ref[0]
    hm = _rms_scale(x) * gpre * (1.0 + sc) + sh
    hh = _rms_scale(xh_ref[0]) * gpre * (1.0 + sc) + sh
    ext_ref[0:HALO, :] = jnp.where(i > 0, hh, 0.0)
    ext_ref[HALO:, :] = hm
    t = i * tt + lax.broadcasted_iota(I32, (tt, 1), 0)
    rows = tt + HALO
    parts = []
    for g, w in enumerate(POOL_WINDOWS):
        lo = g * PG
        tok = hm[:, lo:lo + PG]
        levels = w.bit_length() - 1
        assert w == 1 << levels
        first = [HALO]
        for k in range(levels - 1, 0, -1):
            first.insert(0, (first[0] - (1 << k)) // 8 * 8)
        for k in range(1, levels + 1):
            r0, back = first[k - 1], 1 << (k - 1)
            if k == 1:
                s = ext_ref[r0:rows, lo:lo + PG] + ext_ref[r0 - back:rows - back, lo:lo + PG]
            else:
                prev = lv_ref.at[k % 2]
                s = prev[r0:rows, :] + prev[r0 - back:rows - back, :]
            if k < levels:
                lv_ref[(k + 1) % 2, r0:rows, :] = s
        inv_cnt = 1.0 / jnp.minimum(w, t + 1).astype(F32)
        parts.append(_dot((s * inv_cnt - tok).astype(BF16), pw_ref[g]))
    mix = jnp.concatenate(parts, axis=1) * ps_ref[...]
    x1_ref[0] = x + gt * (_rms_scale(mix) * gpost_ref[...])

    @pl.when(i == pl.num_programs(1) - 1)
    def _():
        pool_ref[0] = ext_ref[rows - POOL_ROWS:rows, :]


def _mix0(x, mods, gpre, gpost, pw_bf, ps):
    b, t, _ = x.shape
    tt = TT_MIX
    hb = tt // HALO
    return pl.pallas_call(
        _mix0_kernel,
        grid=(b, t // tt),
        in_specs=[pl.BlockSpec((1, tt, D), lambda bb, i: (bb, i, 0)),
                  pl.BlockSpec((1, HALO, D), lambda bb, i: (bb, jnp.maximum(i * hb - 1, 0), 0)),
                  pl.BlockSpec((1, 1, 1, 3 * D), lambda bb, i: (0, bb, 0, 0)),
                  pl.BlockSpec((1, D), lambda bb, i: (0, 0)),
                  pl.BlockSpec((1, D), lambda bb, i: (0, 0)),
                  pl.BlockSpec((len(POOL_WINDOWS), PG, PG), lambda bb, i: (0, 0, 0)),
                  pl.BlockSpec((1, D), lambda bb, i: (0, 0))],
        out_specs=[pl.BlockSpec((1, tt, D), lambda bb, i: (bb, i, 0)),
                   pl.BlockSpec((1, POOL_ROWS, D), lambda bb, i: (bb, 0, 0))],
        out_shape=[jax.ShapeDtypeStruct((b, t, D), F32),
                   jax.ShapeDtypeStruct((b, POOL_ROWS, D), F32)],
        scratch_shapes=[pltpu.VMEM((tt + HALO, D), F32), pltpu.VMEM((2, tt + HALO, PG), F32)],
        compiler_params=_cparams(("arbitrary", "arbitrary"), 40),
        name="mix0",
    )(x, x, mods, gpre, gpost, pw_bf, ps)


def _store_dilated(val, plane_ref, dils, out_refs):
    tm = val.shape[0]
    for hp in range(NH // 2):
        plane_ref[hp] = val[:, hp * LANES:(hp + 1) * LANES]
    for dil, out_ref in zip(dils, out_refs):
        for hp in range(NH // 2):
            for r in range(dil):
                out_ref[0, hp, r] = plane_ref[hp, pl.ds(r, tm // dil, stride=dil), :].astype(BF16)


def _dilated_specs(b, t, tm):
    per_b = t // tm
    specs = [pl.BlockSpec((1, NH // 2, dil, tm // dil, LANES), lambda i: (i // per_b, 0, 0, i % per_b, 0))
             for _, dil in ATTN_GROUPS]
    shapes = [jax.ShapeDtypeStruct((b, NH // 2, dil, t // dil, LANES), BF16) for _, dil in ATTN_GROUPS]
    return specs, shapes


def _ffn0_kernel(x1_ref, modf_ref, gpre_ref, gpost_ref, w1_ref, w3_ref, w2_ref, x2_ref, acc_ref):
    sh, sc, gt = _split3(modf_ref[0, 0])
    x1 = x1_ref[...]
    hf = (_rms_scale(x1) * gpre_ref[...] * (1.0 + sc) + sh).astype(BF16)
    for c in range(w1_ref.shape[1] // FC_FFN):
        cs = slice(c * FC_FFN, (c + 1) * FC_FFN)
        a = (_silu(_dot(hf, w1_ref[:, cs])) * _dot(hf, w3_ref[:, cs])).astype(BF16)
        contrib = _dot(a, w2_ref[cs, :])
        if c == 0:
            acc_ref[...] = contrib
        else:
            acc_ref[...] += contrib
    x2 = x1 + gt * (_rms_scale(acc_ref[...]) * gpost_ref[...])
    x2_ref[...] = x2


def _ffn0(x1, mods_ffn, gpre, gpost, w1, w3, w2, t):
    n = x1.shape[0]
    tm = TM_FFN
    per_b = t // tm
    dff = w1.shape[1]
    row = lambda i: (i, 0)
    fixed = lambda i: (0, 0)
    resident = lambda shape: pl.BlockSpec(shape, fixed, pipeline_mode=pl.Buffered(1))
    vec = pl.BlockSpec((1, D), fixed)
    return pl.pallas_call(
        _ffn0_kernel,
        grid=(n // tm,),
        in_specs=[pl.BlockSpec((tm, D), row),
                  pl.BlockSpec((1, 1, 1, 3 * D), lambda i: (0, i // per_b, 0, 0)),
                  vec, vec,
                  resident((D, dff)), resident((D, dff)), resident((dff, D))],
        out_specs=pl.BlockSpec((tm, D), row),
        out_shape=jax.ShapeDtypeStruct((n, D), F32),
        scratch_shapes=[pltpu.VMEM((tm, D), F32)],
        compiler_params=_cparams(("arbitrary",), 56),
        name="ffn0",
    )(x1, mods_ffn, gpre, gpost, w1, w3, w2)


def _kvproj_kernel(x2_ref, gkv_ref, cos_ref, sin_ref, wkv_ref,
                   kt_ref, vt_ref, k0_ref, k1_ref, k2_ref, v0_ref, v1_ref, v2_ref, plane_ref):
    tm = x2_ref.shape[0]
    kv = _dot((_rms_scale(x2_ref[...]) * gkv_ref[...]).astype(BF16), wkv_ref[...])
    k = _rope_paired(kv[:, :D], cos_ref[...], sin_ref[...])
    v = kv[:, D:]
    kt = k.T
    half = HD // 2
    for p in range(NH // 2):
        for j, (head, lo) in enumerate(((2 * p, 0), (2 * p + 1, 0), (2 * p, half), (2 * p + 1, half))):
            kt_ref[0, head, lo:lo + half, :] = kt[p * LANES + j * half:p * LANES + (j + 1) * half, :]
    vt_ref[0] = v.T.reshape(NH, HD, tm)
    dils = [dil for _, dil in ATTN_GROUPS]
    _store_dilated(k, plane_ref, dils, (k0_ref, k1_ref, k2_ref))
    _store_dilated(v, plane_ref, dils, (v0_ref, v1_ref, v2_ref))


def _kvproj(x2, gkv, cos, sin, wkv, b, t):
    n = x2.shape[0]
    tm = TM_POST
    per_b = t // tm
    fixed = lambda i: (0, 0)
    dspecs, dshapes = _dilated_specs(b, t, tm)
    tspec = pl.BlockSpec((1, NH, HD, tm), lambda i: (i // per_b, 0, 0, i % per_b))
    tshape = jax.ShapeDtypeStruct((b, NH, HD, t), F32)
    return pl.pallas_call(
        _kvproj_kernel,
        grid=(n // tm,),
        in_specs=[pl.BlockSpec((tm, D), lambda i: (i, 0)),
                  pl.BlockSpec((1, D), fixed),
                  pl.BlockSpec((tm, LANES), lambda i: (i % per_b, 0)),
                  pl.BlockSpec((tm, LANES), lambda i: (i % per_b, 0)),
                  pl.BlockSpec((D, 2 * D), fixed, pipeline_mode=pl.Buffered(1))],
        out_specs=[tspec, tspec] + dspecs + dspecs,
        out_shape=[tshape, tshape] + dshapes + dshapes,
        scratch_shapes=[pltpu.VMEM((NH // 2, tm, LANES), F32)],
        compiler_params=_cparams(("arbitrary",), 48),
        name="kv_proj",
    )(x2, gkv, cos, sin, wkv)


def _qproj_kernel(x2_ref, modm_ref, gpre1_ref, cos_ref, sin_ref, wq_ref, q0_ref, q1_ref, q2_ref, scr_ref):
    sh1, sc1, _ = _split3(modm_ref[0, 0])
    hm1 = (_rms_scale(x2_ref[...]) * gpre1_ref[...] * (1.0 + sc1) + sh1).astype(BF16)
    q = _rope_paired(_dot(hm1, wq_ref[...]), cos_ref[...], sin_ref[...]) * SCALE
    for g, ((_, dil), q_ref) in enumerate(zip(ATTN_GROUPS, (q0_ref, q1_ref, q2_ref))):
        _store_dilated(q[:, g * D:(g + 1) * D], scr_ref, (dil,), (q_ref,))


def _qproj(x2, mods_mix, gpre1, cos, sin, wq, b, t):
    n = x2.shape[0]
    tm = TM_POST
    per_b = t // tm
    fixed = lambda i: (0, 0)
    dspecs, dshapes = _dilated_specs(b, t, tm)
    return pl.pallas_call(
        _qproj_kernel,
        grid=(n // tm,),
        in_specs=[pl.BlockSpec((tm, D), lambda i: (i, 0)),
                  pl.BlockSpec((1, 1, 1, 3 * D), lambda i: (1, i // per_b, 0, 0)),
                  pl.BlockSpec((1, D), fixed),
                  pl.BlockSpec((tm, LANES), lambda i: (i % per_b, 0)),
                  pl.BlockSpec((tm, LANES), lambda i: (i % per_b, 0)),
                  pl.BlockSpec((D, 3 * D), fixed, pipeline_mode=pl.Buffered(1))],
        out_specs=dspecs,
        out_shape=dshapes,
        scratch_shapes=[pltpu.VMEM((NH // 2, tm, LANES), F32)],
        compiler_params=_cparams(("arbitrary",), 48),
        name="q_proj",
    )(x2, mods_mix, gpre1, cos, sin, wq)


def _attn_kernel(q0_ref, q1_ref, q2_ref, k0_ref, k1_ref, k2_ref, v0_ref, v1_ref, v2_ref,
                 o_ref, og_ref, lg_ref, dg_ref, bias_ref):
    t = o_ref.shape[2]
    lane = lax.broadcasted_iota(I32, (BLK, LANES), 1)
    first_head = lane < HD
    first_head_qk = lane % HD < HD // 2
    nt = (((1,), (1,)), ((), ()))

    @pl.when((pl.program_id(0) == 0) & (pl.program_id(1) == 0))
    def _():
        qi = lax.broadcasted_iota(I32, (2 * BLK, 2 * BLK), 0) % BLK
        kj = lax.broadcasted_iota(I32, (2 * BLK, 2 * BLK), 1)
        for v in range(2):
            dist = v * BLK + qi - kj
            bias_ref[v] = jnp.where((dist >= 0) & (dist <= BLK), 0.0, -jnp.inf)

    groups = zip(ATTN_GROUPS, (q0_ref, q1_ref, q2_ref), (k0_ref, k1_ref, k2_ref), (v0_ref, v1_ref, v2_ref))
    for g, ((win, dil), q_ref, k_ref, v_ref) in enumerate(groups):
        assert win // dil == BLK
        sub = t // dil
        nb = sub // BLK
        kw = min(2 * BLK, sub)

        def scores(u, nb=nb, kw=kw, q_ref=q_ref, k_ref=k_ref):
            r = u // nb
            n = u % nb
            q_lo = pl.multiple_of(n * BLK, BLK)
            k_lo = pl.multiple_of(jnp.maximum(n - 1, 0) * BLK, BLK)
            qp = q_ref[0, 0, r, pl.ds(q_lo, BLK), :]
            zero = jnp.zeros_like(qp)
            qs = jnp.concatenate([jnp.where(first_head_qk, qp, zero), jnp.where(first_head_qk, zero, qp)], axis=0)
            s = lax.dot_general(qs, k_ref[0, 0, r, pl.ds(k_lo, kw), :], nt, preferred_element_type=F32)
            return s + bias_ref[jnp.minimum(n, 1), :, 0:kw], r, q_lo, k_lo

        def finish(s, r, q_lo, k_lo, dil=dil, kw=kw, v_ref=v_ref, g=g):
            m = jnp.max(s, axis=-1, keepdims=True)
            p = jnp.exp(s - m)
            den = jnp.sum(p, axis=-1, keepdims=True)
            pv = _dot(p.astype(BF16), v_ref[0, 0, r, pl.ds(k_lo, kw), :])
            m2 = jnp.broadcast_to(m, (2 * BLK, LANES))
            d2 = jnp.broadcast_to(den, (2 * BLK, LANES))
            rows = pl.ds(q_lo * dil + r, BLK, stride=dil) if dil > 1 else pl.ds(q_lo, BLK)
            og_ref[g, rows, :] = jnp.where(first_head, pv[:BLK], pv[BLK:])
            lg_ref[g, rows, :] = jnp.where(first_head, m2[:BLK], m2[BLK:])
            dg_ref[g, rows, :] = jnp.where(first_head, d2[:BLK], d2[BLK:])

        def trip(i, carry, scores=scores, finish=finish):
            started = [scores(i * ATTN_UNITS + j) for j in range(ATTN_UNITS)]
            for args in started:
                finish(*args)
            return carry

        lax.fori_loop(0, dil * nb // ATTN_UNITS, trip, 0)

    mt = 2 * BLK

    def merge(c, carry):
        rows = pl.ds(pl.multiple_of(c * mt, mt), mt)
        ms = [lg_ref[g, rows, :] for g in range(len(ATTN_GROUPS))]
        mx = jnp.maximum(jnp.maximum(ms[0], ms[1]), ms[2])
        ws = [jnp.exp(m - mx) for m in ms]
        num = ws[0] * og_ref[0, rows, :]
        den = ws[0] * dg_ref[0, rows, :]
        for g in range(1, len(ATTN_GROUPS)):
            num = num + ws[g] * og_ref[g, rows, :]
            den = den + ws[g] * dg_ref[g, rows, :]
        o_ref[0, 0, rows, :] = (num / den).astype(BF16)
        return carry

    lax.fori_loop(0, t // mt, merge, 0)


def _attention(q_l, k_l, v_l, b, t):
    specs = [pl.BlockSpec((1, 1, dil, t // dil, LANES), lambda bb, hp: (bb, hp, 0, 0, 0))
             for _, dil in ATTN_GROUPS]
    return pl.pallas_call(
        _attn_kernel,
        grid=(b, NH // 2),
        in_specs=specs * 3,
        out_specs=pl.BlockSpec((1, 1, t, LANES), lambda bb, hp: (bb, hp, 0, 0)),
        out_shape=jax.ShapeDtypeStruct((b, NH // 2, t, LANES), BF16),
        scratch_shapes=[pltpu.VMEM((len(ATTN_GROUPS), t, LANES), F32)] * 3
        + [pltpu.VMEM((2, 2 * BLK, 2 * BLK), F32)],
        compiler_params=_cparams(("arbitrary", "arbitrary"), 40),
        name="dilated_attn",
    )(*q_l, *k_l, *v_l)


def _router_top2(logits_t):
    idx = lax.broadcasted_iota(I32, logits_t.shape, 0)
    v1 = jnp.max(logits_t, axis=0, keepdims=True)
    i1 = jnp.min(jnp.where(logits_t == v1, idx, N_EXPERTS), axis=0, keepdims=True)
    m1 = idx == i1
    rest = jnp.where(m1, -jnp.inf, logits_t)
    v2 = jnp.max(rest, axis=0, keepdims=True)
    i2 = jnp.min(jnp.where(rest == v2, idx, N_EXPERTS), axis=0, keepdims=True)
    m2 = idx == i2
    e2 = jnp.exp(v2 - v1)
    den = 1.0 + e2
    return jnp.where(m1, 1.0 / den, 0.0) + jnp.where(m2, e2 / den, 0.0)


def _post1_kernel(a_ref, x2_ref, modm_ref, modf_ref, gpost_ref, gpre_ref, wo_ref, rwt_ref, rb_ref,
                  x3_ref, hf_ref, gt_ref):
    o = jnp.concatenate([a_ref[0, hp] for hp in range(NH // 2)], axis=1)
    mix = _dot(o, wo_ref[...])
    _, _, gt = _split3(modm_ref[0, 0])
    x3 = x2_ref[...] + gt * (_rms_scale(mix) * gpost_ref[...])
    x3_ref[...] = x3
    sh, sc, _ = _split3(modf_ref[0, 0])
    hf = _rms_scale(x3) * gpre_ref[...] * (1.0 + sc) + sh
    hf_ref[...] = hf.astype(BF16)
    logits_t = lax.dot_general(rwt_ref[...], hf, (((1,), (1,)), ((), ())),
                               precision=HIGHEST, preferred_element_type=F32) + rb_ref[...]
    gt_ref[...] = _router_top2(logits_t)


def _post1(attn, x2, mods_mix, mods_ffn, gpost, gpre, wo_bf, rwt, rb, t):
    n = x2.shape[0]
    tm = TM_WO
    per_b = t // tm
    row = lambda i: (i, 0)
    fixed = lambda i: (0, 0)
    vec = pl.BlockSpec((1, D), fixed)
    tile = pl.BlockSpec((tm, D), row)
    return pl.pallas_call(
        _post1_kernel,
        grid=(n // tm,),
        in_specs=[pl.BlockSpec((1, NH // 2, tm, LANES), lambda i: (i // per_b, 0, i % per_b, 0)),
                  tile,
                  pl.BlockSpec((1, 1, 1, 3 * D), lambda i: (1, i // per_b, 0, 0)),
                  pl.BlockSpec((1, 1, 1, 3 * D), lambda i: (1, i // per_b, 0, 0)),
                  vec, vec,
                  pl.BlockSpec((D, D), fixed),
                  pl.BlockSpec((N_EXPERTS, D), fixed),
                  pl.BlockSpec((N_EXPERTS, 1), fixed)],
        out_specs=[tile, tile, pl.BlockSpec((N_EXPERTS, tm), lambda i: (0, i))],
        out_shape=[jax.ShapeDtypeStruct((n, D), F32), jax.ShapeDtypeStruct((n, D), BF16),
                   jax.ShapeDtypeStruct((N_EXPERTS, n), F32)],
        compiler_params=_cparams(("arbitrary",), 52),
        name="wo_router",
    )(attn, x2, mods_mix, mods_ffn, gpost, gpre, wo_bf, rwt, rb)


def _route_plan(gates_t, n):
    nb = n // TB
    cnt = jnp.sum((gates_t > 0.0).reshape(N_EXPERTS, nb, TB), axis=-1).astype(I32)
    c_al = (cnt + ROW_ALIGN - 1) // ROW_ALIGN * ROW_ALIGN
    within = jnp.cumsum(c_al, axis=1) - c_al
    tot = jnp.sum(c_al, axis=1)
    region = (tot + SUB + TMX - 1) // TMX * TMX
    start = jnp.cumsum(region) - region
    off = (start[:, None] + within).T.reshape(-1)
    n_used = (jnp.sum(region) // TMX).astype(I32)
    tile_row = jnp.arange(_max_sorted_rows(n) // TMX, dtype=I32) * TMX
    tile_e = jnp.sum(tile_row[:, None] >= (start + region)[None, :], axis=1).astype(I32)
    tile_e = jnp.minimum(tile_e, N_EXPERTS - 1)
    last_e = tile_e[jnp.maximum(n_used - 1, 0)]
    tile_e = jnp.where(jnp.arange(tile_e.shape[0]) < n_used, tile_e, last_e)
    return off.astype(I32), cnt.T.reshape(-1), tile_e, n_used.reshape(1)


def _max_sorted_rows(n):
    nb = n // TB
    worst = 2 * n + nb * N_EXPERTS * (ROW_ALIGN - 1) + N_EXPERTS * (SUB + TMX - 1)
    return (worst + TMX - 1) // TMX * TMX


def _dispatch_kernel(off_ref, cnt_ref, hf_ref, gt_ref, xs_in_ref, gs_in_ref, xs_ref, gs_ref,
                     stage_ref, gstage_ref, sem, gsem, *, block0):
    del xs_in_ref, gs_in_ref
    b = pl.program_id(0) + block0
    hf = hf_ref[...]
    gate = gt_ref[...]
    sel = gate > 0.0
    before = lax.broadcasted_iota(I32, (TB, TB), 0) < lax.broadcasted_iota(I32, (TB, TB), 1)
    rank = _dot(sel.astype(BF16), before.astype(BF16)).astype(I32)
    jj = lax.broadcasted_iota(I32, (SUB, TB), 0)

    def copies(e, s):
        k = e * NS + s
        dst = pl.ds(pl.multiple_of(off_ref[b * N_EXPERTS + e] + s * SUB, ROW_ALIGN), SUB)
        return (pltpu.make_async_copy(stage_ref.at[k], xs_ref.at[dst], sem.at[k]),
                pltpu.make_async_copy(gstage_ref.at[k], gs_ref.at[dst], gsem.at[k]))

    for e in range(N_EXPERTS):
        c = cnt_ref[b * N_EXPERTS + e]
        for s in range(NS):
            @pl.when(s * SUB < c)
            def _(e=e, s=s):
                hit = (rank[e:e + 1, :] == jj + s * SUB) & sel[e:e + 1, :]
                stage_ref[e * NS + s] = _dot(hit.astype(BF16), hf).astype(BF16)
                g_rows = jnp.sum(jnp.where(hit, gate[e:e + 1, :], 0.0), axis=1, keepdims=True)
                gstage_ref[e * NS + s] = jnp.broadcast_to(g_rows, (SUB, LANES))
                for cp in copies(e, s):
                    cp.start()
    for e in range(N_EXPERTS):
        c = cnt_ref[b * N_EXPERTS + e]
        for s in range(NS):
            @pl.when(s * SUB < c)
            def _(e=e, s=s):
                for cp in copies(e, s):
                    cp.wait()


def _dispatch(off, cnt, hf_bf, gates_t, xs, gs, block0):
    n = hf_bf.shape[0]
    rows = xs.shape[0]
    return pl.pallas_call(
        functools.partial(_dispatch_kernel, block0=block0),
        grid_spec=pltpu.PrefetchScalarGridSpec(
            num_scalar_prefetch=2,
            grid=(n // TB,),
            in_specs=[pl.BlockSpec((TB, D), lambda b, o, c: (b, 0)),
                      pl.BlockSpec((N_EXPERTS, TB), lambda b, o, c: (0, b)),
                      pl.BlockSpec(memory_space=pl.ANY),
                      pl.BlockSpec(memory_space=pl.ANY)],
            out_specs=[pl.BlockSpec(memory_space=pl.ANY), pl.BlockSpec(memory_space=pl.ANY)],
            scratch_shapes=[pltpu.VMEM((N_EXPERTS * NS, SUB, D), BF16),
                            pltpu.VMEM((N_EXPERTS * NS, SUB, LANES), F32),
                            pltpu.SemaphoreType.DMA((N_EXPERTS * NS,)),
                            pltpu.SemaphoreType.DMA((N_EXPERTS * NS,))]),
        out_shape=[jax.ShapeDtypeStruct((rows, D), BF16), jax.ShapeDtypeStruct((rows, LANES), F32)],
        input_output_aliases={4: 0, 5: 1},
        compiler_params=_cparams(("arbitrary",), 40),
        name="moe_dispatch",
    )(off, cnt, hf_bf, gates_t, xs, gs)


def _expert_kernel(te_ref, nu_ref, x_ref, g_ref, w1_ref, w3_ref, w2_ref, y_ref, acc_ref):
    del te_ref
    i, j = pl.program_id(0), pl.program_id(1)

    @pl.when(i < nu_ref[0])
    def _():
        x = x_ref[...]
        parts = [slice(p * FCX_SUB, (p + 1) * FCX_SUB) for p in range(FCX // FCX_SUB)]
        pre = [(_dot(x, w1_ref[0, :, p].astype(BF16)), _dot(x, w3_ref[0, :, p].astype(BF16))) for p in parts]
        contrib = None
        for p, (h1, h3) in zip(parts, pre):
            part = _dot((_silu(h1) * h3).astype(BF16), w2_ref[0, p, :].astype(BF16))
            contrib = part if contrib is None else contrib + part

        @pl.when(j == 0)
        def _():
            acc_ref[...] = contrib

        @pl.when(j > 0)
        def _():
            acc_ref[...] += contrib

        @pl.when(j == pl.num_programs(1) - 1)
        def _():
            y_ref[...] = (g_ref[:, 0:1] * acc_ref[...]).astype(BF16)

    @pl.when((i >= nu_ref[0]) & (j == 0))
    def _():
        y_ref[...] = jnp.zeros(y_ref.shape, BF16)


def _experts(tile_e, n_used, xs, gs, w1, w3, w2):
    rows = xs.shape[0]
    dff = w1.shape[2]
    nj = dff // FCX

    def tile_idx(i, j, te, nu):
        return (jnp.minimum(i, nu[0] - 1), 0)

    def chunk(i, j, nu):
        return jnp.where(i < nu[0], j, nj - 1)

    return pl.pallas_call(
        _expert_kernel,
        grid_spec=pltpu.PrefetchScalarGridSpec(
            num_scalar_prefetch=2,
            grid=(rows // TMX, nj),
            in_specs=[pl.BlockSpec((TMX, D), tile_idx),
                      pl.BlockSpec((TMX, LANES), tile_idx),
                      pl.BlockSpec((1, D, FCX), lambda i, j, te, nu: (te[i], 0, chunk(i, j, nu))),
                      pl.BlockSpec((1, D, FCX), lambda i, j, te, nu: (te[i], 0, chunk(i, j, nu))),
                      pl.BlockSpec((1, FCX, D), lambda i, j, te, nu: (te[i], chunk(i, j, nu), 0))],
            out_specs=pl.BlockSpec((TMX, D), lambda i, j, te, nu: (i, 0)),
            scratch_shapes=[pltpu.VMEM((TMX, D), F32)]),
        out_shape=jax.ShapeDtypeStruct((rows, D), BF16),
        compiler_params=_cparams(("arbitrary", "arbitrary"), 56),
        name="moe_experts",
    )(tile_e, n_used, xs, gs, w1, w3, w2)


SLOTS = 2 * TB // SUB + N_EXPERTS


def _gather_expert_rows(b, first_step, off_ref, cnt_ref, gt_ref, ys_ref, ybuf_ref, p_ref, sem):
    @pl.when(first_step)
    def _():
        ybuf_ref[...] = jnp.zeros(ybuf_ref.shape, BF16)

    def copy(e, s, slot):
        src = pl.multiple_of(off_ref[b * N_EXPERTS + e] + s * SUB, ROW_ALIGN)
        dst = pl.multiple_of(slot * SUB, SUB)
        return pltpu.make_async_copy(ys_ref.at[pl.ds(src, SUB)], ybuf_ref.at[pl.ds(dst, SUB)], sem.at[slot])

    pairs = [(e, s) for e in range(N_EXPERTS) for s in range(NS)]
    used, slots = [], []
    slot = jnp.int32(0)
    for e, s in pairs:
        u = s * SUB < cnt_ref[b * N_EXPERTS + e]
        used.append(u)
        slots.append(slot)
        slot = slot + u.astype(I32)
    for (e, s), u, k in zip(pairs, used, slots):
        @pl.when(u)
        def _(e=e, s=s, k=k):
            copy(e, s, k).start()

    sel = gt_ref[...] > 0.0
    before = lax.broadcasted_iota(I32, (TB, TB), 0) < lax.broadcasted_iota(I32, (TB, TB), 1)
    rank = _dot(sel.astype(BF16), before.astype(BF16)).astype(I32)
    jj = lax.broadcasted_iota(I32, (SUB, TB), 0)
    p_ref[...] = jnp.zeros(p_ref.shape, BF16)
    for (e, s), u, k in zip(pairs, used, slots):
        @pl.when(u)
        def _(e=e, s=s, k=k):
            hit = (rank[e:e + 1, :] == jj + s * SUB) & sel[e:e + 1, :]
            p_ref[pl.ds(pl.multiple_of(k * SUB, SUB), SUB), :] = hit.astype(BF16)
    for (e, s), u, k in zip(pairs, used, slots):
        @pl.when(u)
        def _(e=e, s=s, k=k):
            copy(e, s, k).wait()

    tn = (((0,), (0,)), ((), ()))
    return lax.dot_general(p_ref[...], ybuf_ref[...], tn, preferred_element_type=F32)


_COMBINE_SCRATCH = [pltpu.VMEM((SLOTS * SUB, D), BF16),
                    pltpu.VMEM((SLOTS * SUB, TB), BF16),
                    pltpu.SemaphoreType.DMA((SLOTS,))]


def _combine_kernel(off_ref, cnt_ref, gt_ref, x3_ref, mod_ref, gpost_ref, ys_ref,
                    out_ref, ybuf_ref, p_ref, sem):
    b = pl.program_id(0)
    moe = _gather_expert_rows(b, b == 0, off_ref, cnt_ref, gt_ref, ys_ref, ybuf_ref, p_ref, sem)
    _, _, gt = _split3(mod_ref[0, 0])
    out_ref[...] = x3_ref[...] + gt * (_rms_scale(moe) * gpost_ref[...])


def _combine(off, cnt, gates_t, x3, mods_ffn, gpost, ys, t):
    n = x3.shape[0]
    per_b = t // TB
    return pl.pallas_call(
        _combine_kernel,
        grid_spec=pltpu.PrefetchScalarGridSpec(
            num_scalar_prefetch=2,
            grid=(n // TB,),
            in_specs=[pl.BlockSpec((N_EXPERTS, TB), lambda b, o, c: (0, b)),
                      pl.BlockSpec((TB, D), lambda b, o, c: (b, 0)),
                      pl.BlockSpec((1, 1, 1, 3 * D), lambda b, o, c: (1, b // per_b, 0, 0)),
                      pl.BlockSpec((1, D), lambda b, o, c: (0, 0)),
                      pl.BlockSpec(memory_space=pl.ANY)],
            out_specs=pl.BlockSpec((TB, D), lambda b, o, c: (b, 0)),
            scratch_shapes=_COMBINE_SCRATCH),
        out_shape=jax.ShapeDtypeStruct((n, D), F32),
        compiler_params=_cparams(("arbitrary",), 48),
        name="moe_combine",
    )(off, cnt, gates_t, x3, mods_ffn, gpost, ys)


def _combine_rows_kernel(off_ref, cnt_ref, gt_ref, ys_ref, out_ref, ybuf_ref, p_ref, sem, *, block0):
    out_ref[...] = _gather_expert_rows(block0, True, off_ref, cnt_ref, gt_ref, ys_ref, ybuf_ref, p_ref, sem)


def _combine_rows(off, cnt, gates_t, ys, block0):
    return pl.pallas_call(
        functools.partial(_combine_rows_kernel, block0=block0),
        grid_spec=pltpu.PrefetchScalarGridSpec(
            num_scalar_prefetch=2,
            grid=(1,),
            in_specs=[pl.BlockSpec((N_EXPERTS, TB), lambda b, o, c: (0, 0)),
                      pl.BlockSpec(memory_space=pl.ANY)],
            out_specs=pl.BlockSpec((TB, D), lambda b, o, c: (0, 0)),
            scratch_shapes=_COMBINE_SCRATCH),
        out_shape=jax.ShapeDtypeStruct((TB, D), F32),
        compiler_params=_cparams(("arbitrary",), 48),
        name="moe_combine_rows",
    )(off, cnt, gates_t, ys)


def _s_mix0_kernel(x_ref, st_ref, modm_ref, modf_ref, gpre_ref, gpost_ref, gpref_ref, pw_ref, ps_ref,
                   x1_ref, pool_ref, hf_ref, *, pos):
    sh, sc, gt = _split3(modm_ref[...])
    x = x_ref[...]
    hm = _rms_scale(x) * gpre_ref[...] * (1.0 + sc) + sh
    parts = []
    for g, w in enumerate(POOL_WINDOWS):
        lo = g * PG
        tok = hm[:, lo:lo + PG]
        s = tok
        for j in range(1, w):
            s = s + st_ref[POOL_STATE - j, :, lo:lo + PG]
        cnt = float(min(w, pos + 1))
        parts.append(_dot_hi(s / cnt - tok, pw_ref[g]))
    mix = jnp.concatenate(parts, axis=1) * ps_ref[...]
    x1 = x + gt * (_rms_scale(mix) * gpost_ref[...])
    x1_ref[...] = x1
    for j in range(POOL_STATE - 1):
        pool_ref[j] = st_ref[j + 1]
    pool_ref[POOL_STATE - 1] = hm
    shf, scf, _ = _split3(modf_ref[...])
    hf_ref[...] = _rms_scale(x1) * gpref_ref[...] * (1.0 + scf) + shf


def _s_mix0(x_s, state_t, modm, modf, gpre, gpost, gpref, pw, ps, pos):
    rows = x_s.shape[0]
    return pl.pallas_call(
        functools.partial(_s_mix0_kernel, pos=pos),
        out_shape=[jax.ShapeDtypeStruct((rows, D), F32),
                   jax.ShapeDtypeStruct((POOL_STATE, rows, D), F32),
                   jax.ShapeDtypeStruct((rows, D), F32)],
        compiler_params=pltpu.CompilerParams(vmem_limit_bytes=40 * MIB),
        name="s_mix0",
    )(x_s, state_t, modm, modf, gpre, gpost, gpref, pw, ps)


def _s_ffn_kernel(h_ref, w1_ref, w3_ref, w2_ref, f_ref):
    j = pl.program_id(0)
    h = h_ref[...]
    contrib = _dot_hi(_silu(_dot_hi(h, w1_ref[...])) * _dot_hi(h, w3_ref[...]), w2_ref[...])

    @pl.when(j == 0)
    def _():
        f_ref[...] = contrib

    @pl.when(j > 0)
    def _():
        f_ref[...] += contrib


def _s_ffn(h, w1, w3, w2):
    rows = h.shape[0]
    dff = w1.shape[1]
    fc = 256
    return pl.pallas_call(
        _s_ffn_kernel,
        grid=(dff // fc,),
        in_specs=[pl.BlockSpec((rows, D), lambda j: (0, 0)),
                  pl.BlockSpec((D, fc), lambda j: (0, j)),
                  pl.BlockSpec((D, fc), lambda j: (0, j)),
                  pl.BlockSpec((fc, D), lambda j: (j, 0))],
        out_specs=pl.BlockSpec((rows, D), lambda j: (0, 0)),
        out_shape=jax.ShapeDtypeStruct((rows, D), F32),
        compiler_params=_cparams(("arbitrary",), 32),
        name="s_ffn0",
    )(h, w1, w3, w2)


def _s_post0_kernel(x1_ref, f_ref, modf_ref, modm_ref, gpost_ref, gkv_ref, gpre1_ref,
                    x2_ref, kvin_ref, hm1_ref):
    _, _, gt = _split3(modf_ref[...])
    x2 = x1_ref[...] + gt * (_rms_scale(f_ref[...]) * gpost_ref[...])
    x2_ref[...] = x2
    xn = _rms_scale(x2)
    kvin_ref[...] = xn * gkv_ref[...]
    sh1, sc1, _ = _split3(modm_ref[...])
    hm1_ref[...] = xn * gpre1_ref[...] * (1.0 + sc1) + sh1


def _s_post0(x1, f, modf, modm, gpost, gkv, gpre1):
    rows = x1.shape[0]
    return pl.pallas_call(
        _s_post0_kernel,
        out_shape=[jax.ShapeDtypeStruct((rows, D), F32)] * 3,
        name="s_post0",
    )(x1, f, modf, modm, gpost, gkv, gpre1)


def _s_linear_kernel(h_ref, w_ref, cos_ref, sin_ref, o_ref):
    o_ref[...] = _rope_cols(_dot_hi(h_ref[...], w_ref[...]), cos_ref[...], sin_ref[...])


def _s_linear_plain_kernel(h_ref, w_ref, o_ref):
    o_ref[...] = _dot_hi(h_ref[...], w_ref[...])


def _s_linear(h, w, rope=None):
    rows, k = h.shape
    nw = w.shape[1]
    cw = 512
    in_specs = [pl.BlockSpec((rows, k), lambda j: (0, 0)), pl.BlockSpec((k, cw), lambda j: (0, j))]
    args = [h, w]
    body = _s_linear_plain_kernel
    if rope is not None:
        body = _s_linear_kernel
        in_specs += [pl.BlockSpec((1, cw), lambda j: (0, j)), pl.BlockSpec((1, cw), lambda j: (0, j))]
        args += [rope[0], rope[1]]
    return pl.pallas_call(
        body,
        grid=(nw // cw,),
        in_specs=in_specs,
        out_specs=pl.BlockSpec((rows, cw), lambda j: (0, j)),
        out_shape=jax.ShapeDtypeStruct((rows, nw), F32),
        compiler_params=_cparams(("arbitrary",), 32),
        name="s_linear",
    )(*args)


def _s_attn_kernel(qkv_ref, kc_ref, vc_ref, o_ref):
    n_past = kc_ref.shape[3]
    hb = kc_ref.shape[1]
    ng = len(ATTN_GROUPS)
    nt = (((1,), (1,)), ((), ()))
    eye = (lax.broadcasted_iota(I32, (LANES, LANES), 0) == lax.broadcasted_iota(I32, (LANES, LANES), 1)).astype(F32)
    tiles = qkv_ref[0].reshape(hb * 8, HD)
    if hb * 8 < LANES:
        tiles = jnp.concatenate([tiles, jnp.zeros((LANES - hb * 8, HD), F32)], axis=0)
    cols = lax.dot_general(eye[:HD, :HD], tiles, nt, precision=HIGHEST, preferred_element_type=F32)
    col = lambda h, j: cols[:, 8 * h + j:8 * h + j + 1]
    outs, lses = [], []
    for g, (win, dil) in enumerate(ATTN_GROUPS):
        lo = n_past - win
        dist = win - lax.broadcasted_iota(I32, (1, win), 1)
        s_c = jnp.concatenate([jnp.sum(kc_ref[0, h, :, lo:] * col(h, g), axis=0, keepdims=True)
                               for h in range(hb)], axis=0) * SCALE
        s_c = jnp.where((dist & (dil - 1)) == 0, s_c, -jnp.inf)
        s_n = jnp.concatenate([jnp.sum(col(h, ng) * col(h, g), axis=0, keepdims=True)
                               for h in range(hb)], axis=0) * SCALE
        m = jnp.maximum(jnp.max(s_c, axis=1, keepdims=True), s_n)
        p_c = jnp.exp(s_c - m)
        p_n = jnp.exp(s_n - m)
        den = jnp.sum(p_c, axis=1, keepdims=True) + p_n
        outs.append([(jnp.sum(vc_ref[0, h, :, lo:] * p_c[h:h + 1], axis=1, keepdims=True)
                      + p_n[h:h + 1] * col(h, ng + 1)) / den[h:h + 1] for h in range(hb)])
        lses.append(m + jnp.log(den))
    mx = jnp.maximum(jnp.maximum(lses[0], lses[1]), lses[2])
    es = [jnp.exp(l - mx) for l in lses]
    tot = es[0] + es[1] + es[2]
    lane = lax.broadcasted_iota(I32, (1, LANES), 1)
    out_cols = jnp.zeros((HD, LANES), F32)
    for h in range(hb):
        out = (es[0] / tot)[h:h + 1] * outs[0][h]
        for g in range(1, ng):
            out = out + (es[g] / tot)[h:h + 1] * outs[g][h]
        out_cols = out_cols + out * (lane == h).astype(F32)
    o_ref[0] = lax.dot_general(eye[:hb], out_cols, nt, precision=HIGHEST, preferred_element_type=F32)


def _s_attn(qkv, cache_kt, cache_vt):
    rows, _, _, n_past = cache_kt.shape
    for win, dil in ATTN_GROUPS:
        assert n_past >= win and win % LANES == 0 and dil & (dil - 1) == 0
    cache = pl.BlockSpec((1, HB_S, HD, n_past), lambda b, h: (b, h, 0, 0))
    return pl.pallas_call(
        _s_attn_kernel,
        grid=(rows, NH // HB_S),
        in_specs=[pl.BlockSpec((1, HB_S, 8, HD), lambda b, h: (b, h, 0, 0)), cache, cache],
        out_specs=pl.BlockSpec((1, HB_S, HD), lambda b, h: (b, h, 0)),
        out_shape=jax.ShapeDtypeStruct((rows, NH, HD), F32),
        compiler_params=_cparams(("arbitrary", "arbitrary"), 40),
        name="s_attn",
    )(qkv, cache_kt, cache_vt)


def _s_post1_kernel(x2_ref, mix_ref, modm_ref, modf_ref, gpost_ref, gpre_ref, rwt_ref, rb_ref,
                    x3_ref, hf_ref, gc_ref):
    _, _, gt = _split3(modm_ref[...])
    x3 = x2_ref[...] + gt * (_rms_scale(mix_ref[...]) * gpost_ref[...])
    x3_ref[...] = x3
    sh, sc, _ = _split3(modf_ref[...])
    hf = _rms_scale(x3) * gpre_ref[...] * (1.0 + sc) + sh
    hf_ref[...] = hf
    logits_t = lax.dot_general(rwt_ref[...], hf, (((1,), (1,)), ((), ())),
                               precision=HIGHEST, preferred_element_type=F32) + rb_ref[...]
    gc_ref[...] = _router_top2(logits_t)


def _s_post1(x2, mix, modm, modf, gpost, gpre, rwt, rb):
    rows = x2.shape[0]
    return pl.pallas_call(
        _s_post1_kernel,
        out_shape=[jax.ShapeDtypeStruct((rows, D), F32), jax.ShapeDtypeStruct((rows, D), F32),
                   jax.ShapeDtypeStruct((N_EXPERTS, rows), F32)],
        name="s_post1",
    )(x2, mix, modm, modf, gpost, gpre, rwt, rb)


def _s_post2_kernel(x3_ref, f_ref, modf_ref, gpost_ref, y_ref):
    _, _, gt = _split3(modf_ref[...])
    y_ref[...] = x3_ref[...] + gt * (_rms_scale(f_ref[...]) * gpost_ref[...])


def _s_post2(x3, f, modf, gpost):
    return pl.pallas_call(
        _s_post2_kernel,
        out_shape=jax.ShapeDtypeStruct(x3.shape, F32),
        name="s_post2",
    )(x3, f, modf, gpost)


def _rope_tables(pos, paired=False):
    half = HD // 2
    inv = ROPE_THETA ** (-jnp.arange(half, dtype=F32) / half)
    ang = pos.astype(F32)[:, None] * inv[None, :]
    cos, sin = jnp.cos(ang), jnp.sin(ang)
    cos = jnp.tile(cos, (1, LANES // half))
    if paired:
        sin = jnp.concatenate([-sin, -sin, sin, sin], axis=1)
    else:
        sin = jnp.tile(jnp.concatenate([-sin, sin], axis=1), (1, LANES // HD))
    return cos, sin


def kernel(x_prompt, x_sample, state_pool, cache_k, cache_v, c_prompt, c_sample, ada_mix_w, ada_mix_b, g_pre_mix, g_post_mix, pool_w, pool_scale, g_kv, w_kv, w_q, w_o, ada_ffn_w, ada_ffn_b, g_pre_ffn, g_post_ffn, ffn_w1, ffn_w3, ffn_w2, router_w, router_b, moe_w1, moe_w3, moe_w2):
    b, t, _ = x_prompt.shape
    nsmp = x_sample.shape[0]
    n = b * t
    past_len = 16384
    assert x_sample.shape[1] == 1 and g_pre_mix.shape[0] == 2 and t % (ATTN_GROUPS[-1][1] * BLK) == 0
    vec = lambda a: a.reshape(1, D)

    c_all = jnp.concatenate([c_prompt, c_sample], axis=0)
    mods_mix = _ada(c_all, ada_mix_w, ada_mix_b)
    mods_ffn = _ada(c_all, ada_ffn_w, ada_ffn_b)
    pm_mix = mods_mix[:, :b].reshape(2, b, 1, 3 * D)
    pm_ffn = mods_ffn[:, :b].reshape(2, b, 1, 3 * D)
    sm_mix = mods_mix[:, b:]
    sm_ffn = mods_ffn[:, b:]

    rwt = router_w[0].T
    rb = router_b[0].reshape(N_EXPERTS, 1)

    x1, pool16 = _mix0(x_prompt, pm_mix, vec(g_pre_mix[0]), vec(g_post_mix[0]),
                       pool_w[0].astype(BF16), vec(pool_scale[0]))
    cos_p, sin_p = _rope_tables(jnp.arange(t, dtype=jnp.int32), paired=True)
    x2 = _ffn0(x1.reshape(n, D), pm_ffn, vec(g_pre_ffn[0]), vec(g_post_ffn[0]),
               ffn_w1[0].astype(BF16), ffn_w3[0].astype(BF16), ffn_w2[0].astype(BF16), t)
    wkv_bf = jnp.concatenate([_pair_heads(w_kv[:, :D]), w_kv[:, D:]], axis=1).astype(BF16)
    kt_p, vt_p, *kv_l = _kvproj(x2, vec(g_kv), cos_p, sin_p, wkv_bf, b, t)
    q_l = _qproj(x2, pm_mix, vec(g_pre_mix[1]), cos_p, sin_p, _pair_heads(w_q[0]).astype(BF16), b, t)
    attn = _attention(q_l, kv_l[:3], kv_l[3:], b, t)
    x3, hf1, gates_t = _post1(attn, x2, pm_mix, pm_ffn, vec(g_post_mix[1]),
                              vec(g_pre_ffn[1]), w_o[0].astype(BF16), rwt, rb, t)

    state_t = jnp.transpose(state_pool[:, 0], (1, 0, 2))
    x1s, pool_t, hf0s = _s_mix0(x_sample.reshape(nsmp, D), state_t, sm_mix[0], sm_ffn[0],
                                vec(g_pre_mix[0]), vec(g_post_mix[0]), vec(g_pre_ffn[0]),
                                pool_w[0], vec(pool_scale[0]), past_len)
    f0s = _s_ffn(hf0s, ffn_w1[0], ffn_w3[0], ffn_w2[0])
    x2s, kvin_s, hm1s = _s_post0(x1s, f0s, sm_ffn[0], sm_mix[1], vec(g_post_ffn[0]), vec(g_kv), vec(g_pre_mix[1]))
    cos_s, sin_s = _rope_tables(jnp.full((1,), past_len, jnp.int32))
    ones, zeros = jnp.ones((1, D), F32), jnp.zeros((1, D), F32)
    kv_s = _s_linear(kvin_s, w_kv, (jnp.concatenate([jnp.tile(cos_s, (1, D // LANES)), ones], axis=1),
                                    jnp.concatenate([jnp.tile(sin_s, (1, D // LANES)), zeros], axis=1)))
    q_s = _s_linear(hm1s, w_q[0], (jnp.tile(cos_s, (1, 3 * D // LANES)), jnp.tile(sin_s, (1, 3 * D // LANES))))
    k_s = kv_s[:, :D].reshape(nsmp, NH, HD)
    v_s = kv_s[:, D:].reshape(nsmp, NH, HD)
    cache_kt = jnp.transpose(cache_k, (0, 2, 3, 1))
    cache_vt = jnp.transpose(cache_v, (0, 2, 3, 1))
    qkv = jnp.concatenate([jnp.transpose(q_s.reshape(nsmp, len(ATTN_GROUPS), NH, HD), (0, 2, 1, 3)),
                           k_s[:, :, None], v_s[:, :, None]], axis=2)
    qkv = jnp.pad(qkv, ((0, 0), (0, 0), (0, 8 - qkv.shape[2]), (0, 0)))
    attn_s = _s_attn(qkv, cache_kt, cache_vt).reshape(nsmp, D)
    mix1s = _s_linear(attn_s, w_o[0])
    x3s, hf1s, gts = _s_post1(x2s, mix1s, sm_mix[1], sm_ffn[1], vec(g_post_mix[1]), vec(g_pre_ffn[1]), rwt, rb)

    hf_blk = jnp.zeros((TB, D), BF16).at[:nsmp].set(hf1s.astype(BF16))
    gt_blk = jnp.zeros((N_EXPERTS, TB), F32).at[:, :nsmp].set(gts)
    off, cnt, tile_e, n_used = _route_plan(jnp.concatenate([gates_t, gt_blk], axis=1), n + TB)
    xs = jnp.zeros((_max_sorted_rows(n + TB), D), BF16)
    gs = jnp.zeros((xs.shape[0], LANES), F32)
    xs, gs = _dispatch(off, cnt, hf1, gates_t, xs, gs, 0)
    xs, gs = _dispatch(off, cnt, hf_blk, gt_blk, xs, gs, n // TB)
    ys = _experts(tile_e, n_used, xs, gs, moe_w1[0], moe_w3[0], moe_w2[0])
    y_prompt = _combine(off, cnt, gates_t, x3, pm_ffn, vec(g_post_ffn[1]), ys, t).reshape(b, t, D)
    moe_s = _combine_rows(off, cnt, gt_blk, ys, n // TB)[:nsmp]
    y_sample = _s_post2(x3s, moe_s, sm_ffn[1], vec(g_post_ffn[1])).reshape(nsmp, 1, D)

    pool_prompt = pool16[:, None, 1:, :]
    pool_sample = jnp.transpose(pool_t, (1, 0, 2))[:, None]
    return (y_prompt, y_sample, pool_prompt, pool_sample,
            jnp.transpose(kt_p, (0, 3, 1, 2)), jnp.transpose(vt_p, (0, 3, 1, 2)),
            k_s[:, None], v_s[:, None])
```

```python
import functools

import jax
import jax.numpy as jnp
import numpy as np
from jax import lax
from jax.experimental import pallas as pl
from jax.experimental.pallas import tpu as pltpu

F32, BF16, I32 = jnp.float32, jnp.bfloat16, jnp.int32
HIGHEST = lax.Precision.HIGHEST

D = 1024
POOL_WINDOWS = (2, 4, 8, 16)
PG = D // len(POOL_WINDOWS)
POOL_STATE = max(POOL_WINDOWS) - 1
POOL_ROWS = POOL_STATE + 1
HALO = 2 * POOL_ROWS
ATTN_GROUPS = ((128, 1), (512, 4), (2048, 16))
NH = 16
HD = 64
BLK = 128
ROPE_THETA = 10000.0
N_EXPERTS = 8
EPS = 1e-6
SCALE = HD ** -0.5
LANES = 128
MIB = 1024 * 1024

TT_MIX = 512
TM_FFN = 1024
HB_S = 16
ATTN_UNITS = 8
FC_FFN = 256
TM_POST = 512
TM_WO = 1024
TB = 512
SUB = 128
NS = TB // SUB
TMX = 1024
FCX = 512
FCX_SUB = 256
ROW_ALIGN = 16


def _cparams(sem, vmem_mib):
    return pltpu.CompilerParams(dimension_semantics=sem, vmem_limit_bytes=vmem_mib * MIB)


def _rms_scale(x):
    return x * lax.rsqrt(jnp.mean(x * x, axis=-1, keepdims=True) + EPS)


def _split3(m):
    return m[:, :D], m[:, D:2 * D], m[:, 2 * D:]


def _silu(x):
    return x * jax.nn.sigmoid(x)


def _rope_cols(x, cos, sin):
    lane = lax.broadcasted_iota(I32, (x.shape[0], LANES), 1)
    first = (lane % HD) < (HD // 2)
    shared = cos.shape[1] == LANES
    outs = []
    for c in range(x.shape[1] // LANES):
        cols = slice(c * LANES, (c + 1) * LANES)
        xc = x[:, cols]
        swapped = jnp.where(first, pltpu.roll(xc, LANES - HD // 2, 1), pltpu.roll(xc, HD // 2, 1))
        outs.append(xc * (cos if shared else cos[:, cols]) + swapped * (sin if shared else sin[:, cols]))
    return jnp.concatenate(outs, axis=1)


def _rope_paired(x, cos, sin):
    outs = []
    for c in range(x.shape[1] // LANES):
        xc = x[:, c * LANES:(c + 1) * LANES]
        outs.append(xc * cos + pltpu.roll(xc, LANES // 2, 1) * sin)
    return jnp.concatenate(outs, axis=1)


def _pair_heads(w):
    k, n = w.shape
    return w.reshape(k, n // LANES, 2, 2, HD // 2).transpose(0, 1, 3, 2, 4).reshape(k, n)


def _dot_hi(a, b):
    return jnp.dot(a, b, precision=HIGHEST, preferred_element_type=F32)


def _dot(a, b):
    return jnp.dot(a, b, preferred_element_type=F32)


def _ada_kernel(c_ref, w_ref, b_ref, o_ref):
    o_ref[0] = _dot_hi(_silu(c_ref[...]), w_ref[0]) + b_ref[0]


def _ada(c_all, w, b):
    nl, _, n3 = w.shape
    rows = c_all.shape[0]
    cw = 768
    return pl.pallas_call(
        _ada_kernel,
        grid=(nl, n3 // cw),
        in_specs=[pl.BlockSpec((rows, D), lambda l, j: (0, 0)),
                  pl.BlockSpec((1, D, cw), lambda l, j: (l, 0, j)),
                  pl.BlockSpec((1, 1, cw), lambda l, j: (l, 0, j))],
        out_specs=pl.BlockSpec((1, rows, cw), lambda l, j: (l, 0, j)),
        out_shape=jax.ShapeDtypeStruct((nl, rows, n3), F32),
        compiler_params=_cparams(("arbitrary", "arbitrary"), 32),
        name="ada_mod",
    )(c_all, w, b.reshape(nl, 1, n3))


def _mix0_kernel(x_ref, xh_ref, mod_ref, gpre_ref, gpost_ref, pw_ref, ps_ref,
                 x1_ref, pool_ref, ext_ref, lv_ref):
    i = pl.program_id(1)
    tt = x_ref.shape[1]
    sh, sc, gt = _split3(mod_ref[0, 0])
    gpre = gpre_ref[...]
    x = x_ref[0]
    hm = _rms_scale(x) * gpre * (1.0 + sc) + sh
    hh = _rms_scale(xh_ref[0]) * gpre * (1.0 + sc) + sh
    ext_ref[0:HALO, :] = jnp.where(i > 0, hh, 0.0)
    ext_ref[HALO:, :] = hm
    t = i * tt + lax.broadcasted_iota(I32, (tt, 1), 0)
    rows = tt + HALO
    parts = []
    for g, w in enumerate(POOL_WINDOWS):
        lo = g * PG
        tok = hm[:, lo:lo + PG]
        levels = w.bit_length() - 1
        assert w == 1 << levels
        first = [HALO]
        for k in range(levels - 1, 0, -1):
            first.insert(0, (first[0] - (1 << k)) // 8 * 8)
        for k in range(1, levels + 1):
            r0, back = first[k - 1], 1 << (k - 1)
            if k == 1:
                s = ext_ref[r0:rows, lo:lo + PG] + ext_ref[r0 - back:rows - back, lo:lo + PG]
            else:
                prev = lv_ref.at[k % 2]
                s = prev[r0:rows, :] + prev[r0 - back:rows - back, :]
            if k < levels:
                lv_ref[(k + 1) % 2, r0:rows, :] = s
        inv_cnt = 1.0 / jnp.minimum(w, t + 1).astype(F32)
        parts.append(_dot((s * inv_cnt - tok).astype(BF16), pw_ref[g]))
    mix = jnp.concatenate(parts, axis=1) * ps_ref[...]
    x1_ref[0] = x + gt * (_rms_scale(mix) * gpost_ref[...])

    @pl.when(i == pl.num_programs(1) - 1)
    def _():
        pool_ref[0] = ext_ref[rows - POOL_ROWS:rows, :]


def _mix0(x, mods, gpre, gpost, pw_bf, ps):
    b, t, _ = x.shape
    tt = TT_MIX
    hb = tt // HALO
    return pl.pallas_call(
        _mix0_kernel,
        grid=(b, t // tt),
        in_specs=[pl.BlockSpec((1, tt, D), lambda bb, i: (bb, i, 0)),
                  pl.BlockSpec((1, HALO, D), lambda bb, i: (bb, jnp.maximum(i * hb - 1, 0), 0)),
                  pl.BlockSpec((1, 1, 1, 3 * D), lambda bb, i: (0, bb, 0, 0)),
                  pl.BlockSpec((1, D), lambda bb, i: (0, 0)),
                  pl.BlockSpec((1, D), lambda bb, i: (0, 0)),
                  pl.BlockSpec((len(POOL_WINDOWS), PG, PG), lambda bb, i: (0, 0, 0)),
                  pl.BlockSpec((1, D), lambda bb, i: (0, 0))],
        out_specs=[pl.BlockSpec((1, tt, D), lambda bb, i: (bb, i, 0)),
                   pl.BlockSpec((1, POOL_ROWS, D), lambda bb, i: (bb, 0, 0))],
        out_shape=[jax.ShapeDtypeStruct((b, t, D), F32),
                   jax.ShapeDtypeStruct((b, POOL_ROWS, D), F32)],
        scratch_shapes=[pltpu.VMEM((tt + HALO, D), F32), pltpu.VMEM((2, tt + HALO, PG), F32)],
        compiler_params=_cparams(("arbitrary", "arbitrary"), 40),
        name="mix0",
    )(x, x, mods, gpre, gpost, pw_bf, ps)


def _store_dilated(val, plane_ref, dils, out_refs):
    tm = val.shape[0]
    for hp in range(NH // 2):
        plane_ref[hp] = val[:, hp * LANES:(hp + 1) * LANES]
    for dil, out_ref in zip(dils, out_refs):
        for hp in range(NH // 2):
            for r in range(dil):
                out_ref[0, hp, r] = plane_ref[hp, pl.ds(r, tm // dil, stride=dil), :].astype(BF16)


def _dilated_specs(b, t, tm):
    per_b = t // tm
    specs = [pl.BlockSpec((1, NH // 2, dil, tm // dil, LANES), lambda i: (i // per_b, 0, 0, i % per_b, 0))
             for _, dil in ATTN_GROUPS]
    shapes = [jax.ShapeDtypeStruct((b, NH // 2, dil, t // dil, LANES), BF16) for _, dil in ATTN_GROUPS]
    return specs, shapes


def _ffn0_kernel(x1_ref, modf_ref, gpre_ref, gpost_ref, w1_ref, w3_ref, w2_ref, x2_ref, acc_ref):
    sh, sc, gt = _split3(modf_ref[0, 0])
    x1 = x1_ref[...]
    hf = (_rms_scale(x1) * gpre_ref[...] * (1.0 + sc) + sh).astype(BF16)
    for c in range(w1_ref.shape[1] // FC_FFN):
        cs = slice(c * FC_FFN, (c + 1) * FC_FFN)
        a = (_silu(_dot(hf, w1_ref[:, cs])) * _dot(hf, w3_ref[:, cs])).astype(BF16)
        contrib = _dot(a, w2_ref[cs, :])
        if c == 0:
            acc_ref[...] = contrib
        else:
            acc_ref[...] += contrib
    x2 = x1 + gt * (_rms_scale(acc_ref[...]) * gpost_ref[...])
    x2_ref[...] = x2


def _ffn0(x1, mods_ffn, gpre, gpost, w1, w3, w2, t):
    n = x1.shape[0]
    tm = TM_FFN
    per_b = t // tm
    dff = w1.shape[1]
    row = lambda i: (i, 0)
    fixed = lambda i: (0, 0)
    resident = lambda shape: pl.BlockSpec(shape, fixed, pipeline_mode=pl.Buffered(1))
    vec = pl.BlockSpec((1, D), fixed)
    return pl.pallas_call(
        _ffn0_kernel,
        grid=(n // tm,),
        in_specs=[pl.BlockSpec((tm, D), row),
                  pl.BlockSpec((1, 1, 1, 3 * D), lambda i: (0, i // per_b, 0, 0)),
                  vec, vec,
                  resident((D, dff)), resident((D, dff)), resident((dff, D))],
        out_specs=pl.BlockSpec((tm, D), row),
        out_shape=jax.ShapeDtypeStruct((n, D), F32),
        scratch_shapes=[pltpu.VMEM((tm, D), F32)],
        compiler_params=_cparams(("arbitrary",), 56),
        name="ffn0",
    )(x1, mods_ffn, gpre, gpost, w1, w3, w2)


def _kvproj_kernel(x2_ref, gkv_ref, cos_ref, sin_ref, wkv_ref,
                   kt_ref, vt_ref, k0_ref, k1_ref, k2_ref, v0_ref, v1_ref, v2_ref, plane_ref):
    tm = x2_ref.shape[0]
    kv = _dot((_rms_scale(x2_ref[...]) * gkv_ref[...]).astype(BF16), wkv_ref[...])
    k = _rope_paired(kv[:, :D], cos_ref[...], sin_ref[...])
    v = kv[:, D:]
    kt = k.T
    half = HD // 2
    for p in range(NH // 2):
        for j, (head, lo) in enumerate(((2 * p, 0), (2 * p + 1, 0), (2 * p, half), (2 * p + 1, half))):
            kt_ref[0, head, lo:lo + half, :] = kt[p * LANES + j * half:p * LANES + (j + 1) * half, :]
    vt_ref[0] = v.T.reshape(NH, HD, tm)
    dils = [dil for _, dil in ATTN_GROUPS]
    _store_dilated(k, plane_ref, dils, (k0_ref, k1_ref, k2_ref))
    _store_dilated(v, plane_ref, dils, (v0_ref, v1_ref, v2_ref))


def _kvproj(x2, gkv, cos, sin, wkv, b, t):
    n = x2.shape[0]
    tm = TM_POST
    per_b = t // tm
    fixed = lambda i: (0, 0)
    dspecs, dshapes = _dilated_specs(b, t, tm)
    tspec = pl.BlockSpec((1, NH, HD, tm), lambda i: (i // per_b, 0, 0, i % per_b))
    tshape = jax.ShapeDtypeStruct((b, NH, HD, t), F32)
    return pl.pallas_call(
        _kvproj_kernel,
        grid=(n // tm,),
        in_specs=[pl.BlockSpec((tm, D), lambda i: (i, 0)),
                  pl.BlockSpec((1, D), fixed),
                  pl.BlockSpec((tm, LANES), lambda i: (i % per_b, 0)),
                  pl.BlockSpec((tm, LANES), lambda i: (i % per_b, 0)),
                  pl.BlockSpec((D, 2 * D), fixed, pipeline_mode=pl.Buffered(1))],
        out_specs=[tspec, tspec] + dspecs + dspecs,
        out_shape=[tshape, tshape] + dshapes + dshapes,
        scratch_shapes=[pltpu.VMEM((NH // 2, tm, LANES), F32)],
        compiler_params=_cparams(("arbitrary",), 48),
        name="kv_proj",
    )(x2, gkv, cos, sin, wkv)


def _qproj_kernel(x2_ref, modm_ref, gpre1_ref, cos_ref, sin_ref, wq_ref, q0_ref, q1_ref, q2_ref, scr_ref):
    sh1, sc1, _ = _split3(modm_ref[0, 0])
    hm1 = (_rms_scale(x2_ref[...]) * gpre1_ref[...] * (1.0 + sc1) + sh1).astype(BF16)
    q = _rope_paired(_dot(hm1, wq_ref[...]), cos_ref[...], sin_ref[...]) * SCALE
    for g, ((_, dil), q_ref) in enumerate(zip(ATTN_GROUPS, (q0_ref, q1_ref, q2_ref))):
        _store_dilated(q[:, g * D:(g + 1) * D], scr_ref, (dil,), (q_ref,))


def _qproj(x2, mods_mix, gpre1, cos, sin, wq, b, t):
    n = x2.shape[0]
    tm = TM_POST
    per_b = t // tm
    fixed = lambda i: (0, 0)
    dspecs, dshapes = _dilated_specs(b, t, tm)
    return pl.pallas_call(
        _qproj_kernel,
        grid=(n // tm,),
        in_specs=[pl.BlockSpec((tm, D), lambda i: (i, 0)),
                  pl.BlockSpec((1, 1, 1, 3 * D), lambda i: (1, i // per_b, 0, 0)),
                  pl.BlockSpec((1, D), fixed),
                  pl.BlockSpec((tm, LANES), lambda i: (i % per_b, 0)),
                  pl.BlockSpec((tm, LANES), lambda i: (i % per_b, 0)),
                  pl.BlockSpec((D, 3 * D), fixed, pipeline_mode=pl.Buffered(1))],
        out_specs=dspecs,
        out_shape=dshapes,
        scratch_shapes=[pltpu.VMEM((NH // 2, tm, LANES), F32)],
        compiler_params=_cparams(("arbitrary",), 48),
        name="q_proj",
    )(x2, mods_mix, gpre1, cos, sin, wq)


def _attn_kernel(q0_ref, q1_ref, q2_ref, k0_ref, k1_ref, k2_ref, v0_ref, v1_ref, v2_ref,
                 o_ref, og_ref, lg_ref, dg_ref, bias_ref):
    t = o_ref.shape[2]
    lane = lax.broadcasted_iota(I32, (BLK, LANES), 1)
    first_head = lane < HD
    first_head_qk = lane % HD < HD // 2
    nt = (((1,), (1,)), ((), ()))

    @pl.when((pl.program_id(0) == 0) & (pl.program_id(1) == 0))
    def _():
        qi = lax.broadcasted_iota(I32, (2 * BLK, 2 * BLK), 0) % BLK
        kj = lax.broadcasted_iota(I32, (2 * BLK, 2 * BLK), 1)
        for v in range(2):
            dist = v * BLK + qi - kj
            bias_ref[v] = jnp.where((dist >= 0) & (dist <= BLK), 0.0, -jnp.inf)

    groups = zip(ATTN_GROUPS, (q0_ref, q1_ref, q2_ref), (k0_ref, k1_ref, k2_ref), (v0_ref, v1_ref, v2_ref))
    for g, ((win, dil), q_ref, k_ref, v_ref) in enumerate(groups):
        assert win // dil == BLK
        sub = t // dil
        nb = sub // BLK
        kw = min(2 * BLK, sub)

        def scores(u, nb=nb, kw=kw, q_ref=q_ref, k_ref=k_ref):
            r = u // nb
            n = u % nb
            q_lo = pl.multiple_of(n * BLK, BLK)
            k_lo = pl.multiple_of(jnp.maximum(n - 1, 0) * BLK, BLK)
            qp = q_ref[0, 0, r, pl.ds(q_lo, BLK), :]
            zero = jnp.zeros_like(qp)
            qs = jnp.concatenate([jnp.where(first_head_qk, qp, zero), jnp.where(first_head_qk, zero, qp)], axis=0)
            s = lax.dot_general(qs, k_ref[0, 0, r, pl.ds(k_lo, kw), :], nt, preferred_element_type=F32)
            return s + bias_ref[jnp.minimum(n, 1), :, 0:kw], r, q_lo, k_lo

        def finish(s, r, q_lo, k_lo, dil=dil, kw=kw, v_ref=v_ref, g=g):
            m = jnp.max(s, axis=-1, keepdims=True)
            p = jnp.exp(s - m)
            den = jnp.sum(p, axis=-1, keepdims=True)
            pv = _dot(p.astype(BF16), v_ref[0, 0, r, pl.ds(k_lo, kw), :])
            m2 = jnp.broadcast_to(m, (2 * BLK, LANES))
            d2 = jnp.broadcast_to(den, (2 * BLK, LANES))
            rows = pl.ds(q_lo * dil + r, BLK, stride=dil) if dil > 1 else pl.ds(q_lo, BLK)
            og_ref[g, rows, :] = jnp.where(first_head, pv[:BLK], pv[BLK:])
            lg_ref[g, rows, :] = jnp.where(first_head, m2[:BLK], m2[BLK:])
            dg_ref[g, rows, :] = jnp.where(first_head, d2[:BLK], d2[BLK:])

        def trip(i, carry, scores=scores, finish=finish):
            started = [scores(i * ATTN_UNITS + j) for j in range(ATTN_UNITS)]
            for args in started:
                finish(*args)
            return carry

        lax.fori_loop(0, dil * nb // ATTN_UNITS, trip, 0)

    mt = 2 * BLK

    def merge(c, carry):
        rows = pl.ds(pl.multiple_of(c * mt, mt), mt)
        ms = [lg_ref[g, rows, :] for g in range(len(ATTN_GROUPS))]
        mx = jnp.maximum(jnp.maximum(ms[0], ms[1]), ms[2])
        ws = [jnp.exp(m - mx) for m in ms]
        num = ws[0] * og_ref[0, rows, :]
        den = ws[0] * dg_ref[0, rows, :]
        for g in range(1, len(ATTN_GROUPS)):
            num = num + ws[g] * og_ref[g, rows, :]
            den = den + ws[g] * dg_ref[g, rows, :]
        o_ref[0, 0, rows, :] = (num / den).astype(BF16)
        return carry

    lax.fori_loop(0, t // mt, merge, 0)


def _attention(q_l, k_l, v_l, b, t):
    specs = [pl.BlockSpec((1, 1, dil, t // dil, LANES), lambda bb, hp: (bb, hp, 0, 0, 0))
             for _, dil in ATTN_GROUPS]
    return pl.pallas_call(
        _attn_kernel,
        grid=(b, NH // 2),
        in_specs=specs * 3,
        out_specs=pl.BlockSpec((1, 1, t, LANES), lambda bb, hp: (bb, hp, 0, 0)),
        out_shape=jax.ShapeDtypeStruct((b, NH // 2, t, LANES), BF16),
        scratch_shapes=[pltpu.VMEM((len(ATTN_GROUPS), t, LANES), F32)] * 3
        + [pltpu.VMEM((2, 2 * BLK, 2 * BLK), F32)],
        compiler_params=_cparams(("arbitrary", "arbitrary"), 40),
        name="dilated_attn",
    )(*q_l, *k_l, *v_l)


def _router_top2(logits_t):
    idx = lax.broadcasted_iota(I32, logits_t.shape, 0)
    v1 = jnp.max(logits_t, axis=0, keepdims=True)
    i1 = jnp.min(jnp.where(logits_t == v1, idx, N_EXPERTS), axis=0, keepdims=True)
    m1 = idx == i1
    rest = jnp.where(m1, -jnp.inf, logits_t)
    v2 = jnp.max(rest, axis=0, keepdims=True)
    i2 = jnp.min(jnp.where(rest == v2, idx, N_EXPERTS), axis=0, keepdims=True)
    m2 = idx == i2
    e2 = jnp.exp(v2 - v1)
    den = 1.0 + e2
    return jnp.where(m1, 1.0 / den, 0.0) + jnp.where(m2, e2 / den, 0.0)


def _post1_kernel(a_ref, x2_ref, modm_ref, modf_ref, gpost_ref, gpre_ref, wo_ref, rwt_ref, rb_ref,
                  x3_ref, hf_ref, gt_ref):
    o = jnp.concatenate([a_ref[0, hp] for hp in range(NH // 2)], axis=1)
    mix = _dot(o, wo_ref[...])
    _, _, gt = _split3(modm_ref[0, 0])
    x3 = x2_ref[...] + gt * (_rms_scale(mix) * gpost_ref[...])
    x3_ref[...] = x3
    sh, sc, _ = _split3(modf_ref[0, 0])
    hf = _rms_scale(x3) * gpre_ref[...] * (1.0 + sc) + sh
    hf_ref[...] = hf.astype(BF16)
    logits_t = lax.dot_general(rwt_ref[...], hf, (((1,), (1,)), ((), ())),
                               precision=HIGHEST, preferred_element_type=F32) + rb_ref[...]
    gt_ref[...] = _router_top2(logits_t)


def _post1(attn, x2, mods_mix, mods_ffn, gpost, gpre, wo_bf, rwt, rb, t):
    n = x2.shape[0]
    tm = TM_WO
    per_b = t // tm
    row = lambda i: (i, 0)
    fixed = lambda i: (0, 0)
    vec = pl.BlockSpec((1, D), fixed)
    tile = pl.BlockSpec((tm, D), row)
    return pl.pallas_call(
        _post1_kernel,
        grid=(n // tm,),
        in_specs=[pl.BlockSpec((1, NH // 2, tm, LANES), lambda i: (i // per_b, 0, i % per_b, 0)),
                  tile,
                  pl.BlockSpec((1, 1, 1, 3 * D), lambda i: (1, i // per_b, 0, 0)),
                  pl.BlockSpec((1, 1, 1, 3 * D), lambda i: (1, i // per_b, 0, 0)),
                  vec, vec,
                  pl.BlockSpec((D, D), fixed),
                  pl.BlockSpec((N_EXPERTS, D), fixed),
                  pl.BlockSpec((N_EXPERTS, 1), fixed)],
        out_specs=[tile, tile, pl.BlockSpec((N_EXPERTS, tm), lambda i: (0, i))],
        out_shape=[jax.ShapeDtypeStruct((n, D), F32), jax.ShapeDtypeStruct((n, D), BF16),
                   jax.ShapeDtypeStruct((N_EXPERTS, n), F32)],
        compiler_params=_cparams(("arbitrary",), 52),
        name="wo_router",
    )(attn, x2, mods_mix, mods_ffn, gpost, gpre, wo_bf, rwt, rb)


def _route_plan(gates_t, n):
    nb = n // TB
    cnt = jnp.sum((gates_t > 0.0).reshape(N_EXPERTS, nb, TB), axis=-1).astype(I32)
    c_al = (cnt + ROW_ALIGN - 1) // ROW_ALIGN * ROW_ALIGN
    within = jnp.cumsum(c_al, axis=1) - c_al
    tot = jnp.sum(c_al, axis=1)
    region = (tot + SUB + TMX - 1) // TMX * TMX
    start = jnp.cumsum(region) - region
    off = (start[:, None] + within).T.reshape(-1)
    n_used = (jnp.sum(region) // TMX).astype(I32)
    tile_row = jnp.arange(_max_sorted_rows(n) // TMX, dtype=I32) * TMX
    tile_e = jnp.sum(tile_row[:, None] >= (start + region)[None, :], axis=1).astype(I32)
    tile_e = jnp.minimum(tile_e, N_EXPERTS - 1)
    last_e = tile_e[jnp.maximum(n_used - 1, 0)]
    tile_e = jnp.where(jnp.arange(tile_e.shape[0]) < n_used, tile_e, last_e)
    return off.astype(I32), cnt.T.reshape(-1), tile_e, n_used.reshape(1)


def _max_sorted_rows(n):
    nb = n // TB
    worst = 2 * n + nb * N_EXPERTS * (ROW_ALIGN - 1) + N_EXPERTS * (SUB + TMX - 1)
    return (worst + TMX - 1) // TMX * TMX


def _dispatch_kernel(off_ref, cnt_ref, hf_ref, gt_ref, xs_in_ref, gs_in_ref, xs_ref, gs_ref,
                     stage_ref, gstage_ref, sem, gsem, *, block0):
    del xs_in_ref, gs_in_ref
    b = pl.program_id(0) + block0
    hf = hf_ref[...]
    gate = gt_ref[...]
    sel = gate > 0.0
    before = lax.broadcasted_iota(I32, (TB, TB), 0) < lax.broadcasted_iota(I32, (TB, TB), 1)
    rank = _dot(sel.astype(BF16), before.astype(BF16)).astype(I32)
    jj = lax.broadcasted_iota(I32, (SUB, TB), 0)

    def copies(e, s):
        k = e * NS + s
        dst = pl.ds(pl.multiple_of(off_ref[b * N_EXPERTS + e] + s * SUB, ROW_ALIGN), SUB)
        return (pltpu.make_async_copy(stage_ref.at[k], xs_ref.at[dst], sem.at[k]),
                pltpu.make_async_copy(gstage_ref.at[k], gs_ref.at[dst], gsem.at[k]))

    for e in range(N_EXPERTS):
        c = cnt_ref[b * N_EXPERTS + e]
        for s in range(NS):
            @pl.when(s * SUB < c)
            def _(e=e, s=s):
                hit = (rank[e:e + 1, :] == jj + s * SUB) & sel[e:e + 1, :]
                stage_ref[e * NS + s] = _dot(hit.astype(BF16), hf).astype(BF16)
                g_rows = jnp.sum(jnp.where(hit, gate[e:e + 1, :], 0.0), axis=1, keepdims=True)
                gstage_ref[e * NS + s] = jnp.broadcast_to(g_rows, (SUB, LANES))
                for cp in copies(e, s):
                    cp.start()
    for e in range(N_EXPERTS):
        c = cnt_ref[b * N_EXPERTS + e]
        for s in range(NS):
            @pl.when(s * SUB < c)
            def _(e=e, s=s):
                for cp in copies(e, s):
                    cp.wait()


def _dispatch(off, cnt, hf_bf, gates_t, xs, gs, block0):
    n = hf_bf.shape[0]
    rows = xs.shape[0]
    return pl.pallas_call(
        functools.partial(_dispatch_kernel, block0=block0),
        grid_spec=pltpu.PrefetchScalarGridSpec(
            num_scalar_prefetch=2,
            grid=(n // TB,),
            in_specs=[pl.BlockSpec((TB, D), lambda b, o, c: (b, 0)),
                      pl.BlockSpec((N_EXPERTS, TB), lambda b, o, c: (0, b)),
                      pl.BlockSpec(memory_space=pl.ANY),
                      pl.BlockSpec(memory_space=pl.ANY)],
            out_specs=[pl.BlockSpec(memory_space=pl.ANY), pl.BlockSpec(memory_space=pl.ANY)],
            scratch_shapes=[pltpu.VMEM((N_EXPERTS * NS, SUB, D), BF16),
                            pltpu.VMEM((N_EXPERTS * NS, SUB, LANES), F32),
                            pltpu.SemaphoreType.DMA((N_EXPERTS * NS,)),
                            pltpu.SemaphoreType.DMA((N_EXPERTS * NS,))]),
        out_shape=[jax.ShapeDtypeStruct((rows, D), BF16), jax.ShapeDtypeStruct((rows, LANES), F32)],
        input_output_aliases={4: 0, 5: 1},
        compiler_params=_cparams(("arbitrary",), 40),
        name="moe_dispatch",
    )(off, cnt, hf_bf, gates_t, xs, gs)


def _expert_kernel(te_ref, nu_ref, x_ref, g_ref, w1_ref, w3_ref, w2_ref, y_ref, acc_ref):
    del te_ref
    i, j = pl.program_id(0), pl.program_id(1)

    @pl.when(i < nu_ref[0])
    def _():
        x = x_ref[...]
        parts = [slice(p * FCX_SUB, (p + 1) * FCX_SUB) for p in range(FCX // FCX_SUB)]
        pre = [(_dot(x, w1_ref[0, :, p].astype(BF16)), _dot(x, w3_ref[0, :, p].astype(BF16))) for p in parts]
        contrib = None
        for p, (h1, h3) in zip(parts, pre):
            part = _dot((_silu(h1) * h3).astype(BF16), w2_ref[0, p, :].astype(BF16))
            contrib = part if contrib is None else contrib + part

        @pl.when(j == 0)
        def _():
            acc_ref[...] = contrib

        @pl.when(j > 0)
        def _():
            acc_ref[...] += contrib

        @pl.when(j == pl.num_programs(1) - 1)
        def _():
            y_ref[...] = (g_ref[:, 0:1] * acc_ref[...]).astype(BF16)

    @pl.when((i >= nu_ref[0]) & (j == 0))
    def _():
        y_ref[...] = jnp.zeros(y_ref.shape, BF16)


def _experts(tile_e, n_used, xs, gs, w1, w3, w2):
    rows = xs.shape[0]
    dff = w1.shape[2]
    nj = dff // FCX

    def tile_idx(i, j, te, nu):
        return (jnp.minimum(i, nu[0] - 1), 0)

    def chunk(i, j, nu):
        return jnp.where(i < nu[0], j, nj - 1)

    return pl.pallas_call(
        _expert_kernel,
        grid_spec=pltpu.PrefetchScalarGridSpec(
            num_scalar_prefetch=2,
            grid=(rows // TMX, nj),
            in_specs=[pl.BlockSpec((TMX, D), tile_idx),
                      pl.BlockSpec((TMX, LANES), tile_idx),
                      pl.BlockSpec((1, D, FCX), lambda i, j, te, nu: (te[i], 0, chunk(i, j, nu))),
                      pl.BlockSpec((1, D, FCX), lambda i, j, te, nu: (te[i], 0, chunk(i, j, nu))),
                      pl.BlockSpec((1, FCX, D), lambda i, j, te, nu: (te[i], chunk(i, j, nu), 0))],
            out_specs=pl.BlockSpec((TMX, D), lambda i, j, te, nu: (i, 0)),
            scratch_shapes=[pltpu.VMEM((TMX, D), F32)]),
        out_shape=jax.ShapeDtypeStruct((rows, D), BF16),
        compiler_params=_cparams(("arbitrary", "arbitrary"), 56),
        name="moe_experts",
    )(tile_e, n_used, xs, gs, w1, w3, w2)


SLOTS = 2 * TB // SUB + N_EXPERTS
SLOTS_FEW = 13


def _gather_expert_rows(b, first_step, off_ref, cnt_ref, gt_ref, ys_ref, ybuf_ref, p_ref, sem, acc_ref):
    @pl.when(first_step)
    def _():
        ybuf_ref[...] = jnp.zeros(ybuf_ref.shape, BF16)

    def copy(e, s, slot):
        src = pl.multiple_of(off_ref[b * N_EXPERTS + e] + s * SUB, ROW_ALIGN)
        dst = pl.multiple_of(slot * SUB, SUB)
        return pltpu.make_async_copy(ys_ref.at[pl.ds(src, SUB)], ybuf_ref.at[pl.ds(dst, SUB)], sem.at[slot])

    pairs = [(e, s) for e in range(N_EXPERTS) for s in range(NS)]
    used, slots = [], []
    slot = jnp.int32(0)
    for e, s in pairs:
        u = s * SUB < cnt_ref[b * N_EXPERTS + e]
        used.append(u)
        slots.append(slot)
        slot = slot + u.astype(I32)
    for (e, s), u, k in zip(pairs, used, slots):
        @pl.when(u)
        def _(e=e, s=s, k=k):
            copy(e, s, k).start()

    sel = gt_ref[...] > 0.0
    before = lax.broadcasted_iota(I32, (TB, TB), 0) < lax.broadcasted_iota(I32, (TB, TB), 1)
    rank = _dot(sel.astype(BF16), before.astype(BF16)).astype(I32)
    jj = lax.broadcasted_iota(I32, (SUB, TB), 0)
    p_ref[...] = jnp.zeros(p_ref.shape, BF16)
    for (e, s), u, k in zip(pairs, used, slots):
        @pl.when(u)
        def _(e=e, s=s, k=k):
            hit = (rank[e:e + 1, :] == jj + s * SUB) & sel[e:e + 1, :]
            p_ref[pl.ds(pl.multiple_of(k * SUB, SUB), SUB), :] = hit.astype(BF16)
    for (e, s), u, k in zip(pairs, used, slots):
        @pl.when(u)
        def _(e=e, s=s, k=k):
            copy(e, s, k).wait()

    tn = (((0,), (0,)), ((), ()))
    few = SLOTS_FEW * SUB

    @pl.when(slot <= SLOTS_FEW)
    def _():
        acc_ref[...] = lax.dot_general(p_ref[0:few, :], ybuf_ref[0:few, :], tn, preferred_element_type=F32)

    @pl.when(slot > SLOTS_FEW)
    def _():
        acc_ref[...] = lax.dot_general(p_ref[...], ybuf_ref[...], tn, preferred_element_type=F32)

    return acc_ref[...]


_COMBINE_SCRATCH = [pltpu.VMEM((SLOTS * SUB, D), BF16),
                    pltpu.VMEM((SLOTS * SUB, TB), BF16),
                    pltpu.SemaphoreType.DMA((SLOTS,)),
                    pltpu.VMEM((TB, D), F32)]


def _combine_kernel(off_ref, cnt_ref, gt_ref, x3_ref, mod_ref, gpost_ref, ys_ref,
                    out_ref, ybuf_ref, p_ref, sem, acc_ref):
    b = pl.program_id(0)
    moe = _gather_expert_rows(b, b == 0, off_ref, cnt_ref, gt_ref, ys_ref, ybuf_ref, p_ref, sem, acc_ref)
    _, _, gt = _split3(mod_ref[0, 0])
    out_ref[...] = x3_ref[...] + gt * (_rms_scale(moe) * gpost_ref[...])


def _combine(off, cnt, gates_t, x3, mods_ffn, gpost, ys, t):
    n = x3.shape[0]
    per_b = t // TB
    return pl.pallas_call(
        _combine_kernel,
        grid_spec=pltpu.PrefetchScalarGridSpec(
            num_scalar_prefetch=2,
            grid=(n // TB,),
            in_specs=[pl.BlockSpec((N_EXPERTS, TB), lambda b, o, c: (0, b)),
                      pl.BlockSpec((TB, D), lambda b, o, c: (b, 0)),
                      pl.BlockSpec((1, 1, 1, 3 * D), lambda b, o, c: (1, b // per_b, 0, 0)),
                      pl.BlockSpec((1, D), lambda b, o, c: (0, 0)),
                      pl.BlockSpec(memory_space=pl.ANY)],
            out_specs=pl.BlockSpec((TB, D), lambda b, o, c: (b, 0)),
            scratch_shapes=_COMBINE_SCRATCH),
        out_shape=jax.ShapeDtypeStruct((n, D), F32),
        compiler_params=_cparams(("arbitrary",), 48),
        name="moe_combine",
    )(off, cnt, gates_t, x3, mods_ffn, gpost, ys)


def _combine_rows_kernel(off_ref, cnt_ref, gt_ref, ys_ref, out_ref, ybuf_ref, p_ref, sem, acc_ref, *, block0):
    out_ref[...] = _gather_expert_rows(block0, True, off_ref, cnt_ref, gt_ref, ys_ref, ybuf_ref, p_ref, sem, acc_ref)


def _combine_rows(off, cnt, gates_t, ys, block0):
    return pl.pallas_call(
        functools.partial(_combine_rows_kernel, block0=block0),
        grid_spec=pltpu.PrefetchScalarGridSpec(
            num_scalar_prefetch=2,
            grid=(1,),
            in_specs=[pl.BlockSpec((N_EXPERTS, TB), lambda b, o, c: (0, 0)),
                      pl.BlockSpec(memory_space=pl.ANY)],
            out_specs=pl.BlockSpec((TB, D), lambda b, o, c: (0, 0)),
            scratch_shapes=_COMBINE_SCRATCH),
        out_shape=jax.ShapeDtypeStruct((TB, D), F32),
        compiler_params=_cparams(("arbitrary",), 48),
        name="moe_combine_rows",
    )(off, cnt, gates_t, ys)


def _s_mix0_kernel(x_ref, st_ref, modm_ref, modf_ref, gpre_ref, gpost_ref, gpref_ref, pw_ref, ps_ref,
                   x1_ref, pool_ref, hf_ref, *, pos):
    sh, sc, gt = _split3(modm_ref[...])
    x = x_ref[...]
    hm = _rms_scale(x) * gpre_ref[...] * (1.0 + sc) + sh
    parts = []
    for g, w in enumerate(POOL_WINDOWS):
        lo = g * PG
        tok = hm[:, lo:lo + PG]
        s = tok
        for j in range(1, w):
            s = s + st_ref[POOL_STATE - j, :, lo:lo + PG]
        cnt = float(min(w, pos + 1))
        parts.append(_dot_hi(s / cnt - tok, pw_ref[g]))
    mix = jnp.concatenate(parts, axis=1) * ps_ref[...]
    x1 = x + gt * (_rms_scale(mix) * gpost_ref[...])
    x1_ref[...] = x1
    for j in range(POOL_STATE - 1):
        pool_ref[j] = st_ref[j + 1]
    pool_ref[POOL_STATE - 1] = hm
    shf, scf, _ = _split3(modf_ref[...])
    hf_ref[...] = _rms_scale(x1) * gpref_ref[...] * (1.0 + scf) + shf


def _s_mix0(x_s, state_t, modm, modf, gpre, gpost, gpref, pw, ps, pos):
    rows = x_s.shape[0]
    return pl.pallas_call(
        functools.partial(_s_mix0_kernel, pos=pos),
        out_shape=[jax.ShapeDtypeStruct((rows, D), F32),
                   jax.ShapeDtypeStruct((POOL_STATE, rows, D), F32),
                   jax.ShapeDtypeStruct((rows, D), F32)],
        compiler_params=pltpu.CompilerParams(vmem_limit_bytes=40 * MIB),
        name="s_mix0",
    )(x_s, state_t, modm, modf, gpre, gpost, gpref, pw, ps)


def _s_ffn_kernel(h_ref, w1_ref, w3_ref, w2_ref, f_ref):
    j = pl.program_id(0)
    h = h_ref[...]
    contrib = _dot_hi(_silu(_dot_hi(h, w1_ref[...])) * _dot_hi(h, w3_ref[...]), w2_ref[...])

    @pl.when(j == 0)
    def _():
        f_ref[...] = contrib

    @pl.when(j > 0)
    def _():
        f_ref[...] += contrib


def _s_ffn(h, w1, w3, w2):
    rows = h.shape[0]
    dff = w1.shape[1]
    fc = 256
    return pl.pallas_call(
        _s_ffn_kernel,
        grid=(dff // fc,),
        in_specs=[pl.BlockSpec((rows, D), lambda j: (0, 0)),
                  pl.BlockSpec((D, fc), lambda j: (0, j)),
                  pl.BlockSpec((D, fc), lambda j: (0, j)),
                  pl.BlockSpec((fc, D), lambda j: (j, 0))],
        out_specs=pl.BlockSpec((rows, D), lambda j: (0, 0)),
        out_shape=jax.ShapeDtypeStruct((rows, D), F32),
        compiler_params=_cparams(("arbitrary",), 32),
        name="s_ffn0",
    )(h, w1, w3, w2)


def _s_post0_kernel(x1_ref, f_ref, modf_ref, modm_ref, gpost_ref, gkv_ref, gpre1_ref,
                    x2_ref, kvin_ref, hm1_ref):
    _, _, gt = _split3(modf_ref[...])
    x2 = x1_ref[...] + gt * (_rms_scale(f_ref[...]) * gpost_ref[...])
    x2_ref[...] = x2
    xn = _rms_scale(x2)
    kvin_ref[...] = xn * gkv_ref[...]
    sh1, sc1, _ = _split3(modm_ref[...])
    hm1_ref[...] = xn * gpre1_ref[...] * (1.0 + sc1) + sh1


def _s_post0(x1, f, modf, modm, gpost, gkv, gpre1):
    rows = x1.shape[0]
    return pl.pallas_call(
        _s_post0_kernel,
        out_shape=[jax.ShapeDtypeStruct((rows, D), F32)] * 3,
        name="s_post0",
    )(x1, f, modf, modm, gpost, gkv, gpre1)


def _s_linear_kernel(h_ref, w_ref, cos_ref, sin_ref, o_ref):
    o_ref[...] = _rope_cols(_dot_hi(h_ref[...], w_ref[...]), cos_ref[...], sin_ref[...])


def _s_linear_plain_kernel(h_ref, w_ref, o_ref):
    o_ref[...] = _dot_hi(h_ref[...], w_ref[...])


def _s_linear(h, w, rope=None):
    rows, k = h.shape
    nw = w.shape[1]
    cw = 512
    in_specs = [pl.BlockSpec((rows, k), lambda j: (0, 0)), pl.BlockSpec((k, cw), lambda j: (0, j))]
    args = [h, w]
    body = _s_linear_plain_kernel
    if rope is not None:
        body = _s_linear_kernel
        in_specs += [pl.BlockSpec((1, cw), lambda j: (0, j)), pl.BlockSpec((1, cw), lambda j: (0, j))]
        args += [rope[0], rope[1]]
    return pl.pallas_call(
        body,
        grid=(nw // cw,),
        in_specs=in_specs,
        out_specs=pl.BlockSpec((rows, cw), lambda j: (0, j)),
        out_shape=jax.ShapeDtypeStruct((rows, nw), F32),
        compiler_params=_cparams(("arbitrary",), 32),
        name="s_linear",
    )(*args)


def _s_attn_kernel(qkv_ref, kc_ref, vc_ref, o_ref):
    n_past = kc_ref.shape[3]
    hb = kc_ref.shape[1]
    ng = len(ATTN_GROUPS)
    nt = (((1,), (1,)), ((), ()))
    eye = (lax.broadcasted_iota(I32, (LANES, LANES), 0) == lax.broadcasted_iota(I32, (LANES, LANES), 1)).astype(F32)
    tiles = qkv_ref[0].reshape(hb * 8, HD)
    if hb * 8 < LANES:
        tiles = jnp.concatenate([tiles, jnp.zeros((LANES - hb * 8, HD), F32)], axis=0)
    cols = lax.dot_general(eye[:HD, :HD], tiles, nt, precision=HIGHEST, preferred_element_type=F32)
    col = lambda h, j: cols[:, 8 * h + j:8 * h + j + 1]
    outs, lses = [], []
    for g, (win, dil) in enumerate(ATTN_GROUPS):
        lo = n_past - win
        dist = win - lax.broadcasted_iota(I32, (1, win), 1)
        s_c = jnp.concatenate([jnp.sum(kc_ref[0, h, :, lo:] * col(h, g), axis=0, keepdims=True)
                               for h in range(hb)], axis=0) * SCALE
        s_c = jnp.where((dist & (dil - 1)) == 0, s_c, -jnp.inf)
        s_n = jnp.concatenate([jnp.sum(col(h, ng) * col(h, g), axis=0, keepdims=True)
                               for h in range(hb)], axis=0) * SCALE
        m = jnp.maximum(jnp.max(s_c, axis=1, keepdims=True), s_n)
        p_c = jnp.exp(s_c - m)
        p_n = jnp.exp(s_n - m)
        den = jnp.sum(p_c, axis=1, keepdims=True) + p_n
        outs.append([(jnp.sum(vc_ref[0, h, :, lo:] * p_c[h:h + 1], axis=1, keepdims=True)
                      + p_n[h:h + 1] * col(h, ng + 1)) / den[h:h + 1] for h in range(hb)])
        lses.append(m + jnp.log(den))
    mx = jnp.maximum(jnp.maximum(lses[0], lses[1]), lses[2])
    es = [jnp.exp(l - mx) for l in lses]
    tot = es[0] + es[1] + es[2]
    lane = lax.broadcasted_iota(I32, (1, LANES), 1)
    out_cols = jnp.zeros((HD, LANES), F32)
    for h in range(hb):
        out = (es[0] / tot)[h:h + 1] * outs[0][h]
        for g in range(1, ng):
            out = out + (es[g] / tot)[h:h + 1] * outs[g][h]
        out_cols = out_cols + out * (lane == h).astype(F32)
    o_ref[0] = lax.dot_general(eye[:hb], out_cols, nt, precision=HIGHEST, preferred_element_type=F32)


def _s_attn(qkv, cache_kt, cache_vt):
    rows, _, _, n_past = cache_kt.shape
    for win, dil in ATTN_GROUPS:
        assert n_past >= win and win % LANES == 0 and dil & (dil - 1) == 0
    cache = pl.BlockSpec((1, HB_S, HD, n_past), lambda b, h: (b, h, 0, 0))
    return pl.pallas_call(
        _s_attn_kernel,
        grid=(rows, NH // HB_S),
        in_specs=[pl.BlockSpec((1, HB_S, 8, HD), lambda b, h: (b, h, 0, 0)), cache, cache],
        out_specs=pl.BlockSpec((1, HB_S, HD), lambda b, h: (b, h, 0)),
        out_shape=jax.ShapeDtypeStruct((rows, NH, HD), F32),
        compiler_params=_cparams(("arbitrary", "arbitrary"), 40),
        name="s_attn",
    )(qkv, cache_kt, cache_vt)


def _s_post1_kernel(x2_ref, mix_ref, modm_ref, modf_ref, gpost_ref, gpre_ref, rwt_ref, rb_ref,
                    x3_ref, hf_ref, gc_ref):
    _, _, gt = _split3(modm_ref[...])
    x3 = x2_ref[...] + gt * (_rms_scale(mix_ref[...]) * gpost_ref[...])
    x3_ref[...] = x3
    sh, sc, _ = _split3(modf_ref[...])
    hf = _rms_scale(x3) * gpre_ref[...] * (1.0 + sc) + sh
    hf_ref[...] = hf
    logits_t = lax.dot_general(rwt_ref[...], hf, (((1,), (1,)), ((), ())),
                               precision=HIGHEST, preferred_element_type=F32) + rb_ref[...]
    gc_ref[...] = _router_top2(logits_t)


def _s_post1(x2, mix, modm, modf, gpost, gpre, rwt, rb):
    rows = x2.shape[0]
    return pl.pallas_call(
        _s_post1_kernel,
        out_shape=[jax.ShapeDtypeStruct((rows, D), F32), jax.ShapeDtypeStruct((rows, D), F32),
                   jax.ShapeDtypeStruct((N_EXPERTS, rows), F32)],
        name="s_post1",
    )(x2, mix, modm, modf, gpost, gpre, rwt, rb)


def _s_post2_kernel(x3_ref, f_ref, modf_ref, gpost_ref, y_ref):
    _, _, gt = _split3(modf_ref[...])
    y_ref[...] = x3_ref[...] + gt * (_rms_scale(f_ref[...]) * gpost_ref[...])


def _s_post2(x3, f, modf, gpost):
    return pl.pallas_call(
        _s_post2_kernel,
        out_shape=jax.ShapeDtypeStruct(x3.shape, F32),
        name="s_post2",
    )(x3, f, modf, gpost)


def _rope_tables(pos, paired=False):
    half = HD // 2
    inv = ROPE_THETA ** (-jnp.arange(half, dtype=F32) / half)
    ang = pos.astype(F32)[:, None] * inv[None, :]
    cos, sin = jnp.cos(ang), jnp.sin(ang)
    cos = jnp.tile(cos, (1, LANES // half))
    if paired:
        sin = jnp.concatenate([-sin, -sin, sin, sin], axis=1)
    else:
        sin = jnp.tile(jnp.concatenate([-sin, sin], axis=1), (1, LANES // HD))
    return cos, sin


def kernel(x_prompt, x_sample, state_pool, cache_k, cache_v, c_prompt, c_sample, ada_mix_w, ada_mix_b, g_pre_mix, g_post_mix, pool_w, pool_scale, g_kv, w_kv, w_q, w_o, ada_ffn_w, ada_ffn_b, g_pre_ffn, g_post_ffn, ffn_w1, ffn_w3, ffn_w2, router_w, router_b, moe_w1, moe_w3, moe_w2):
    b, t, _ = x_prompt.shape
    nsmp = x_sample.shape[0]
    n = b * t
    past_len = 16384
    assert x_sample.shape[1] == 1 and g_pre_mix.shape[0] == 2 and t % (ATTN_GROUPS[-1][1] * BLK) == 0
    vec = lambda a: a.reshape(1, D)

    c_all = jnp.concatenate([c_prompt, c_sample], axis=0)
    mods_mix = _ada(c_all, ada_mix_w, ada_mix_b)
    mods_ffn = _ada(c_all, ada_ffn_w, ada_ffn_b)
    pm_mix = mods_mix[:, :b].reshape(2, b, 1, 3 * D)
    pm_ffn = mods_ffn[:, :b].reshape(2, b, 1, 3 * D)
    sm_mix = mods_mix[:, b:]
    sm_ffn = mods_ffn[:, b:]

    rwt = router_w[0].T
    rb = router_b[0].reshape(N_EXPERTS, 1)

    x1, pool16 = _mix0(x_prompt, pm_mix, vec(g_pre_mix[0]), vec(g_post_mix[0]),
                       pool_w[0].astype(BF16), vec(pool_scale[0]))
    cos_p, sin_p = _rope_tables(jnp.arange(t, dtype=jnp.int32), paired=True)
    x2 = _ffn0(x1.reshape(n, D), pm_ffn, vec(g_pre_ffn[0]), vec(g_post_ffn[0]),
               ffn_w1[0].astype(BF16), ffn_w3[0].astype(BF16), ffn_w2[0].astype(BF16), t)
    wkv_bf = jnp.concatenate([_pair_heads(w_kv[:, :D]), w_kv[:, D:]], axis=1).astype(BF16)
    kt_p, vt_p, *kv_l = _kvproj(x2, vec(g_kv), cos_p, sin_p, wkv_bf, b, t)
    q_l = _qproj(x2, pm_mix, vec(g_pre_mix[1]), cos_p, sin_p, _pair_heads(w_q[0]).astype(BF16), b, t)
    attn = _attention(q_l, kv_l[:3], kv_l[3:], b, t)
    x3, hf1, gates_t = _post1(attn, x2, pm_mix, pm_ffn, vec(g_post_mix[1]),
                              vec(g_pre_ffn[1]), w_o[0].astype(BF16), rwt, rb, t)

    state_t = jnp.transpose(state_pool[:, 0], (1, 0, 2))
    x1s, pool_t, hf0s = _s_mix0(x_sample.reshape(nsmp, D), state_t, sm_mix[0], sm_ffn[0],
                                vec(g_pre_mix[0]), vec(g_post_mix[0]), vec(g_pre_ffn[0]),
                                pool_w[0], vec(pool_scale[0]), past_len)
    f0s = _s_ffn(hf0s, ffn_w1[0], ffn_w3[0], ffn_w2[0])
    x2s, kvin_s, hm1s = _s_post0(x1s, f0s, sm_ffn[0], sm_mix[1], vec(g_post_ffn[0]), vec(g_kv), vec(g_pre_mix[1]))
    cos_s, sin_s = _rope_tables(jnp.full((1,), past_len, jnp.int32))
    ones, zeros = jnp.ones((1, D), F32), jnp.zeros((1, D), F32)
    kv_s = _s_linear(kvin_s, w_kv, (jnp.concatenate([jnp.tile(cos_s, (1, D // LANES)), ones], axis=1),
                                    jnp.concatenate([jnp.tile(sin_s, (1, D // LANES)), zeros], axis=1)))
    q_s = _s_linear(hm1s, w_q[0], (jnp.tile(cos_s, (1, 3 * D // LANES)), jnp.tile(sin_s, (1, 3 * D // LANES))))
    k_s = kv_s[:, :D].reshape(nsmp, NH, HD)
    v_s = kv_s[:, D:].reshape(nsmp, NH, HD)
    cache_kt = jnp.transpose(cache_k, (0, 2, 3, 1))
    cache_vt = jnp.transpose(cache_v, (0, 2, 3, 1))
    qkv = jnp.concatenate([jnp.transpose(q_s.reshape(nsmp, len(ATTN_GROUPS), NH, HD), (0, 2, 1, 3)),
                           k_s[:, :, None], v_s[:, :, None]], axis=2)
    qkv = jnp.pad(qkv, ((0, 0), (0, 0), (0, 8 - qkv.shape[2]), (0, 0)))
    attn_s = _s_attn(qkv, cache_kt, cache_vt).reshape(nsmp, D)
    mix1s = _s_linear(attn_s, w_o[0])
    x3s, hf1s, gts = _s_post1(x2s, mix1s, sm_mix[1], sm_ffn[1], vec(g_post_mix[1]), vec(g_pre_ffn[1]), rwt, rb)

    hf_blk = jnp.zeros((TB, D), BF16).at[:nsmp].set(hf1s.astype(BF16))
    gt_blk = jnp.zeros((N_EXPERTS, TB), F32).at[:, :nsmp].set(gts)
    off, cnt, tile_e, n_used = _route_plan(jnp.concatenate([gates_t, gt_blk], axis=1), n + TB)
    xs = jnp.zeros((_max_sorted_rows(n + TB), D), BF16)
    gs = jnp.zeros((xs.shape[0], LANES), F32)
    xs, gs = _dispatch(off, cnt, hf1, gates_t, xs, gs, 0)
    xs, gs = _dispatch(off, cnt, hf_blk, gt_blk, xs, gs, n // TB)
    ys = _experts(tile_e, n_used, xs, gs, moe_w1[0], moe_w3[0], moe_w2[0])
    y_prompt = _combine(off, cnt, gates_t, x3, pm_ffn, vec(g_post_ffn[1]), ys, t).reshape(b, t, D)
    moe_s = _combine_rows(off, cnt, gt_blk, ys, n // TB)[:nsmp]
    y_sample = _s_post2(x3s, moe_s, sm_ffn[1], vec(g_post_ffn[1])).reshape(nsmp, 1, D)

    pool_prompt = pool16[:, None, 1:, :]
    pool_sample = jnp.transpose(pool_t, (1, 0, 2))[:, None]
    return (y_prompt, y_sample, pool_prompt, pool_sample,
            jnp.transpose(kt_p, (0, 3, 1, 2)), jnp.transpose(vt_p, (0, 3, 1, 2)),
            k_s[:, None], v_s[:, None])
```
